```python
import math
import jax, jax.numpy as jnp
from jax import lax
import numpy as np

D_MODEL = 1024
BATCH = 1
SEQ = 16384
DEPTH = 4

ROPE_THETA = 500000.0
NORM_EPS = 1e-6
LN_EPS = 1e-5
Q_BLOCK = 128
N_EVEN = (DEPTH + 1) // 2
N_ODD = DEPTH // 2

A_WIDTH = D_MODEL // 2
A_QK_DIM = 64
A_HEADS = A_WIDTH // (2 * A_QK_DIM)
A_V_DIM = 2 * A_QK_DIM
A_ROT = A_QK_DIM // 4

B_WIDTH = D_MODEL // 2
B_GROUPS = 4
B_GROUP_CH = B_WIDTH // B_GROUPS
B_CHUNK = 128

EVEN_IN = 3 * A_WIDTH + 2 * B_WIDTH

C_WIDTH = D_MODEL // 2
C_WINDOWS = (2, 4, 8, 16)
C_GROUPS = len(C_WINDOWS)
C_GROUP_CH = C_WIDTH // C_GROUPS

D_HEADS = 8
D_NOPE = 64
D_ROPE = 32
D_V = 64
D_Q_RANK = 384
D_KV_RANK = 256
D_WIDTH = D_HEADS * D_V

ODD_IN = C_WIDTH + D_Q_RANK + D_KV_RANK + D_ROPE

FFN_DENSE = 2816
N_EXPERTS = 8
TOP_K = 2
FFN_EXPERT = 3584

kernel_name = "hybrid_diffattn_sgu_pool_mla_moe"


def _rmsnorm(x, g):
    xf = x.astype(jnp.float32)
    y = xf * lax.rsqrt(jnp.mean(xf * xf, axis=-1, keepdims=True) + NORM_EPS)
    return (y * g.astype(jnp.float32)).astype(x.dtype)


def _layernorm(x, g, b):
    xf = x.astype(jnp.float32)
    mu = jnp.mean(xf, axis=-1, keepdims=True)
    var = jnp.mean(jnp.square(xf - mu), axis=-1, keepdims=True)
    y = (xf - mu) * lax.rsqrt(var + LN_EPS)
    return (y * g.astype(jnp.float32) + b.astype(jnp.float32)).astype(x.dtype)


def _rope_tables(seq, rot_dim):
    pos = jnp.arange(seq, dtype=jnp.float32)
    inv = ROPE_THETA ** (-jnp.arange(0, rot_dim, 2, dtype=jnp.float32) / rot_dim)
    ang = pos[:, None] * inv[None, :]
    return jnp.cos(ang), jnp.sin(ang)


def _apply_rope(x, cos, sin, rot_dim):
    half = rot_dim // 2
    shape = (x.shape[1],) + (1,) * (x.ndim - 3) + (half,)
    c = cos.reshape(shape)
    s = sin.reshape(shape)
    xf = x.astype(jnp.float32)
    x1 = xf[..., :half]
    x2 = xf[..., half:rot_dim]
    out = jnp.concatenate([x1 * c - x2 * s, x2 * c + x1 * s, xf[..., rot_dim:]], axis=-1)
    return out.astype(x.dtype)


def _to_blocks(t):
    b, h, s, d = t.shape
    return t.reshape(b, h, s // Q_BLOCK, Q_BLOCK, d).transpose(2, 0, 1, 3, 4)


def _from_blocks(o):
    nb, b, h, qb, d = o.shape
    return o.transpose(1, 0, 3, 2, 4).reshape(b, nb * qb, h, d)


def _block_causal_probs(q_blk, k, q_start, scale):
    s = jnp.einsum('bhqd,bhkd->bhqk', q_blk, k).astype(jnp.float32) * scale
    q_pos = q_start + jnp.arange(q_blk.shape[2])
    k_pos = jnp.arange(k.shape[2])
    s = jnp.where(k_pos[None, :] <= q_pos[:, None], s, -jnp.inf)
    return jax.nn.softmax(s, axis=-1)


def _diff_attention(q, k, v, lam, scale):
    nb = q.shape[3] // Q_BLOCK
    starts = jnp.arange(nb) * Q_BLOCK
    k1 = k[:, :, 0]
    k2 = k[:, :, 1]

    def blk(args):
        q1b, q2b, st = args
        p = _block_causal_probs(q1b, k1, st, scale) - lam * _block_causal_probs(q2b, k2, st, scale)
        return jnp.einsum('bhqk,bhkd->bhqd', p.astype(v.dtype), v)

    return _from_blocks(lax.map(blk, (_to_blocks(q[:, :, 0]), _to_blocks(q[:, :, 1]), starts)))


def _causal_attention(q, k, v, scale):
    nb = q.shape[2] // Q_BLOCK
    starts = jnp.arange(nb) * Q_BLOCK

    def blk(args):
        qb, st = args
        p = _block_causal_probs(qb, k, st, scale)
        return jnp.einsum('bhqk,bhkd->bhqd', p.astype(v.dtype), v)

    return _from_blocks(lax.map(blk, (_to_blocks(q), starts)))


def _even_mixer(h, layer_idx, w_in, lam_q1, lam_k1, lam_q2, lam_k2, subln_g,
                sgu_ln_g, sgu_ln_b, sgu_w, sgu_b, w_out, cos_a, sin_a):
    b, s, _ = h.shape
    proj = h @ w_in
    qa, ka, va, ub, vb = jnp.split(
        proj, [A_WIDTH, 2 * A_WIDTH, 3 * A_WIDTH, 3 * A_WIDTH + B_WIDTH], axis=-1)

    qa = _apply_rope(qa.reshape(b, s, A_HEADS, 2, A_QK_DIM), cos_a, sin_a, A_ROT).transpose(0, 2, 3, 1, 4)
    ka = _apply_rope(ka.reshape(b, s, A_HEADS, 2, A_QK_DIM), cos_a, sin_a, A_ROT).transpose(0, 2, 3, 1, 4)
    va = va.reshape(b, s, A_HEADS, A_V_DIM).transpose(0, 2, 1, 3)
    lam_init = 0.8 - 0.6 * math.exp(-0.3 * layer_idx)
    f32 = jnp.float32
    lam = (jnp.exp(jnp.sum(lam_q1.astype(f32) * lam_k1.astype(f32)))
           - jnp.exp(jnp.sum(lam_q2.astype(f32) * lam_k2.astype(f32))) + lam_init)
    oa = _diff_attention(qa, ka, va, lam, A_QK_DIM ** -0.5)
    oa = (_rmsnorm(oa, subln_g) * (1.0 - lam_init)).reshape(b, s, A_WIDTH)

    ub = jax.nn.gelu(ub)
    vb = _layernorm(jax.nn.gelu(vb), sgu_ln_g, sgu_ln_b)
    n = s // B_CHUNK
    vb = vb.reshape(b, n, B_CHUNK, B_GROUPS, B_GROUP_CH)
    causal = jnp.tril(jnp.ones((B_CHUNK, B_CHUNK), dtype=bool))
    ws = jnp.where(causal[None], sgu_w, 0)
    mixed = jnp.einsum('gpq,bnqgc->bnpgc', ws, vb) + sgu_b.T[None, None, :, :, None]
    ob = (ub.reshape(b, n, B_CHUNK, B_GROUPS, B_GROUP_CH) * mixed).reshape(b, s, B_WIDTH)

    return jnp.concatenate([oa, ob], axis=-1) @ w_out


def _odd_mixer(h, w_in, pool_w, pool_scale, q_norm_g, w_uq, kv_norm_g, w_ukv, w_out, cos_d, sin_d):
    b, s, _ = h.shape
    proj = h @ w_in
    xc, cq, ckv, kpe = jnp.split(
        proj, [C_WIDTH, C_WIDTH + D_Q_RANK, C_WIDTH + D_Q_RANK + D_KV_RANK], axis=-1)

    csum = jnp.pad(jnp.cumsum(xc.astype(jnp.float32), axis=1), ((0, 0), (1, 0), (0, 0)))
    pos = jnp.arange(s)
    groups = []
    for g, w in enumerate(C_WINDOWS):
        sl = slice(g * C_GROUP_CH, (g + 1) * C_GROUP_CH)
        lo = jnp.maximum(pos + 1 - w, 0)
        win_sum = csum[:, 1:, sl] - csum[:, lo, sl]
        cnt = jnp.minimum(pos + 1, w).astype(jnp.float32)[None, :, None]
        groups.append(win_sum / cnt - xc[..., sl].astype(jnp.float32))
    pooled = jnp.stack(groups, axis=2).astype(h.dtype)
    oc = jnp.einsum('bsgc,gcd->bsgd', pooled, pool_w).reshape(b, s, C_WIDTH) * pool_scale

    q = (_rmsnorm(cq, q_norm_g) @ w_uq).reshape(b, s, D_HEADS, D_NOPE + D_ROPE)
    q = jnp.concatenate([q[..., :D_NOPE], _apply_rope(q[..., D_NOPE:], cos_d, sin_d, D_ROPE)], axis=-1)
    kv = (_rmsnorm(ckv, kv_norm_g) @ w_ukv).reshape(b, s, D_HEADS, D_NOPE + D_V)
    kpe = _apply_rope(kpe[:, :, None, :], cos_d, sin_d, D_ROPE)
    k = jnp.concatenate([kv[..., :D_NOPE], jnp.broadcast_to(kpe, (b, s, D_HEADS, D_ROPE))], axis=-1)
    v = kv[..., D_NOPE:]
    od = _causal_attention(q.transpose(0, 2, 1, 3), k.transpose(0, 2, 1, 3), v.transpose(0, 2, 1, 3),
                           (D_NOPE + D_ROPE) ** -0.5).reshape(b, s, D_WIDTH)

    return jnp.concatenate([oc, od], axis=-1) @ w_out


def _swiglu(x, wg, wu, wd):
    return (jax.nn.silu(x @ wg) * (x @ wu)) @ wd


def _moe(x, router, wg, wu, wd):
    logits = (x @ router).astype(jnp.float32)
    top_val, top_idx = lax.top_k(logits, TOP_K)
    top_w = jax.nn.softmax(top_val, axis=-1)
    gate = jnp.sum(jax.nn.one_hot(top_idx, N_EXPERTS, dtype=jnp.float32) * top_w[..., None], axis=-2)
    y = jnp.zeros_like(x)
    for e in range(N_EXPERTS):
        y = y + gate[..., e:e + 1].astype(x.dtype) * _swiglu(x, wg[e], wu[e], wd[e])
    return y


def setup_inputs(seed: int = 0) -> dict:
    key = jax.random.key(seed)
    ks = iter(jax.random.split(key, 40))
    f32 = jnp.float32

    def nrm(shape, scale):
        return jax.random.normal(next(ks), shape, f32) * scale

    def gain(shape):
        return 1.0 + 0.02 * jax.random.normal(next(ks), shape, f32)

    D = D_MODEL
    return {
        "x": nrm((BATCH, SEQ, D), 1.0),
        "norm_mix_even": gain((N_EVEN, D)),
        "w_in_even": nrm((N_EVEN, D, EVEN_IN), D ** -0.5),
        "lam_q1": nrm((N_EVEN, A_QK_DIM), 0.1),
        "lam_k1": nrm((N_EVEN, A_QK_DIM), 0.1),
        "lam_q2": nrm((N_EVEN, A_QK_DIM), 0.1),
        "lam_k2": nrm((N_EVEN, A_QK_DIM), 0.1),
        "subln_g": gain((N_EVEN, A_V_DIM)),
        "sgu_ln_g": gain((N_EVEN, B_WIDTH)),
        "sgu_ln_b": nrm((N_EVEN, B_WIDTH), 0.02),
        "sgu_w": nrm((N_EVEN, B_GROUPS, B_CHUNK, B_CHUNK), B_CHUNK ** -0.5),
        "sgu_b": gain((N_EVEN, B_GROUPS, B_CHUNK)),
        "w_out_even": nrm((N_EVEN, A_WIDTH + B_WIDTH, D), (A_WIDTH + B_WIDTH) ** -0.5),
        "norm_ffn_even": gain((N_EVEN, D)),
        "ffn_wg": nrm((N_EVEN, D, FFN_DENSE), D ** -0.5),
        "ffn_wu": nrm((N_EVEN, D, FFN_DENSE), D ** -0.5),
        "ffn_wd": nrm((N_EVEN, FFN_DENSE, D), FFN_DENSE ** -0.5),
        "norm_mix_odd": gain((N_ODD, D)),
        "w_in_odd": nrm((N_ODD, D, ODD_IN), D ** -0.5),
        "pool_w": nrm((N_ODD, C_GROUPS, C_GROUP_CH, C_GROUP_CH), C_GROUP_CH ** -0.5),
        "pool_scale": 1.0 + 0.1 * jax.random.normal(next(ks), (N_ODD, C_WIDTH), f32),
        "q_norm_g": gain((N_ODD, D_Q_RANK)),
        "w_uq": nrm((N_ODD, D_Q_RANK, D_HEADS * (D_NOPE + D_ROPE)), D_Q_RANK ** -0.5),
        "kv_norm_g": gain((N_ODD, D_KV_RANK)),
        "w_ukv": nrm((N_ODD, D_KV_RANK, D_HEADS * (D_NOPE + D_V)), D_KV_RANK ** -0.5),
        "w_out_odd": nrm((N_ODD, C_WIDTH + D_WIDTH, D), (C_WIDTH + D_WIDTH) ** -0.5),
        "norm_ffn_odd": gain((N_ODD, D)),
        "router": nrm((N_ODD, D, N_EXPERTS), D ** -0.5),
        "moe_wg": nrm((N_ODD, N_EXPERTS, D, FFN_EXPERT), D ** -0.5),
        "moe_wu": nrm((N_ODD, N_EXPERTS, D, FFN_EXPERT), D ** -0.5),
        "moe_wd": nrm((N_ODD, N_EXPERTS, FFN_EXPERT, D), FFN_EXPERT ** -0.5),
        "final_norm": gain((D,)),
    }


def reference(x, norm_mix_even, w_in_even, lam_q1, lam_k1, lam_q2, lam_k2, subln_g,
              sgu_ln_g, sgu_ln_b, sgu_w, sgu_b, w_out_even, norm_ffn_even, ffn_wg, ffn_wu, ffn_wd,
              norm_mix_odd, w_in_odd, pool_w, pool_scale, q_norm_g, w_uq, kv_norm_g, w_ukv,
              w_out_odd, norm_ffn_odd, router, moe_wg, moe_wu, moe_wd, final_norm):
    s = x.shape[1]
    cos_a, sin_a = _rope_tables(s, A_ROT)
    cos_d, sin_d = _rope_tables(s, D_ROPE)
    h = x
    for l in range(DEPTH):
        i = l // 2
        if l % 2 == 0:
            h = h + _even_mixer(_rmsnorm(h, norm_mix_even[i]), l, w_in_even[i],
                                lam_q1[i], lam_k1[i], lam_q2[i], lam_k2[i], subln_g[i],
                                sgu_ln_g[i], sgu_ln_b[i], sgu_w[i], sgu_b[i], w_out_even[i],
                                cos_a, sin_a)
            h = h + _swiglu(_rmsnorm(h, norm_ffn_even[i]), ffn_wg[i], ffn_wu[i], ffn_wd[i])
        else:
            h = h + _odd_mixer(_rmsnorm(h, norm_mix_odd[i]), w_in_odd[i], pool_w[i], pool_scale[i],
                               q_norm_g[i], w_uq[i], kv_norm_g[i], w_ukv[i], w_out_odd[i],
                               cos_d, sin_d)
            h = h + _moe(_rmsnorm(h, norm_ffn_odd[i]), router[i], moe_wg[i], moe_wu[i], moe_wd[i])
    return _rmsnorm(h, final_norm)
```

```python
import functools
import math

import jax
import jax.numpy as jnp
from jax import lax
from jax.experimental import pallas as pl
from jax.experimental.pallas import tpu as pltpu

F32 = jnp.float32
BF16 = jnp.bfloat16

ROPE_THETA = 500000.0
NORM_EPS = 1e-6
LN_EPS = 1e-5
LOG2E = 1.4426950408889634

D_MODEL = 1024
A_HEADS = 4
A_QK_DIM = 64
A_ROT = 16
A_WIDTH = 512
B_WIDTH = 512
B_GROUPS = 4
B_CHUNK = 128
C_WIDTH = 512
C_WINDOWS = (2, 4, 8, 16)
C_HALO = 16
D_HEADS = 8
D_NOPE = 64
D_ROPE = 32
D_V = 64
D_Q_RANK = 384
D_KV_RANK = 256
D_HEAD_PAD = 128
N_EXPERTS = 8

LANES = 128
ROW_TILE = 512
ATTN_TILE = 512
MOE_TILE = 512
VMEM_LIMIT = 56 * 1024 * 1024


def _cparams(*sem):
    return pltpu.CompilerParams(dimension_semantics=sem, vmem_limit_bytes=VMEM_LIMIT)


def _rms(x, g):
    return x * lax.rsqrt(jnp.mean(x * x, axis=-1, keepdims=True) + NORM_EPS) * g


def _dot(a, b):
    return jnp.dot(a, b, preferred_element_type=F32)


def _dot_nt(a, b):
    return lax.dot_general(a, b, (((1,), (1,)), ((), ())), preferred_element_type=F32)


def _rope_block(x, cosf, sins, shift, lo_mask):
    up = pltpu.roll(x, LANES - shift, 1)
    dn = pltpu.roll(x, shift, 1)
    return x * cosf + jnp.where(lo_mask, up, dn) * sins


def _rope_wide(x, cosf, sins, shift, lo_mask):
    nb = x.shape[1] // LANES
    return jnp.concatenate(
        [_rope_block(x[:, b * LANES:(b + 1) * LANES], cosf, sins, shift, lo_mask) for b in range(nb)],
        axis=1)


def _even_in_kernel(h_ref, g_ref, w_ref, cos_ref, sin_ref, lng_ref, lnb_ref, sw_ref, sbt_ref,
                    q_ref, k_ref, v_ref, ob_ref):
    tm = h_ref.shape[0]
    xn = _rms(h_ref[...], g_ref[...]).astype(BF16)
    cosf = cos_ref[...]
    sins = sin_ref[...]
    lane = lax.broadcasted_iota(jnp.int32, (tm, LANES), 1)
    lo_mask = (lane % A_QK_DIM) < (A_ROT // 2)

    q = _dot(xn, w_ref[:, 0:A_WIDTH])
    q = _rope_wide(q, cosf, sins, A_ROT // 2, lo_mask) * (A_QK_DIM ** -0.5 * LOG2E)
    q_ref[...] = q.astype(BF16)
    k = _dot(xn, w_ref[:, A_WIDTH:2 * A_WIDTH])
    k_ref[...] = _rope_wide(k, cosf, sins, A_ROT // 2, lo_mask).astype(BF16)
    v_ref[...] = _dot(xn, w_ref[:, 2 * A_WIDTH:3 * A_WIDTH]).astype(BF16)

    ub = jax.nn.gelu(_dot(xn, w_ref[:, 3 * A_WIDTH:3 * A_WIDTH + B_WIDTH]))
    vb = jax.nn.gelu(_dot(xn, w_ref[:, 3 * A_WIDTH + B_WIDTH:3 * A_WIDTH + 2 * B_WIDTH]))
    mu = jnp.mean(vb, axis=-1, keepdims=True)
    var = jnp.mean(jnp.square(vb - mu), axis=-1, keepdims=True)
    vn = ((vb - mu) * lax.rsqrt(var + LN_EPS) * lng_ref[...] + lnb_ref[...]).astype(BF16)

    row = lax.broadcasted_iota(jnp.int32, (B_CHUNK, B_CHUNK), 0)
    col = lax.broadcasted_iota(jnp.int32, (B_CHUNK, B_CHUNK), 1)
    causal = col <= row
    gc = B_WIDTH // B_GROUPS
    for g in range(B_GROUPS):
        wg = jnp.where(causal, sw_ref[g], 0.0).astype(BF16)
        bias = sbt_ref[:, g:g + 1]
        for c in range(tm // B_CHUNK):
            rs = slice(c * B_CHUNK, (c + 1) * B_CHUNK)
            cs = slice(g * gc, (g + 1) * gc)
            mixed = _dot(wg, vn[rs, cs]) + bias
            ob_ref[rs, cs] = (ub[rs, cs] * mixed).astype(BF16)


def _even_in(h, g, w, cosf, sins, lng, lnb, sgu_w, sgu_bt):
    s = h.shape[0]
    tm = min(ROW_TILE, s)
    n_in = w.shape[1]
    row_spec = lambda width: pl.BlockSpec((tm, width), lambda i: (i, 0))
    full = lambda shape: pl.BlockSpec(shape, lambda i: (0,) * len(shape))
    out = jax.ShapeDtypeStruct((s, A_WIDTH), BF16)
    return pl.pallas_call(
        _even_in_kernel,
        grid=(s // tm,),
        in_specs=[row_spec(D_MODEL), full((1, D_MODEL)), full((D_MODEL, n_in)),
                  row_spec(LANES), row_spec(LANES), full((1, B_WIDTH)), full((1, B_WIDTH)),
                  full((B_GROUPS, B_CHUNK, B_CHUNK)), full((B_CHUNK, B_GROUPS))],
        out_specs=[row_spec(A_WIDTH)] * 4,
        out_shape=[out] * 4,
        compiler_params=_cparams("parallel"),
        name="even_in",
    )(h, g, w, cosf, sins, lng, lnb, sgu_w, sgu_bt)


def _softmax_step(s, v, m_ref, l_ref, acc_ref, rows):
    m_prev = m_ref[rows, :]
    m_new = jnp.maximum(m_prev, jnp.max(s, axis=1, keepdims=True))
    alpha = jnp.exp2(m_prev - m_new)
    p = jnp.exp2(s - m_new)
    l_ref[rows, :] = alpha * l_ref[rows, :] + jnp.sum(p, axis=1, keepdims=True)
    acc_ref[rows, :] = alpha * acc_ref[rows, :] + _dot(p.astype(BF16), v)
    m_ref[rows, :] = m_new


def _diff_attn_kernel(q_ref, k_ref, v_ref, lq1_ref, lk1_ref, lq2_ref, lk2_ref, sg_ref, o_ref,
                      m_ref, l_ref, acc_ref, *, lam_init):
    t = q_ref.shape[0]
    qi = pl.program_id(1)
    q = q_ref[...]
    lane = lax.broadcasted_iota(jnp.int32, q.shape, 1)
    zero = jnp.zeros_like(q)
    q2 = jnp.concatenate([jnp.where(lane < A_QK_DIM, q, zero), jnp.where(lane >= A_QK_DIM, q, zero)], axis=0)
    m_ref[...] = jnp.full(m_ref.shape, -jnp.inf, F32)
    l_ref[...] = jnp.zeros(l_ref.shape, F32)
    acc_ref[...] = jnp.zeros(acc_ref.shape, F32)
    rows = slice(0, 2 * t)

    def body(j, carry):
        off = pl.multiple_of(j * t, t)
        s = _dot_nt(q2, k_ref[pl.ds(off, t), :])
        _softmax_step(s, v_ref[pl.ds(off, t), :], m_ref, l_ref, acc_ref, rows)
        return carry

    lax.fori_loop(0, qi, body, 0)

    off = pl.multiple_of(qi * t, t)
    s = _dot_nt(q2, k_ref[pl.ds(off, t), :])
    r = lax.broadcasted_iota(jnp.int32, s.shape, 0) % t
    c = lax.broadcasted_iota(jnp.int32, s.shape, 1)
    s = jnp.where(c <= r, s, -jnp.inf)
    _softmax_step(s, v_ref[pl.ds(off, t), :], m_ref, l_ref, acc_ref, rows)

    lam = (jnp.exp(jnp.sum(lq1_ref[...] * lk1_ref[...], axis=-1, keepdims=True))
           - jnp.exp(jnp.sum(lq2_ref[...] * lk2_ref[...], axis=-1, keepdims=True)) + lam_init)
    o = acc_ref[0:t, :] / l_ref[0:t, :] - lam * (acc_ref[t:2 * t, :] / l_ref[t:2 * t, :])
    o_ref[...] = (_rms(o, sg_ref[...]) * (1.0 - lam_init)).astype(BF16)


def _diff_attn(q, k, v, lq1, lk1, lq2, lk2, subln_g, lam_init):
    s = q.shape[0]
    t = min(ATTN_TILE, s)
    hw = 2 * A_QK_DIM
    small = lambda n: pl.BlockSpec((1, n), lambda h, i: (0, 0))
    return pl.pallas_call(
        functools.partial(_diff_attn_kernel, lam_init=lam_init),
        grid=(A_HEADS, s // t),
        in_specs=[pl.BlockSpec((t, hw), lambda h, i: (i, h)),
                  pl.BlockSpec((s, hw), lambda h, i: (0, h)),
                  pl.BlockSpec((s, hw), lambda h, i: (0, h)),
                  small(A_QK_DIM), small(A_QK_DIM), small(A_QK_DIM), small(A_QK_DIM), small(hw)],
        out_specs=pl.BlockSpec((t, hw), lambda h, i: (i, h)),
        out_shape=jax.ShapeDtypeStruct((s, A_WIDTH), BF16),
        scratch_shapes=[pltpu.VMEM((2 * t, 1), F32), pltpu.VMEM((2 * t, 1), F32),
                        pltpu.VMEM((2 * t, hw), F32)],
        compiler_params=_cparams("parallel", "parallel"),
        name="diff_attn",
    )(q, k, v, lq1, lk1, lq2, lk2, subln_g)


def _mla_attn_kernel(q_ref, k_ref, v_ref, o_ref, m_ref, l_ref, acc_ref):
    t = q_ref.shape[0]
    qi = pl.program_id(1)
    qa = q_ref[:, 0:D_HEAD_PAD]
    qb = q_ref[:, D_HEAD_PAD:2 * D_HEAD_PAD]
    m_ref[...] = jnp.full(m_ref.shape, -jnp.inf, F32)
    l_ref[...] = jnp.zeros(l_ref.shape, F32)
    acc_ref[...] = jnp.zeros(acc_ref.shape, F32)
    ra = slice(0, t)
    rb = slice(t, 2 * t)

    def body(j, carry):
        off = pl.multiple_of(j * t, t)
        v = v_ref[pl.ds(off, t), :]
        _softmax_step(_dot_nt(qa, k_ref[pl.ds(off, t), 0:D_HEAD_PAD]), v, m_ref, l_ref, acc_ref, ra)
        _softmax_step(_dot_nt(qb, k_ref[pl.ds(off, t), D_HEAD_PAD:2 * D_HEAD_PAD]), v, m_ref, l_ref, acc_ref, rb)
        return carry

    lax.fori_loop(0, qi, body, 0)

    off = pl.multiple_of(qi * t, t)
    r = lax.broadcasted_iota(jnp.int32, (t, t), 0)
    c = lax.broadcasted_iota(jnp.int32, (t, t), 1)
    keep = c <= r
    v = v_ref[pl.ds(off, t), :]
    sa = jnp.where(keep, _dot_nt(qa, k_ref[pl.ds(off, t), 0:D_HEAD_PAD]), -jnp.inf)
    _softmax_step(sa, v, m_ref, l_ref, acc_ref, ra)
    sb = jnp.where(keep, _dot_nt(qb, k_ref[pl.ds(off, t), D_HEAD_PAD:2 * D_HEAD_PAD]), -jnp.inf)
    _softmax_step(sb, v, m_ref, l_ref, acc_ref, rb)

    oa = acc_ref[0:t, :] / l_ref[0:t, :]
    ob = acc_ref[t:2 * t, :] / l_ref[t:2 * t, :]
    lane = lax.broadcasted_iota(jnp.int32, oa.shape, 1)
    o_ref[...] = jnp.where(lane < D_V, oa, ob).astype(BF16)


def _mla_attn(q, k, v):
    s = q.shape[0]
    t = min(ATTN_TILE, s)
    return pl.pallas_call(
        _mla_attn_kernel,
        grid=(D_HEADS // 2, s // t),
        in_specs=[pl.BlockSpec((t, 2 * D_HEAD_PAD), lambda h, i: (i, h)),
                  pl.BlockSpec((s, 2 * D_HEAD_PAD), lambda h, i: (0, h)),
                  pl.BlockSpec((s, 2 * D_V), lambda h, i: (0, h))],
        out_specs=pl.BlockSpec((t, 2 * D_V), lambda h, i: (i, h)),
        out_shape=jax.ShapeDtypeStruct((s, D_HEADS * D_V), BF16),
        scratch_shapes=[pltpu.VMEM((2 * t, 1), F32), pltpu.VMEM((2 * t, 1), F32),
                        pltpu.VMEM((2 * t, 2 * D_V), F32)],
        compiler_params=_cparams("parallel", "parallel"),
        name="mla_attn",
    )(q, k, v)


def _out_proj_kernel(h_ref, a_ref, b_ref, w_ref, o_ref):
    half = a_ref.shape[1]
    o_ref[...] = h_ref[...] + _dot(a_ref[...], w_ref[0:half, :]) + _dot(b_ref[...], w_ref[half:2 * half, :])


def _out_proj(h, a, b, w):
    s = h.shape[0]
    tm = min(ROW_TILE, s)
    half = a.shape[1]
    return pl.pallas_call(
        _out_proj_kernel,
        grid=(s // tm,),
        in_specs=[pl.BlockSpec((tm, D_MODEL), lambda i: (i, 0)),
                  pl.BlockSpec((tm, half), lambda i: (i, 0)),
                  pl.BlockSpec((tm, half), lambda i: (i, 0)),
                  pl.BlockSpec((2 * half, D_MODEL), lambda i: (0, 0))],
        out_specs=pl.BlockSpec((tm, D_MODEL), lambda i: (i, 0)),
        out_shape=jax.ShapeDtypeStruct((s, D_MODEL), F32),
        compiler_params=_cparams("parallel"),
        name="out_proj",
    )(h, a, b, w)


def _dense_ffn_kernel(h_ref, g_ref, wg_ref, wu_ref, wd_ref, o_ref, xn_ref):
    f = pl.program_id(1)

    @pl.when(f == 0)
    def _():
        x = h_ref[...]
        xn_ref[...] = _rms(x, g_ref[...]).astype(BF16)
        o_ref[...] = x

    xn = xn_ref[...]
    a = jax.nn.silu(_dot(xn, wg_ref[...])) * _dot(xn, wu_ref[...])
    o_ref[...] += _dot(a.astype(BF16), wd_ref[...])


def _dense_ffn(h, g, wg, wu, wd, f_tile):
    s = h.shape[0]
    tm = min(ROW_TILE, s)
    ffn = wg.shape[1]
    return pl.pallas_call(
        _dense_ffn_kernel,
        grid=(s // tm, ffn // f_tile),
        in_specs=[pl.BlockSpec((tm, D_MODEL), lambda i, f: (i, 0)),
                  pl.BlockSpec((1, D_MODEL), lambda i, f: (0, 0)),
                  pl.BlockSpec((D_MODEL, f_tile), lambda i, f: (0, f)),
                  pl.BlockSpec((D_MODEL, f_tile), lambda i, f: (0, f)),
                  pl.BlockSpec((f_tile, D_MODEL), lambda i, f: (f, 0))],
        out_specs=pl.BlockSpec((tm, D_MODEL), lambda i, f: (i, 0)),
        out_shape=jax.ShapeDtypeStruct((s, D_MODEL), F32),
        scratch_shapes=[pltpu.VMEM((tm, D_MODEL), BF16)],
        compiler_params=_cparams("parallel", "arbitrary"),
        name="dense_ffn",
    )(h, g, wg, wu, wd)


def _expert_ffn_kernel(te_ref, nu_ref, x_ref, wg_ref, wu_ref, wd_ref, o_ref, xb_ref):
    r = pl.program_id(0)
    f = pl.program_id(1)

    @pl.when(r < nu_ref[0])
    def _():
        @pl.when(f == 0)
        def _():
            xb_ref[...] = x_ref[...].astype(BF16)
            o_ref[...] = jnp.zeros(o_ref.shape, F32)

        xb = xb_ref[...]
        a = jax.nn.silu(_dot(xb, wg_ref[0])) * _dot(xb, wu_ref[0])
        o_ref[...] += _dot(a.astype(BF16), wd_ref[0])

    @pl.when(jnp.logical_and(r >= nu_ref[0], f == 0))
    def _():
        o_ref[...] = jnp.zeros(o_ref.shape, F32)


def _expert_ffn(tile_expert, n_used, xs, wg, wu, wd, f_tile):
    rows = xs.shape[0]
    tm = MOE_TILE
    ffn = wg.shape[2]
    grid_spec = pltpu.PrefetchScalarGridSpec(
        num_scalar_prefetch=2,
        grid=(rows // tm, ffn // f_tile),
        in_specs=[pl.BlockSpec((tm, D_MODEL), lambda r, f, te, nu: (jnp.minimum(r, nu[0] - 1), 0)),
                  pl.BlockSpec((1, D_MODEL, f_tile), lambda r, f, te, nu: (te[r], 0, jnp.where(r < nu[0], f, 0))),
                  pl.BlockSpec((1, D_MODEL, f_tile), lambda r, f, te, nu: (te[r], 0, jnp.where(r < nu[0], f, 0))),
                  pl.BlockSpec((1, f_tile, D_MODEL), lambda r, f, te, nu: (te[r], jnp.where(r < nu[0], f, 0), 0))],
        out_specs=pl.BlockSpec((tm, D_MODEL), lambda r, f, te, nu: (r, 0)),
        scratch_shapes=[pltpu.VMEM((tm, D_MODEL), BF16)],
    )
    return pl.pallas_call(
        _expert_ffn_kernel,
        grid_spec=grid_spec,
        out_shape=jax.ShapeDtypeStruct((rows, D_MODEL), F32),
        compiler_params=_cparams("parallel", "arbitrary"),
        name="expert_ffn",
    )(tile_expert, n_used, xs, wg, wu, wd)


def _odd_in_kernel(h_ref, g_ref, w_ref, wkpe_ref, cos_ref, sin_ref, pw_ref, ps_ref,
                   qg_ref, wuq_ref, kvg_ref, wk_ref, wv_ref,
                   oc_ref, q_ref, k_ref, v_ref, halo_ref):
    tm = h_ref.shape[0]
    i = pl.program_id(0)
    xn = _rms(h_ref[...], g_ref[...]).astype(BF16)
    cosf = cos_ref[...]
    sins = sin_ref[...]
    lane = lax.broadcasted_iota(jnp.int32, (tm, LANES), 1)
    lo_mask = lane < (D_NOPE + D_ROPE // 2)

    @pl.when(i == 0)
    def _():
        halo_ref[...] = jnp.zeros(halo_ref.shape, F32)

    xc = _dot(xn, w_ref[:, 0:C_WIDTH])
    ext = jnp.concatenate([halo_ref[...], xc], axis=0)
    halo_ref[...] = xc[tm - C_HALO:tm, :]
    pos = i * tm + lax.broadcasted_iota(jnp.int32, (tm, 1), 0)
    gc = C_WIDTH // len(C_WINDOWS)
    for g, win in enumerate(C_WINDOWS):
        cs = slice(g * gc, (g + 1) * gc)
        a = ext[:, cs]
        span = 1
        while span < win:
            n = a.shape[0] - span
            a = a[span:span + n, :] + a[0:n, :]
            span *= 2
        start = C_HALO - (win - 1)
        wsum = a[start:start + tm, :]
        cnt = jnp.minimum(pos + 1, win).astype(F32)
        pooled = (wsum / cnt - xc[:, cs]).astype(BF16)
        oc_ref[:, cs] = (_dot(pooled, pw_ref[g]) * ps_ref[:, cs]).astype(BF16)

    cq = _dot(xn, w_ref[:, C_WIDTH:C_WIDTH + D_Q_RANK])
    q = _dot(_rms(cq, qg_ref[...]).astype(BF16), wuq_ref[...])
    q = _rope_wide(q, cosf, sins, D_ROPE // 2, lo_mask) * ((D_NOPE + D_ROPE) ** -0.5 * LOG2E)
    q_ref[...] = q.astype(BF16)

    ckv = _dot(xn, w_ref[:, C_WIDTH + D_Q_RANK:C_WIDTH + D_Q_RANK + D_KV_RANK])
    ckvn = _rms(ckv, kvg_ref[...]).astype(BF16)
    kpe = _rope_block(_dot(xn, wkpe_ref[...]), cosf, sins, D_ROPE // 2, lo_mask)
    kn = _dot(ckvn, wk_ref[...])
    k_ref[...] = jnp.concatenate(
        [kn[:, b * LANES:(b + 1) * LANES] + kpe for b in range(D_HEADS)], axis=1).astype(BF16)
    v_ref[...] = _dot(ckvn, wv_ref[...]).astype(BF16)


def _odd_in(h, g, w, wkpe, cosf, sins, pool_w, pool_scale, qg, wuq, kvg, wk, wv):
    s = h.shape[0]
    tm = min(ROW_TILE, s)
    row_spec = lambda width: pl.BlockSpec((tm, width), lambda i: (i, 0))
    full = lambda shape: pl.BlockSpec(shape, lambda i: (0,) * len(shape))
    return pl.pallas_call(
        _odd_in_kernel,
        grid=(s // tm,),
        in_specs=[row_spec(D_MODEL), full((1, D_MODEL)), full(w.shape), full(wkpe.shape),
                  row_spec(LANES), row_spec(LANES), full(pool_w.shape), full((1, C_WIDTH)),
                  full((1, D_Q_RANK)), full(wuq.shape), full((1, D_KV_RANK)), full(wk.shape), full(wv.shape)],
        out_specs=[row_spec(C_WIDTH), row_spec(D_HEADS * D_HEAD_PAD), row_spec(D_HEADS * D_HEAD_PAD),
                   row_spec(D_HEADS * D_V)],
        out_shape=[jax.ShapeDtypeStruct((s, C_WIDTH), BF16),
                   jax.ShapeDtypeStruct((s, D_HEADS * D_HEAD_PAD), BF16),
                   jax.ShapeDtypeStruct((s, D_HEADS * D_HEAD_PAD), BF16),
                   jax.ShapeDtypeStruct((s, D_HEADS * D_V), BF16)],
        scratch_shapes=[pltpu.VMEM((C_HALO, C_WIDTH), F32)],
        compiler_params=_cparams("arbitrary"),
        name="odd_in",
    )(h, g, w, wkpe, cosf, sins, pool_w, pool_scale, qg, wuq, kvg, wk, wv)


def _router_kernel(h_ref, g_ref, rt_ref, xn_ref, eidx_ref, rank_ref, wcol_ref, cnt_ref, run_ref):
    tm = h_ref.shape[0]
    i = pl.program_id(0)

    @pl.when(i == 0)
    def _():
        run_ref[...] = jnp.zeros(run_ref.shape, F32)

    xn = _rms(h_ref[...], g_ref[...])
    xn_ref[...] = xn
    lg = lax.dot_general(rt_ref[...], xn, (((1,), (1,)), ((), ())),
                         preferred_element_type=F32, precision=lax.Precision.HIGHEST)
    eio = lax.broadcasted_iota(jnp.int32, lg.shape, 0)
    m1 = jnp.max(lg, axis=0, keepdims=True)
    i1 = jnp.min(jnp.where(lg == m1, eio, N_EXPERTS), axis=0, keepdims=True)
    lg2 = jnp.where(eio == i1, -jnp.inf, lg)
    m2 = jnp.max(lg2, axis=0, keepdims=True)
    i2 = jnp.min(jnp.where(lg2 == m2, eio, N_EXPERTS), axis=0, keepdims=True)
    e = jnp.exp(m2 - m1)
    w1 = 1.0 / (1.0 + e)
    w2 = e / (1.0 + e)

    sel1 = eio == i1
    sel2 = eio == i2
    onehot = jnp.logical_or(sel1, sel2)
    tr = lax.broadcasted_iota(jnp.int32, (tm, tm), 0)
    tc = lax.broadcasted_iota(jnp.int32, (tm, tm), 1)
    upper = (tr < tc).astype(BF16)
    before = _dot(onehot.astype(BF16), upper) + run_ref[:, 0:1]
    r1 = jnp.sum(jnp.where(sel1, before, 0.0), axis=0, keepdims=True)
    r2 = jnp.sum(jnp.where(sel2, before, 0.0), axis=0, keepdims=True)
    run_new = run_ref[:, 0:1] + jnp.sum(onehot.astype(F32), axis=1, keepdims=True)
    run_ref[...] = jnp.broadcast_to(run_new, run_ref.shape)
    cnt_ref[...] = jnp.broadcast_to(run_new, cnt_ref.shape).astype(jnp.int32)

    eidx_ref[...] = jnp.concatenate([i1, i2], axis=0)
    rank_ref[...] = jnp.concatenate([r1, r2], axis=0).astype(jnp.int32)
    w8 = jnp.concatenate([w1, w2, jnp.zeros((N_EXPERTS - 2, tm), F32)], axis=0)
    eye = (tr == tc).astype(F32)
    wcol_ref[...] = lax.dot_general(eye, w8, (((1,), (1,)), ((), ())),
                                    preferred_element_type=F32, precision=lax.Precision.HIGHEST)


def _router(h, g, router_t):
    s = h.shape[0]
    tm = min(ROW_TILE, s)
    return pl.pallas_call(
        _router_kernel,
        grid=(s // tm,),
        in_specs=[pl.BlockSpec((tm, D_MODEL), lambda i: (i, 0)),
                  pl.BlockSpec((1, D_MODEL), lambda i: (0, 0)),
                  pl.BlockSpec((N_EXPERTS, D_MODEL), lambda i: (0, 0))],
        out_specs=[pl.BlockSpec((tm, D_MODEL), lambda i: (i, 0)),
                   pl.BlockSpec((2, tm), lambda i: (0, i)),
                   pl.BlockSpec((2, tm), lambda i: (0, i)),
                   pl.BlockSpec((tm, N_EXPERTS), lambda i: (i, 0)),
                   pl.BlockSpec((N_EXPERTS, LANES), lambda i: (0, 0))],
        out_shape=[jax.ShapeDtypeStruct((s, D_MODEL), F32),
                   jax.ShapeDtypeStruct((2, s), jnp.int32),
                   jax.ShapeDtypeStruct((2, s), jnp.int32),
                   jax.ShapeDtypeStruct((s, N_EXPERTS), F32),
                   jax.ShapeDtypeStruct((N_EXPERTS, LANES), jnp.int32)],
        scratch_shapes=[pltpu.VMEM((N_EXPERTS, LANES), F32)],
        compiler_params=_cparams("arbitrary"),
        name="router",
    )(h, g, router_t)


def _row_copy(src, dst, sem):
    return pltpu.make_async_copy(src, dst, sem)


def _dispatch_kernel(pad_ref, nu_ref, pos_ref, x_hbm, xs_hbm, zero_ref, sem, zsem):
    i = pl.program_id(0)
    tm = pos_ref.shape[1]
    zt = zero_ref.shape[0]

    @pl.when(i == 0)
    def _():
        zero_ref[...] = jnp.zeros(zero_ref.shape, F32)
        zrow = zero_ref.at[pl.ds(0, 1)]
        for e in range(N_EXPERTS):
            lo = pad_ref[0, e]
            hi = pad_ref[1, e]

            def zbody(r, carry):
                _row_copy(zrow, xs_hbm.at[pl.ds(r, 1)], zsem).start()
                return carry

            def zwait(r, carry):
                _row_copy(zrow, xs_hbm.at[pl.ds(r, 1)], zsem).wait()
                return carry

            lax.fori_loop(lo, hi, zbody, 0)
            lax.fori_loop(lo, hi, zwait, 0)

        def tbody(r, carry):
            _row_copy(zero_ref, xs_hbm.at[pl.ds(pl.multiple_of(r * zt, zt), zt)], zsem).start()
            return carry

        def twait(r, carry):
            _row_copy(zero_ref, xs_hbm.at[pl.ds(pl.multiple_of(r * zt, zt), zt)], zsem).wait()
            return carry

        lax.fori_loop(nu_ref[0], xs_hbm.shape[0] // zt, tbody, 0)
        lax.fori_loop(nu_ref[0], xs_hbm.shape[0] // zt, twait, 0)

    def issue(t, carry):
        src = x_hbm.at[pl.ds(i * tm + t, 1)]
        _row_copy(src, xs_hbm.at[pl.ds(pos_ref[0, t], 1)], sem).start()
        _row_copy(src, xs_hbm.at[pl.ds(pos_ref[1, t], 1)], sem).start()
        return carry

    def drain(t, carry):
        src = x_hbm.at[pl.ds(i * tm + t, 1)]
        _row_copy(src, xs_hbm.at[pl.ds(pos_ref[0, t], 1)], sem).wait()
        _row_copy(src, xs_hbm.at[pl.ds(pos_ref[1, t], 1)], sem).wait()
        return carry

    lax.fori_loop(0, tm, issue, 0)
    lax.fori_loop(0, tm, drain, 0)


def _dispatch(pad_ranges, n_used, pos, xn, rows):
    s = xn.shape[0]
    tm = min(ROW_TILE, s)
    grid_spec = pltpu.PrefetchScalarGridSpec(
        num_scalar_prefetch=2,
        grid=(s // tm,),
        in_specs=[pl.BlockSpec((2, tm), lambda i, pad, nu: (0, i), memory_space=pltpu.SMEM),
                  pl.BlockSpec(memory_space=pl.ANY)],
        out_specs=pl.BlockSpec(memory_space=pl.ANY),
        scratch_shapes=[pltpu.VMEM((MOE_TILE, D_MODEL), F32), pltpu.SemaphoreType.DMA(()),
                        pltpu.SemaphoreType.DMA(())],
    )
    return pl.pallas_call(
        _dispatch_kernel,
        grid_spec=grid_spec,
        out_shape=jax.ShapeDtypeStruct((rows, D_MODEL), F32),
        compiler_params=_cparams("arbitrary"),
        name="dispatch",
    )(pad_ranges, n_used, pos, xn)


def _combine_kernel(pos_ref, ys_hbm, h_ref, wcol_ref, fg_ref, o_ref, buf_ref, sem, *, final_norm):
    tm = h_ref.shape[0]

    def issue(t, carry):
        _row_copy(ys_hbm.at[pl.ds(pos_ref[0, t], 1)], buf_ref.at[0, pl.ds(t, 1)], sem).start()
        _row_copy(ys_hbm.at[pl.ds(pos_ref[1, t], 1)], buf_ref.at[1, pl.ds(t, 1)], sem).start()
        return carry

    def drain(t, carry):
        _row_copy(ys_hbm.at[pl.ds(pos_ref[0, t], 1)], buf_ref.at[0, pl.ds(t, 1)], sem).wait()
        _row_copy(ys_hbm.at[pl.ds(pos_ref[1, t], 1)], buf_ref.at[1, pl.ds(t, 1)], sem).wait()
        return carry

    lax.fori_loop(0, tm, issue, 0)
    lax.fori_loop(0, tm, drain, 0)
    out = h_ref[...] + wcol_ref[:, 0:1] * buf_ref[0] + wcol_ref[:, 1:2] * buf_ref[1]
    if final_norm:
        out = _rms(out, fg_ref[...])
    o_ref[...] = out


def _combine(pos, ys, h, wcol, final_g, final_norm):
    s = h.shape[0]
    tm = min(ROW_TILE, s)
    return pl.pallas_call(
        functools.partial(_combine_kernel, final_norm=final_norm),
        grid=(s // tm,),
        in_specs=[pl.BlockSpec((2, tm), lambda i: (0, i), memory_space=pltpu.SMEM),
                  pl.BlockSpec(memory_space=pl.ANY),
                  pl.BlockSpec((tm, D_MODEL), lambda i: (i, 0)),
                  pl.BlockSpec((tm, N_EXPERTS), lambda i: (i, 0)),
                  pl.BlockSpec((1, D_MODEL), lambda i: (0, 0))],
        out_specs=pl.BlockSpec((tm, D_MODEL), lambda i: (i, 0)),
        out_shape=jax.ShapeDtypeStruct((s, D_MODEL), F32),
        scratch_shapes=[pltpu.VMEM((2, tm, D_MODEL), F32), pltpu.SemaphoreType.DMA(())],
        compiler_params=_cparams("arbitrary"),
        name="combine",
    )(pos, ys, h, wcol, final_g)


def _final_norm_kernel(h_ref, g_ref, o_ref):
    o_ref[...] = _rms(h_ref[...], g_ref[...])


def _final_norm(h, g):
    s = h.shape[0]
    tm = min(ROW_TILE, s)
    return pl.pallas_call(
        _final_norm_kernel,
        grid=(s // tm,),
        in_specs=[pl.BlockSpec((tm, D_MODEL), lambda i: (i, 0)), pl.BlockSpec((1, D_MODEL), lambda i: (0, 0))],
        out_specs=pl.BlockSpec((tm, D_MODEL), lambda i: (i, 0)),
        out_shape=jax.ShapeDtypeStruct((s, D_MODEL), F32),
        compiler_params=_cparams("parallel"),
        name="final_norm",
    )(h, g)


def _rope_lane_tables(seq, rot_dim, first_lane, period):
    half = rot_dim // 2
    pos = jnp.arange(seq, dtype=F32)
    inv = ROPE_THETA ** (-jnp.arange(0, rot_dim, 2, dtype=F32) / rot_dim)
    ang = pos[:, None] * inv[None, :]
    cos, sin = jnp.cos(ang), jnp.sin(ang)
    lane = jnp.arange(LANES) % period - first_lane
    lo = (lane >= 0) & (lane < half)
    hi = (lane >= half) & (lane < rot_dim)
    idx = jnp.clip(jnp.where(hi, lane - half, lane), 0, half - 1)
    cosf = jnp.where((lo | hi)[None, :], cos[:, idx], 1.0)
    sins = jnp.where(lo[None, :], -sin[:, idx], jnp.where(hi[None, :], sin[:, idx], 0.0))
    return cosf, sins


def _even_layer(h, l, p, i, rope_a):
    w_in = p["w_in_even"][i].astype(BF16)
    q, k, v, ob = _even_in(
        h, p["norm_mix_even"][i][None, :], w_in, rope_a[0], rope_a[1],
        p["sgu_ln_g"][i][None, :], p["sgu_ln_b"][i][None, :], p["sgu_w"][i], p["sgu_b"][i].T)
    lam_init = 0.8 - 0.6 * math.exp(-0.3 * l)
    oa = _diff_attn(q, k, v, p["lam_q1"][i][None, :], p["lam_k1"][i][None, :],
                    p["lam_q2"][i][None, :], p["lam_k2"][i][None, :], p["subln_g"][i][None, :], lam_init)
    h = _out_proj(h, oa, ob, p["w_out_even"][i].astype(BF16))
    ffn = p["ffn_wg"].shape[2]
    return _dense_ffn(h, p["norm_ffn_even"][i][None, :], p["ffn_wg"][i].astype(BF16),
                      p["ffn_wu"][i].astype(BF16), p["ffn_wd"][i].astype(BF16), ffn // 2)


def _odd_layer(h, p, i, rope_d, final_g, final_norm):
    s = h.shape[0]
    w_in = p["w_in_odd"][i]
    n_main = C_WIDTH + D_Q_RANK + D_KV_RANK
    w_main = w_in[:, :n_main].astype(BF16)
    wkpe = jnp.zeros((D_MODEL, LANES), F32).at[:, D_NOPE:D_NOPE + D_ROPE].set(w_in[:, n_main:]).astype(BF16)
    wuq = jnp.pad(p["w_uq"][i].reshape(D_Q_RANK, D_HEADS, D_NOPE + D_ROPE),
                  ((0, 0), (0, 0), (0, D_HEAD_PAD - D_NOPE - D_ROPE))).reshape(D_Q_RANK, -1).astype(BF16)
    wukv = p["w_ukv"][i].reshape(D_KV_RANK, D_HEADS, D_NOPE + D_V)
    wk = jnp.pad(wukv[:, :, :D_NOPE], ((0, 0), (0, 0), (0, D_HEAD_PAD - D_NOPE))).reshape(D_KV_RANK, -1).astype(BF16)
    wv = wukv[:, :, D_NOPE:].reshape(D_KV_RANK, -1).astype(BF16)
    oc, q, k, v = _odd_in(
        h, p["norm_mix_odd"][i][None, :], w_main, wkpe, rope_d[0], rope_d[1],
        p["pool_w"][i].astype(BF16), p["pool_scale"][i][None, :],
        p["q_norm_g"][i][None, :], wuq, p["kv_norm_g"][i][None, :], wk, wv)
    od = _mla_attn(q, k, v)
    h = _out_proj(h, oc, od, p["w_out_odd"][i].astype(BF16))

    xn, eidx, rank, wcol, cnt = _router(h, p["norm_ffn_odd"][i][None, :], p["router"][i].T)
    counts = cnt[:, 0]
    padded = (counts + MOE_TILE - 1) // MOE_TILE * MOE_TILE
    ends = jnp.cumsum(padded)
    base = ends - padded
    pos = base[eidx] + rank
    n_tiles = (2 * s) // MOE_TILE + N_EXPERTS
    tile_start = jnp.arange(n_tiles, dtype=jnp.int32) * MOE_TILE
    tile_expert = jnp.minimum(jnp.sum(tile_start[:, None] >= ends[None, :], axis=1), N_EXPERTS - 1).astype(jnp.int32)
    n_used = (ends[-1:] // MOE_TILE).astype(jnp.int32)
    pad_ranges = jnp.stack([base + counts, ends]).astype(jnp.int32)
    xs = _dispatch(pad_ranges, n_used, pos, xn, n_tiles * MOE_TILE)
    ffn = p["moe_wg"].shape[3]
    ys = _expert_ffn(tile_expert, n_used, xs, p["moe_wg"][i].astype(BF16), p["moe_wu"][i].astype(BF16),
                     p["moe_wd"][i].astype(BF16), ffn // 2)
    return _combine(pos, ys, h, wcol, final_g, final_norm)


def kernel(x, norm_mix_even, w_in_even, lam_q1, lam_k1, lam_q2, lam_k2, subln_g, sgu_ln_g, sgu_ln_b, sgu_w, sgu_b, w_out_even, norm_ffn_even, ffn_wg, ffn_wu, ffn_wd, norm_mix_odd, w_in_odd, pool_w, pool_scale, q_norm_g, w_uq, kv_norm_g, w_ukv, w_out_odd, norm_ffn_odd, router, moe_wg, moe_wu, moe_wd, final_norm):
    p = dict(norm_mix_even=norm_mix_even, w_in_even=w_in_even, lam_q1=lam_q1, lam_k1=lam_k1, lam_q2=lam_q2,
             lam_k2=lam_k2, subln_g=subln_g, sgu_ln_g=sgu_ln_g, sgu_ln_b=sgu_ln_b, sgu_w=sgu_w, sgu_b=sgu_b,
             w_out_even=w_out_even, norm_ffn_even=norm_ffn_even, ffn_wg=ffn_wg, ffn_wu=ffn_wu, ffn_wd=ffn_wd,
             norm_mix_odd=norm_mix_odd, w_in_odd=w_in_odd, pool_w=pool_w, pool_scale=pool_scale,
             q_norm_g=q_norm_g, w_uq=w_uq, kv_norm_g=kv_norm_g, w_ukv=w_ukv, w_out_odd=w_out_odd,
             norm_ffn_odd=norm_ffn_odd, router=router, moe_wg=moe_wg, moe_wu=moe_wu, moe_wd=moe_wd)
    b, s, d = x.shape
    depth = norm_mix_even.shape[0] + norm_mix_odd.shape[0]
    rope_a = _rope_lane_tables(s, A_ROT, 0, A_QK_DIM)
    rope_d = _rope_lane_tables(s, D_ROPE, D_NOPE, LANES)
    final_g = final_norm[None, :]
    outs = []
    for bi in range(b):
        h = x[bi]
        for l in range(depth):
            if l % 2 == 0:
                h = _even_layer(h, l, p, l // 2, rope_a)
            else:
                h = _odd_layer(h, p, l // 2, rope_d, final_g, final_norm=(l == depth - 1))
        if depth % 2 == 1:
            h = _final_norm(h, final_g)
        outs.append(h)
    return jnp.stack(outs)
```

```python
import functools
import math

import jax
import jax.numpy as jnp
from jax import lax
from jax.experimental import pallas as pl
from jax.experimental.pallas import tpu as pltpu

F32 = jnp.float32
BF16 = jnp.bfloat16

ROPE_THETA = 500000.0
NORM_EPS = 1e-6
LN_EPS = 1e-5
LOG2E = 1.4426950408889634

D_MODEL = 1024
A_HEADS = 4
A_QK_DIM = 64
A_ROT = 16
A_WIDTH = 512
B_WIDTH = 512
B_GROUPS = 4
B_CHUNK = 128
C_WIDTH = 512
C_WINDOWS = (2, 4, 8, 16)
C_HALO = 16
D_HEADS = 8
D_NOPE = 64
D_ROPE = 32
D_V = 64
D_Q_RANK = 384
D_KV_RANK = 256
D_HEAD_PAD = 128
VT_ROWS = 144
N_EXPERTS = 8

LANES = 128
ROW_TILE = 512
ATTN_TILE = 512
MOE_TILE = 512
VMEM_LIMIT = 56 * 1024 * 1024


def _cparams(*sem):
    return pltpu.CompilerParams(dimension_semantics=sem, vmem_limit_bytes=VMEM_LIMIT)


def _rms(x, g):
    return x * lax.rsqrt(jnp.mean(x * x, axis=-1, keepdims=True) + NORM_EPS) * g


def _dot(a, b):
    return jnp.dot(a, b, preferred_element_type=F32)


def _rope_block(x, cosf, sins, shift, lo_mask):
    up = pltpu.roll(x, LANES - shift, 1)
    dn = pltpu.roll(x, shift, 1)
    return x * cosf + jnp.where(lo_mask, up, dn) * sins


def _rope_wide(x, cosf, sins, shift, lo_mask):
    nb = x.shape[1] // LANES
    return jnp.concatenate(
        [_rope_block(x[:, b * LANES:(b + 1) * LANES], cosf, sins, shift, lo_mask) for b in range(nb)],
        axis=1)


def _store_vt(vt_ref, v):
    rows = v.shape[0]
    vt = v.T
    r = lax.broadcasted_iota(jnp.int32, (VT_ROWS - LANES, rows), 0)
    tail = jnp.where(r == 0, 1.0, 0.0).astype(BF16)
    for b in range(v.shape[1] // LANES):
        vt_ref[b * VT_ROWS:b * VT_ROWS + LANES, :] = vt[b * LANES:(b + 1) * LANES, :].astype(BF16)
        vt_ref[b * VT_ROWS + LANES:(b + 1) * VT_ROWS, :] = tail


def _even_in_kernel(h_ref, g_ref, w_ref, cos_ref, sin_ref, lng_ref, lnb_ref, sw_ref, sbt_ref,
                    q_ref, k_ref, v_ref, ob_ref):
    tm = h_ref.shape[0]
    xn = _rms(h_ref[...], g_ref[...]).astype(BF16)
    cosf = cos_ref[...]
    sins = sin_ref[...]
    lane = lax.broadcasted_iota(jnp.int32, (tm, LANES), 1)
    lo_mask = (lane % A_QK_DIM) < (A_ROT // 2)

    q = _dot(xn, w_ref[:, 0:A_WIDTH])
    q = _rope_wide(q, cosf, sins, A_ROT // 2, lo_mask) * (A_QK_DIM ** -0.5 * LOG2E)
    q_ref[...] = q.astype(BF16)
    k = _dot(xn, w_ref[:, A_WIDTH:2 * A_WIDTH])
    k_ref[...] = _rope_wide(k, cosf, sins, A_ROT // 2, lo_mask).astype(BF16)
    _store_vt(v_ref, _dot(xn, w_ref[:, 2 * A_WIDTH:3 * A_WIDTH]))

    ub = jax.nn.gelu(_dot(xn, w_ref[:, 3 * A_WIDTH:3 * A_WIDTH + B_WIDTH]))
    vb = jax.nn.gelu(_dot(xn, w_ref[:, 3 * A_WIDTH + B_WIDTH:3 * A_WIDTH + 2 * B_WIDTH]))
    mu = jnp.mean(vb, axis=-1, keepdims=True)
    var = jnp.mean(jnp.square(vb - mu), axis=-1, keepdims=True)
    vn = ((vb - mu) * lax.rsqrt(var + LN_EPS) * lng_ref[...] + lnb_ref[...]).astype(BF16)

    row = lax.broadcasted_iota(jnp.int32, (B_CHUNK, B_CHUNK), 0)
    col = lax.broadcasted_iota(jnp.int32, (B_CHUNK, B_CHUNK), 1)
    causal = col <= row
    gc = B_WIDTH // B_GROUPS
    for g in range(B_GROUPS):
        wg = jnp.where(causal, sw_ref[g], 0.0).astype(BF16)
        bias = sbt_ref[:, g:g + 1]
        for c in range(tm // B_CHUNK):
            rs = slice(c * B_CHUNK, (c + 1) * B_CHUNK)
            cs = slice(g * gc, (g + 1) * gc)
            mixed = _dot(wg, vn[rs, cs]) + bias
            ob_ref[rs, cs] = (ub[rs, cs] * mixed).astype(BF16)


def _even_in(h, g, w, cosf, sins, lng, lnb, sgu_w, sgu_bt):
    s = h.shape[0]
    tm = min(ROW_TILE, s)
    n_in = w.shape[1]
    row_spec = lambda width: pl.BlockSpec((tm, width), lambda i: (i, 0))
    full = lambda shape: pl.BlockSpec(shape, lambda i: (0,) * len(shape))
    out = jax.ShapeDtypeStruct((s, A_WIDTH), BF16)
    return pl.pallas_call(
        _even_in_kernel,
        grid=(s // tm,),
        in_specs=[row_spec(D_MODEL), full((1, D_MODEL)), full((D_MODEL, n_in)),
                  row_spec(LANES), row_spec(LANES), full((1, B_WIDTH)), full((1, B_WIDTH)),
                  full((B_GROUPS, B_CHUNK, B_CHUNK)), full((B_CHUNK, B_GROUPS))],
        out_specs=[row_spec(A_WIDTH), row_spec(A_WIDTH),
                   pl.BlockSpec((A_HEADS * VT_ROWS, tm), lambda i: (0, i)), row_spec(A_WIDTH)],
        out_shape=[out, out, jax.ShapeDtypeStruct((A_HEADS * VT_ROWS, s), BF16), out],
        compiler_params=_cparams("parallel"),
        name="even_in",
    )(h, g, w, cosf, sins, lng, lnb, sgu_w, sgu_bt)


VT_ONES = LANES


def _flash_core(qi, t, score_fn, vt_ref, s_a, s_b, m_ref, acc_ref):
    m_ref[...] = jnp.full(m_ref.shape, -jnp.inf, F32)
    acc_ref[...] = jnp.zeros(acc_ref.shape, F32)

    def process(s_ref, j, diagonal):
        s = s_ref[...]
        if diagonal:
            key = lax.broadcasted_iota(jnp.int32, s.shape, 0)
            qry = lax.broadcasted_iota(jnp.int32, s.shape, 1) % t
            s = jnp.where(key <= qry, s, -jnp.inf)
        m_prev = m_ref[...]
        m_new = jnp.maximum(m_prev, jnp.max(s, axis=0, keepdims=True))
        alpha = jnp.exp2(m_prev - m_new)
        p = jnp.exp2(s - m_new).astype(BF16)
        off = pl.multiple_of(j * t, t)
        acc_ref[...] = acc_ref[...] * alpha + _dot(vt_ref[:, pl.ds(off, t)], p)
        m_ref[...] = m_new

    score_fn(0, s_a)

    def body(it, carry):
        j = 2 * it
        score_fn(j + 1, s_b)
        process(s_a, j, False)
        score_fn(j + 2, s_a)
        process(s_b, j + 1, False)
        return carry

    lax.fori_loop(0, qi // 2, body, 0)

    @pl.when(qi % 2 == 1)
    def _():
        score_fn(qi, s_b)
        process(s_a, qi - 1, False)
        process(s_b, qi, True)

    @pl.when(qi % 2 == 0)
    def _():
        process(s_a, qi, True)


def _attn_scratch(t):
    return [pltpu.VMEM((LANES, 2 * t), BF16), pltpu.VMEM((t, 2 * t), F32), pltpu.VMEM((t, 2 * t), F32),
            pltpu.VMEM((1, 2 * t), F32), pltpu.VMEM((VT_ROWS, 2 * t), F32)]


def _diff_attn_kernel(q_ref, k_ref, vt_ref, lq1_ref, lk1_ref, lq2_ref, lk2_ref, sg_ref, o_ref,
                      qt_ref, s_a, s_b, m_ref, acc_ref, *, lam_init):
    t = q_ref.shape[0]
    qi = pl.program_id(1)
    qt = q_ref[...].astype(F32).T
    row = lax.broadcasted_iota(jnp.int32, qt.shape, 0)
    qt_ref[:, 0:t] = jnp.where(row < A_QK_DIM, qt, 0.0).astype(BF16)
    qt_ref[:, t:2 * t] = jnp.where(row >= A_QK_DIM, qt, 0.0).astype(BF16)

    def score_fn(j, dst_ref):
        off = pl.multiple_of(j * t, t)
        dst_ref[...] = _dot(k_ref[pl.ds(off, t), :], qt_ref[...])

    _flash_core(qi, t, score_fn, vt_ref, s_a, s_b, m_ref, acc_ref)

    lam = (jnp.exp(jnp.sum(lq1_ref[...] * lk1_ref[...], axis=-1, keepdims=True))
           - jnp.exp(jnp.sum(lq2_ref[...] * lk2_ref[...], axis=-1, keepdims=True)) + lam_init)
    o = (acc_ref[0:LANES, 0:t] / acc_ref[VT_ONES:VT_ONES + 1, 0:t]
         - lam * (acc_ref[0:LANES, t:2 * t] / acc_ref[VT_ONES:VT_ONES + 1, t:2 * t]))
    o = o * lax.rsqrt(jnp.mean(o * o, axis=0, keepdims=True) + NORM_EPS) * sg_ref[...] * (1.0 - lam_init)
    o_ref[...] = o.T.astype(BF16)


def _diff_attn(q, k, vt, lq1, lk1, lq2, lk2, subln_g, lam_init):
    s = q.shape[0]
    t = min(ATTN_TILE, s)
    hw = 2 * A_QK_DIM
    small = lambda n: pl.BlockSpec((1, n), lambda h, i: (0, 0))
    return pl.pallas_call(
        functools.partial(_diff_attn_kernel, lam_init=lam_init),
        grid=(A_HEADS, s // t),
        in_specs=[pl.BlockSpec((t, hw), lambda h, i: (i, h)),
                  pl.BlockSpec((s, hw), lambda h, i: (0, h)),
                  pl.BlockSpec((VT_ROWS, s), lambda h, i: (h, 0)),
                  small(A_QK_DIM), small(A_QK_DIM), small(A_QK_DIM), small(A_QK_DIM),
                  pl.BlockSpec((hw, 1), lambda h, i: (0, 0))],
        out_specs=pl.BlockSpec((t, hw), lambda h, i: (i, h)),
        out_shape=jax.ShapeDtypeStruct((s, A_WIDTH), BF16),
        scratch_shapes=_attn_scratch(t),
        compiler_params=_cparams("parallel", "parallel"),
        name="diff_attn",
    )(q, k, vt, lq1, lk1, lq2, lk2, subln_g)


def _mla_attn_kernel(q_ref, k_ref, vt_ref, o_ref, qt_ref, s_a, s_b, m_ref, acc_ref):
    t = q_ref.shape[0]
    qi = pl.program_id(1)
    qt_ref[:, 0:t] = q_ref[:, 0:D_HEAD_PAD].astype(F32).T.astype(BF16)
    qt_ref[:, t:2 * t] = q_ref[:, D_HEAD_PAD:2 * D_HEAD_PAD].astype(F32).T.astype(BF16)

    def score_fn(j, dst_ref):
        off = pl.multiple_of(j * t, t)
        dst_ref[:, 0:t] = _dot(k_ref[pl.ds(off, t), 0:D_HEAD_PAD], qt_ref[:, 0:t])
        dst_ref[:, t:2 * t] = _dot(k_ref[pl.ds(off, t), D_HEAD_PAD:2 * D_HEAD_PAD], qt_ref[:, t:2 * t])

    _flash_core(qi, t, score_fn, vt_ref, s_a, s_b, m_ref, acc_ref)

    oa = acc_ref[0:D_V, 0:t] / acc_ref[VT_ONES:VT_ONES + 1, 0:t]
    ob = acc_ref[D_V:2 * D_V, t:2 * t] / acc_ref[VT_ONES:VT_ONES + 1, t:2 * t]
    o_ref[...] = jnp.concatenate([oa, ob], axis=0).T.astype(BF16)


def _mla_attn(q, k, vt):
    s = q.shape[0]
    t = min(ATTN_TILE, s)
    return pl.pallas_call(
        _mla_attn_kernel,
        grid=(D_HEADS // 2, s // t),
        in_specs=[pl.BlockSpec((t, 2 * D_HEAD_PAD), lambda h, i: (i, h)),
                  pl.BlockSpec((s, 2 * D_HEAD_PAD), lambda h, i: (0, h)),
                  pl.BlockSpec((VT_ROWS, s), lambda h, i: (h, 0))],
        out_specs=pl.BlockSpec((t, 2 * D_V), lambda h, i: (i, h)),
        out_shape=jax.ShapeDtypeStruct((s, D_HEADS * D_V), BF16),
        scratch_shapes=_attn_scratch(t),
        compiler_params=_cparams("parallel", "parallel"),
        name="mla_attn",
    )(q, k, vt)


def _out_proj_kernel(h_ref, a_ref, b_ref, w_ref, o_ref):
    half = a_ref.shape[1]
    o_ref[...] = h_ref[...] + _dot(a_ref[...], w_ref[0:half, :]) + _dot(b_ref[...], w_ref[half:2 * half, :])


def _out_proj(h, a, b, w):
    s = h.shape[0]
    tm = min(ROW_TILE, s)
    half = a.shape[1]
    return pl.pallas_call(
        _out_proj_kernel,
        grid=(s // tm,),
        in_specs=[pl.BlockSpec((tm, D_MODEL), lambda i: (i, 0)),
                  pl.BlockSpec((tm, half), lambda i: (i, 0)),
                  pl.BlockSpec((tm, half), lambda i: (i, 0)),
                  pl.BlockSpec((2 * half, D_MODEL), lambda i: (0, 0))],
        out_specs=pl.BlockSpec((tm, D_MODEL), lambda i: (i, 0)),
        out_shape=jax.ShapeDtypeStruct((s, D_MODEL), F32),
        compiler_params=_cparams("parallel"),
        name="out_proj",
    )(h, a, b, w)


def _dense_ffn_kernel(h_ref, g_ref, wg_ref, wu_ref, wd_ref, o_ref, xn_ref):
    f = pl.program_id(1)

    @pl.when(f == 0)
    def _():
        x = h_ref[...]
        xn_ref[...] = _rms(x, g_ref[...]).astype(BF16)
        o_ref[...] = x

    xn = xn_ref[...]
    a = jax.nn.silu(_dot(xn, wg_ref[...])) * _dot(xn, wu_ref[...])
    o_ref[...] += _dot(a.astype(BF16), wd_ref[...])


def _dense_ffn(h, g, wg, wu, wd, f_tile):
    s = h.shape[0]
    tm = min(ROW_TILE, s)
    ffn = wg.shape[1]
    return pl.pallas_call(
        _dense_ffn_kernel,
        grid=(s // tm, ffn // f_tile),
        in_specs=[pl.BlockSpec((tm, D_MODEL), lambda i, f: (i, 0)),
                  pl.BlockSpec((1, D_MODEL), lambda i, f: (0, 0)),
                  pl.BlockSpec((D_MODEL, f_tile), lambda i, f: (0, f)),
                  pl.BlockSpec((D_MODEL, f_tile), lambda i, f: (0, f)),
                  pl.BlockSpec((f_tile, D_MODEL), lambda i, f: (f, 0))],
        out_specs=pl.BlockSpec((tm, D_MODEL), lambda i, f: (i, 0)),
        out_shape=jax.ShapeDtypeStruct((s, D_MODEL), F32),
        scratch_shapes=[pltpu.VMEM((tm, D_MODEL), BF16)],
        compiler_params=_cparams("parallel", "arbitrary"),
        name="dense_ffn",
    )(h, g, wg, wu, wd)


def _row_copy(src, dst, sem):
    return pltpu.make_async_copy(src, dst, sem)


def _expert_ffn_kernel(te_ref, nu_ref, src_ref, x_hbm, wg_ref, wu_ref, wd_ref, o_ref, xbuf_ref, xb_ref, sem):
    r = pl.program_id(0)
    f = pl.program_id(1)
    tm = o_ref.shape[0]
    n_used = nu_ref[0]

    def gather(tile, slot, start):
        def body(t, carry):
            cp = _row_copy(x_hbm.at[pl.ds(src_ref[tile * tm + t], 1)], xbuf_ref.at[slot, pl.ds(t, 1)],
                           sem.at[slot])
            if start:
                cp.start()
            else:
                cp.wait()
            return carry

        lax.fori_loop(0, tm, body, 0)

    @pl.when(r < n_used)
    def _():
        @pl.when(f == 0)
        def _():
            slot = r % 2

            @pl.when(r == 0)
            def _():
                gather(0, 0, True)

            @pl.when(r + 1 < n_used)
            def _():
                gather(r + 1, 1 - slot, True)

            gather(r, slot, False)
            xb_ref[...] = xbuf_ref[slot].astype(BF16)
            o_ref[...] = jnp.zeros(o_ref.shape, F32)

        xb = xb_ref[...]
        a = jax.nn.silu(_dot(xb, wg_ref[0])) * _dot(xb, wu_ref[0])
        o_ref[...] += _dot(a.astype(BF16), wd_ref[0])

    @pl.when(jnp.logical_and(r >= n_used, f == 0))
    def _():
        o_ref[...] = jnp.zeros(o_ref.shape, F32)


def _expert_ffn(tile_expert, n_used, src, xn, wg, wu, wd, f_tile):
    rows = src.shape[0]
    tm = MOE_TILE
    ffn = wg.shape[2]
    grid_spec = pltpu.PrefetchScalarGridSpec(
        num_scalar_prefetch=3,
        grid=(rows // tm, ffn // f_tile),
        in_specs=[pl.BlockSpec(memory_space=pl.ANY),
                  pl.BlockSpec((1, D_MODEL, f_tile), lambda r, f, te, nu, sr: (te[r], 0, jnp.where(r < nu[0], f, 0))),
                  pl.BlockSpec((1, D_MODEL, f_tile), lambda r, f, te, nu, sr: (te[r], 0, jnp.where(r < nu[0], f, 0))),
                  pl.BlockSpec((1, f_tile, D_MODEL), lambda r, f, te, nu, sr: (te[r], jnp.where(r < nu[0], f, 0), 0))],
        out_specs=pl.BlockSpec((tm, D_MODEL), lambda r, f, te, nu, sr: (r, 0)),
        scratch_shapes=[pltpu.VMEM((2, tm, D_MODEL), F32), pltpu.VMEM((tm, D_MODEL), BF16),
                        pltpu.SemaphoreType.DMA((2,))],
    )
    return pl.pallas_call(
        _expert_ffn_kernel,
        grid_spec=grid_spec,
        out_shape=jax.ShapeDtypeStruct((rows, D_MODEL), F32),
        compiler_params=_cparams("arbitrary", "arbitrary"),
        name="expert_ffn",
    )(tile_expert, n_used, src, xn, wg, wu, wd)


def _odd_in_kernel(h_ref, g_ref, w_ref, wkpe_ref, cos_ref, sin_ref, pw_ref, ps_ref,
                   qg_ref, wuq_ref, kvg_ref, wk_ref, wv_ref,
                   oc_ref, q_ref, k_ref, v_ref, halo_ref):
    tm = h_ref.shape[0]
    i = pl.program_id(0)
    xn = _rms(h_ref[...], g_ref[...]).astype(BF16)
    cosf = cos_ref[...]
    sins = sin_ref[...]
    lane = lax.broadcasted_iota(jnp.int32, (tm, LANES), 1)
    lo_mask = lane < (D_NOPE + D_ROPE // 2)

    @pl.when(i == 0)
    def _():
        halo_ref[...] = jnp.zeros(halo_ref.shape, F32)

    xc = _dot(xn, w_ref[:, 0:C_WIDTH])
    ext = jnp.concatenate([halo_ref[...], xc], axis=0)
    halo_ref[...] = xc[tm - C_HALO:tm, :]
    pos = i * tm + lax.broadcasted_iota(jnp.int32, (tm, 1), 0)
    gc = C_WIDTH // len(C_WINDOWS)
    for g, win in enumerate(C_WINDOWS):
        cs = slice(g * gc, (g + 1) * gc)
        a = ext[:, cs]
        span = 1
        while span < win:
            n = a.shape[0] - span
            a = a[span:span + n, :] + a[0:n, :]
            span *= 2
        start = C_HALO - (win - 1)
        wsum = a[start:start + tm, :]
        cnt = jnp.minimum(pos + 1, win).astype(F32)
        pooled = (wsum / cnt - xc[:, cs]).astype(BF16)
        oc_ref[:, cs] = (_dot(pooled, pw_ref[g]) * ps_ref[:, cs]).astype(BF16)

    cq = _dot(xn, w_ref[:, C_WIDTH:C_WIDTH + D_Q_RANK])
    q = _dot(_rms(cq, qg_ref[...]).astype(BF16), wuq_ref[...])
    q = _rope_wide(q, cosf, sins, D_ROPE // 2, lo_mask) * ((D_NOPE + D_ROPE) ** -0.5 * LOG2E)
    q_ref[...] = q.astype(BF16)

    ckv = _dot(xn, w_ref[:, C_WIDTH + D_Q_RANK:C_WIDTH + D_Q_RANK + D_KV_RANK])
    ckvn = _rms(ckv, kvg_ref[...]).astype(BF16)
    kpe = _rope_block(_dot(xn, wkpe_ref[...]), cosf, sins, D_ROPE // 2, lo_mask)
    kn = _dot(ckvn, wk_ref[...])
    k_ref[...] = jnp.concatenate(
        [kn[:, b * LANES:(b + 1) * LANES] + kpe for b in range(D_HEADS)], axis=1).astype(BF16)
    _store_vt(v_ref, _dot(ckvn, wv_ref[...]))


def _odd_in(h, g, w, wkpe, cosf, sins, pool_w, pool_scale, qg, wuq, kvg, wk, wv):
    s = h.shape[0]
    tm = min(ROW_TILE, s)
    row_spec = lambda width: pl.BlockSpec((tm, width), lambda i: (i, 0))
    full = lambda shape: pl.BlockSpec(shape, lambda i: (0,) * len(shape))
    return pl.pallas_call(
        _odd_in_kernel,
        grid=(s // tm,),
        in_specs=[row_spec(D_MODEL), full((1, D_MODEL)), full(w.shape), full(wkpe.shape),
                  row_spec(LANES), row_spec(LANES), full(pool_w.shape), full((1, C_WIDTH)),
                  full((1, D_Q_RANK)), full(wuq.shape), full((1, D_KV_RANK)), full(wk.shape), full(wv.shape)],
        out_specs=[row_spec(C_WIDTH), row_spec(D_HEADS * D_HEAD_PAD), row_spec(D_HEADS * D_HEAD_PAD),
                   pl.BlockSpec((D_HEADS // 2 * VT_ROWS, tm), lambda i: (0, i))],
        out_shape=[jax.ShapeDtypeStruct((s, C_WIDTH), BF16),
                   jax.ShapeDtypeStruct((s, D_HEADS * D_HEAD_PAD), BF16),
                   jax.ShapeDtypeStruct((s, D_HEADS * D_HEAD_PAD), BF16),
                   jax.ShapeDtypeStruct((D_HEADS // 2 * VT_ROWS, s), BF16)],
        scratch_shapes=[pltpu.VMEM((C_HALO, C_WIDTH), F32)],
        compiler_params=_cparams("arbitrary"),
        name="odd_in",
    )(h, g, w, wkpe, cosf, sins, pool_w, pool_scale, qg, wuq, kvg, wk, wv)


def _router_kernel(h_ref, g_ref, rt_ref, xn_ref, eidx_ref, rank_ref, wcol_ref, cnt_ref, run_ref):
    tm = h_ref.shape[0]
    i = pl.program_id(0)

    @pl.when(i == 0)
    def _():
        run_ref[...] = jnp.zeros(run_ref.shape, F32)

    xn = _rms(h_ref[...], g_ref[...])
    xn_ref[...] = xn
    lg = lax.dot_general(rt_ref[...], xn, (((1,), (1,)), ((), ())),
                         preferred_element_type=F32, precision=lax.Precision.HIGHEST)
    eio = lax.broadcasted_iota(jnp.int32, lg.shape, 0)
    m1 = jnp.max(lg, axis=0, keepdims=True)
    i1 = jnp.min(jnp.where(lg == m1, eio, N_EXPERTS), axis=0, keepdims=True)
    lg2 = jnp.where(eio == i1, -jnp.inf, lg)
    m2 = jnp.max(lg2, axis=0, keepdims=True)
    i2 = jnp.min(jnp.where(lg2 == m2, eio, N_EXPERTS), axis=0, keepdims=True)
    e = jnp.exp(m2 - m1)
    w1 = 1.0 / (1.0 + e)
    w2 = e / (1.0 + e)

    sel1 = eio == i1
    sel2 = eio == i2
    onehot = jnp.logical_or(sel1, sel2)
    tr = lax.broadcasted_iota(jnp.int32, (tm, tm), 0)
    tc = lax.broadcasted_iota(jnp.int32, (tm, tm), 1)
    upper = (tr < tc).astype(BF16)
    before = _dot(onehot.astype(BF16), upper) + run_ref[:, 0:1]
    r1 = jnp.sum(jnp.where(sel1, before, 0.0), axis=0, keepdims=True)
    r2 = jnp.sum(jnp.where(sel2, before, 0.0), axis=0, keepdims=True)
    run_new = run_ref[:, 0:1] + jnp.sum(onehot.astype(F32), axis=1, keepdims=True)
    run_ref[...] = jnp.broadcast_to(run_new, run_ref.shape)
    cnt_ref[...] = jnp.broadcast_to(run_new, cnt_ref.shape).astype(jnp.int32)

    eidx_ref[...] = jnp.concatenate([i1, i2], axis=0)
    rank_ref[...] = jnp.concatenate([r1, r2], axis=0).astype(jnp.int32)
    w8 = jnp.concatenate([w1, w2, jnp.zeros((N_EXPERTS - 2, tm), F32)], axis=0)
    eye = (tr == tc).astype(F32)
    wcol_ref[...] = lax.dot_general(eye, w8, (((1,), (1,)), ((), ())),
                                    preferred_element_type=F32, precision=lax.Precision.HIGHEST)


def _router(h, g, router_t):
    s = h.shape[0]
    tm = min(ROW_TILE, s)
    return pl.pallas_call(
        _router_kernel,
        grid=(s // tm,),
        in_specs=[pl.BlockSpec((tm, D_MODEL), lambda i: (i, 0)),
                  pl.BlockSpec((1, D_MODEL), lambda i: (0, 0)),
                  pl.BlockSpec((N_EXPERTS, D_MODEL), lambda i: (0, 0))],
        out_specs=[pl.BlockSpec((tm, D_MODEL), lambda i: (i, 0)),
                   pl.BlockSpec((2, tm), lambda i: (0, i)),
                   pl.BlockSpec((2, tm), lambda i: (0, i)),
                   pl.BlockSpec((tm, N_EXPERTS), lambda i: (i, 0)),
                   pl.BlockSpec((N_EXPERTS, LANES), lambda i: (0, 0))],
        out_shape=[jax.ShapeDtypeStruct((s, D_MODEL), F32),
                   jax.ShapeDtypeStruct((2, s), jnp.int32),
                   jax.ShapeDtypeStruct((2, s), jnp.int32),
                   jax.ShapeDtypeStruct((s, N_EXPERTS), F32),
                   jax.ShapeDtypeStruct((N_EXPERTS, LANES), jnp.int32)],
        scratch_shapes=[pltpu.VMEM((N_EXPERTS, LANES), F32)],
        compiler_params=_cparams("arbitrary"),
        name="router",
    )(h, g, router_t)


def _invert_kernel(pos_ref, src_ref):
    n_tok = pos_ref.shape[0] // 2

    def clear(i, carry):
        src_ref[i] = 0
        return carry

    lax.fori_loop(0, src_ref.shape[0], clear, 0, unroll=8)

    def put(t, carry):
        src_ref[pos_ref[t]] = t
        src_ref[pos_ref[n_tok + t]] = t
        return carry

    lax.fori_loop(0, n_tok, put, 0, unroll=8)


def _invert(pos_flat, n_slots):
    return pl.pallas_call(
        _invert_kernel,
        in_specs=[pl.BlockSpec(memory_space=pltpu.SMEM)],
        out_specs=pl.BlockSpec(memory_space=pltpu.SMEM),
        out_shape=jax.ShapeDtypeStruct((n_slots,), jnp.int32),
        name="invert",
    )(pos_flat)


def _combine_kernel(pos_ref, ys_hbm, h_ref, wcol_ref, fg_ref, o_ref, buf_ref, sem, *, final_norm):
    tm = h_ref.shape[0]

    def issue(t, carry):
        _row_copy(ys_hbm.at[pl.ds(pos_ref[0, t], 1)], buf_ref.at[0, pl.ds(t, 1)], sem).start()
        _row_copy(ys_hbm.at[pl.ds(pos_ref[1, t], 1)], buf_ref.at[1, pl.ds(t, 1)], sem).start()
        return carry

    def drain(t, carry):
        _row_copy(ys_hbm.at[pl.ds(pos_ref[0, t], 1)], buf_ref.at[0, pl.ds(t, 1)], sem).wait()
        _row_copy(ys_hbm.at[pl.ds(pos_ref[1, t], 1)], buf_ref.at[1, pl.ds(t, 1)], sem).wait()
        return carry

    lax.fori_loop(0, tm, issue, 0)
    lax.fori_loop(0, tm, drain, 0)
    out = h_ref[...] + wcol_ref[:, 0:1] * buf_ref[0] + wcol_ref[:, 1:2] * buf_ref[1]
    if final_norm:
        out = _rms(out, fg_ref[...])
    o_ref[...] = out


def _combine(pos, ys, h, wcol, final_g, final_norm):
    s = h.shape[0]
    tm = min(ROW_TILE, s)
    return pl.pallas_call(
        functools.partial(_combine_kernel, final_norm=final_norm),
        grid=(s // tm,),
        in_specs=[pl.BlockSpec((2, tm), lambda i: (0, i), memory_space=pltpu.SMEM),
                  pl.BlockSpec(memory_space=pl.ANY),
                  pl.BlockSpec((tm, D_MODEL), lambda i: (i, 0)),
                  pl.BlockSpec((tm, N_EXPERTS), lambda i: (i, 0)),
                  pl.BlockSpec((1, D_MODEL), lambda i: (0, 0))],
        out_specs=pl.BlockSpec((tm, D_MODEL), lambda i: (i, 0)),
        out_shape=jax.ShapeDtypeStruct((s, D_MODEL), F32),
        scratch_shapes=[pltpu.VMEM((2, tm, D_MODEL), F32), pltpu.SemaphoreType.DMA(())],
        compiler_params=_cparams("arbitrary"),
        name="combine",
    )(pos, ys, h, wcol, final_g)


def _final_norm_kernel(h_ref, g_ref, o_ref):
    o_ref[...] = _rms(h_ref[...], g_ref[...])


def _final_norm(h, g):
    s = h.shape[0]
    tm = min(ROW_TILE, s)
    return pl.pallas_call(
        _final_norm_kernel,
        grid=(s // tm,),
        in_specs=[pl.BlockSpec((tm, D_MODEL), lambda i: (i, 0)), pl.BlockSpec((1, D_MODEL), lambda i: (0, 0))],
        out_specs=pl.BlockSpec((tm, D_MODEL), lambda i: (i, 0)),
        out_shape=jax.ShapeDtypeStruct((s, D_MODEL), F32),
        compiler_params=_cparams("parallel"),
        name="final_norm",
    )(h, g)


def _rope_lane_tables(seq, rot_dim, first_lane, period):
    half = rot_dim // 2
    pos = jnp.arange(seq, dtype=F32)
    inv = ROPE_THETA ** (-jnp.arange(0, rot_dim, 2, dtype=F32) / rot_dim)
    ang = pos[:, None] * inv[None, :]
    cos, sin = jnp.cos(ang), jnp.sin(ang)
    lane = jnp.arange(LANES) % period - first_lane
    lo = (lane >= 0) & (lane < half)
    hi = (lane >= half) & (lane < rot_dim)
    idx = jnp.clip(jnp.where(hi, lane - half, lane), 0, half - 1)
    cosf = jnp.where((lo | hi)[None, :], cos[:, idx], 1.0)
    sins = jnp.where(lo[None, :], -sin[:, idx], jnp.where(hi[None, :], sin[:, idx], 0.0))
    return cosf, sins


def _even_layer(h, l, p, i, rope_a):
    w_in = p["w_in_even"][i].astype(BF16)
    q, k, v, ob = _even_in(
        h, p["norm_mix_even"][i][None, :], w_in, rope_a[0], rope_a[1],
        p["sgu_ln_g"][i][None, :], p["sgu_ln_b"][i][None, :], p["sgu_w"][i], p["sgu_b"][i].T)
    lam_init = 0.8 - 0.6 * math.exp(-0.3 * l)
    oa = _diff_attn(q, k, v, p["lam_q1"][i][None, :], p["lam_k1"][i][None, :],
                    p["lam_q2"][i][None, :], p["lam_k2"][i][None, :], p["subln_g"][i][:, None], lam_init)
    h = _out_proj(h, oa, ob, p["w_out_even"][i].astype(BF16))
    ffn = p["ffn_wg"].shape[2]
    return _dense_ffn(h, p["norm_ffn_even"][i][None, :], p["ffn_wg"][i].astype(BF16),
                      p["ffn_wu"][i].astype(BF16), p["ffn_wd"][i].astype(BF16), ffn // 2)


def _odd_layer(h, p, i, rope_d, final_g, final_norm):
    s = h.shape[0]
    w_in = p["w_in_odd"][i]
    n_main = C_WIDTH + D_Q_RANK + D_KV_RANK
    w_main = w_in[:, :n_main].astype(BF16)
    wkpe = jnp.zeros((D_MODEL, LANES), F32).at[:, D_NOPE:D_NOPE + D_ROPE].set(w_in[:, n_main:]).astype(BF16)
    wuq = jnp.pad(p["w_uq"][i].reshape(D_Q_RANK, D_HEADS, D_NOPE + D_ROPE),
                  ((0, 0), (0, 0), (0, D_HEAD_PAD - D_NOPE - D_ROPE))).reshape(D_Q_RANK, -1).astype(BF16)
    wukv = p["w_ukv"][i].reshape(D_KV_RANK, D_HEADS, D_NOPE + D_V)
    wk = jnp.pad(wukv[:, :, :D_NOPE], ((0, 0), (0, 0), (0, D_HEAD_PAD - D_NOPE))).reshape(D_KV_RANK, -1).astype(BF16)
    wv = wukv[:, :, D_NOPE:].reshape(D_KV_RANK, -1).astype(BF16)
    oc, q, k, v = _odd_in(
        h, p["norm_mix_odd"][i][None, :], w_main, wkpe, rope_d[0], rope_d[1],
        p["pool_w"][i].astype(BF16), p["pool_scale"][i][None, :],
        p["q_norm_g"][i][None, :], wuq, p["kv_norm_g"][i][None, :], wk, wv)
    od = _mla_attn(q, k, v)
    h = _out_proj(h, oc, od, p["w_out_odd"][i].astype(BF16))

    xn, eidx, rank, wcol, cnt = _router(h, p["norm_ffn_odd"][i][None, :], p["router"][i].T)
    counts = cnt[:, 0]
    padded = (counts + MOE_TILE - 1) // MOE_TILE * MOE_TILE
    ends = jnp.cumsum(padded)
    base = ends - padded
    pos = rank
    for e in range(N_EXPERTS):
        pos = pos + jnp.where(eidx == e, base[e], 0)
    n_tiles = (2 * s) // MOE_TILE + N_EXPERTS
    tile_start = jnp.arange(n_tiles, dtype=jnp.int32) * MOE_TILE
    tile_expert = jnp.minimum(jnp.sum(tile_start[:, None] >= ends[None, :], axis=1), N_EXPERTS - 1).astype(jnp.int32)
    n_used = (ends[-1:] // MOE_TILE).astype(jnp.int32)
    src = _invert(pos.reshape(-1), n_tiles * MOE_TILE)
    ffn = p["moe_wg"].shape[3]
    ys = _expert_ffn(tile_expert, n_used, src, xn, p["moe_wg"][i].astype(BF16), p["moe_wu"][i].astype(BF16),
                     p["moe_wd"][i].astype(BF16), ffn // 2)
    return _combine(pos, ys, h, wcol, final_g, final_norm)


def kernel(x, norm_mix_even, w_in_even, lam_q1, lam_k1, lam_q2, lam_k2, subln_g, sgu_ln_g, sgu_ln_b, sgu_w, sgu_b, w_out_even, norm_ffn_even, ffn_wg, ffn_wu, ffn_wd, norm_mix_odd, w_in_odd, pool_w, pool_scale, q_norm_g, w_uq, kv_norm_g, w_ukv, w_out_odd, norm_ffn_odd, router, moe_wg, moe_wu, moe_wd, final_norm):
    p = dict(norm_mix_even=norm_mix_even, w_in_even=w_in_even, lam_q1=lam_q1, lam_k1=lam_k1, lam_q2=lam_q2,
             lam_k2=lam_k2, subln_g=subln_g, sgu_ln_g=sgu_ln_g, sgu_ln_b=sgu_ln_b, sgu_w=sgu_w, sgu_b=sgu_b,
             w_out_even=w_out_even, norm_ffn_even=norm_ffn_even, ffn_wg=ffn_wg, ffn_wu=ffn_wu, ffn_wd=ffn_wd,
             norm_mix_odd=norm_mix_odd, w_in_odd=w_in_odd, pool_w=pool_w, pool_scale=pool_scale,
             q_norm_g=q_norm_g, w_uq=w_uq, kv_norm_g=kv_norm_g, w_ukv=w_ukv, w_out_odd=w_out_odd,
             norm_ffn_odd=norm_ffn_odd, router=router, moe_wg=moe_wg, moe_wu=moe_wu, moe_wd=moe_wd)
    b, s, d = x.shape
    depth = norm_mix_even.shape[0] + norm_mix_odd.shape[0]
    rope_a = _rope_lane_tables(s, A_ROT, 0, A_QK_DIM)
    rope_d = _rope_lane_tables(s, D_ROPE, D_NOPE, LANES)
    final_g = final_norm[None, :]
    outs = []
    for bi in range(b):
        h = x[bi]
        for l in range(depth):
            if l % 2 == 0:
                h = _even_layer(h, l, p, l // 2, rope_a)
            else:
                h = _odd_layer(h, p, l // 2, rope_d, final_g, final_norm=(l == depth - 1))
        if depth % 2 == 1:
            h = _final_norm(h, final_g)
        outs.append(h)
    return jnp.stack(outs)
```

```python
import functools
import math

import jax
import jax.numpy as jnp
from jax import lax
from jax.experimental import pallas as pl
from jax.experimental.pallas import tpu as pltpu

F32 = jnp.float32
BF16 = jnp.bfloat16

ROPE_THETA = 500000.0
NORM_EPS = 1e-6
LN_EPS = 1e-5
LOG2E = 1.4426950408889634

D_MODEL = 1024
A_HEADS = 4
A_QK_DIM = 64
A_ROT = 16
A_WIDTH = 512
B_WIDTH = 512
B_GROUPS = 4
B_CHUNK = 128
C_WIDTH = 512
C_WINDOWS = (2, 4, 8, 16)
C_HALO = 16
D_HEADS = 8
D_NOPE = 64
D_ROPE = 32
D_V = 64
D_Q_RANK = 384
D_KV_RANK = 256
D_HEAD_PAD = 128
VT_ROWS = 144
N_EXPERTS = 8

LANES = 128
ROW_TILE = 512
ATTN_TILE = 512
MOE_TILE = 512
VMEM_LIMIT = 56 * 1024 * 1024


def _cparams(*sem):
    return pltpu.CompilerParams(dimension_semantics=sem, vmem_limit_bytes=VMEM_LIMIT)


def _rms(x, g):
    return x * lax.rsqrt(jnp.mean(x * x, axis=-1, keepdims=True) + NORM_EPS) * g


def _dot(a, b):
    return jnp.dot(a, b, preferred_element_type=F32)


def _rope_block(x, cosf, sins, shift, lo_mask):
    up = pltpu.roll(x, LANES - shift, 1)
    dn = pltpu.roll(x, shift, 1)
    return x * cosf + jnp.where(lo_mask, up, dn) * sins


def _rope_wide(x, cosf, sins, shift, lo_mask):
    nb = x.shape[1] // LANES
    return jnp.concatenate(
        [_rope_block(x[:, b * LANES:(b + 1) * LANES], cosf, sins, shift, lo_mask) for b in range(nb)],
        axis=1)


def _store_vt(vt_ref, v):
    rows = v.shape[0]
    vt = v.T
    r = lax.broadcasted_iota(jnp.int32, (VT_ROWS - LANES, rows), 0)
    tail = jnp.where(r == 0, 1.0, 0.0).astype(BF16)
    for b in range(v.shape[1] // LANES):
        vt_ref[b * VT_ROWS:b * VT_ROWS + LANES, :] = vt[b * LANES:(b + 1) * LANES, :].astype(BF16)
        vt_ref[b * VT_ROWS + LANES:(b + 1) * VT_ROWS, :] = tail


def _even_in_kernel(h_ref, g_ref, w_ref, cos_ref, sin_ref, lng_ref, lnb_ref, sw_ref, sbt_ref,
                    q_ref, k_ref, v_ref, ob_ref):
    tm = h_ref.shape[0]
    xn = _rms(h_ref[...], g_ref[...]).astype(BF16)
    cosf = cos_ref[...]
    sins = sin_ref[...]
    lane = lax.broadcasted_iota(jnp.int32, (tm, LANES), 1)
    lo_mask = (lane % A_QK_DIM) < (A_ROT // 2)

    q = _dot(xn, w_ref[:, 0:A_WIDTH])
    q = _rope_wide(q, cosf, sins, A_ROT // 2, lo_mask) * (A_QK_DIM ** -0.5 * LOG2E)
    q_ref[...] = q.astype(BF16)
    k = _dot(xn, w_ref[:, A_WIDTH:2 * A_WIDTH])
    k_ref[...] = _rope_wide(k, cosf, sins, A_ROT // 2, lo_mask).astype(BF16)
    _store_vt(v_ref, _dot(xn, w_ref[:, 2 * A_WIDTH:3 * A_WIDTH]))

    ub = jax.nn.gelu(_dot(xn, w_ref[:, 3 * A_WIDTH:3 * A_WIDTH + B_WIDTH]))
    vb = jax.nn.gelu(_dot(xn, w_ref[:, 3 * A_WIDTH + B_WIDTH:3 * A_WIDTH + 2 * B_WIDTH]))
    mu = jnp.mean(vb, axis=-1, keepdims=True)
    var = jnp.mean(jnp.square(vb - mu), axis=-1, keepdims=True)
    vn = ((vb - mu) * lax.rsqrt(var + LN_EPS) * lng_ref[...] + lnb_ref[...]).astype(BF16)

    row = lax.broadcasted_iota(jnp.int32, (B_CHUNK, B_CHUNK), 0)
    col = lax.broadcasted_iota(jnp.int32, (B_CHUNK, B_CHUNK), 1)
    causal = col <= row
    gc = B_WIDTH // B_GROUPS
    for g in range(B_GROUPS):
        wg = jnp.where(causal, sw_ref[g], 0.0).astype(BF16)
        bias = sbt_ref[:, g:g + 1]
        for c in range(tm // B_CHUNK):
            rs = slice(c * B_CHUNK, (c + 1) * B_CHUNK)
            cs = slice(g * gc, (g + 1) * gc)
            mixed = _dot(wg, vn[rs, cs]) + bias
            ob_ref[rs, cs] = (ub[rs, cs] * mixed).astype(BF16)


def _even_in(h, g, w, cosf, sins, lng, lnb, sgu_w, sgu_bt):
    s = h.shape[0]
    tm = min(ROW_TILE, s)
    n_in = w.shape[1]
    row_spec = lambda width: pl.BlockSpec((tm, width), lambda i: (i, 0))
    full = lambda shape: pl.BlockSpec(shape, lambda i: (0,) * len(shape))
    out = jax.ShapeDtypeStruct((s, A_WIDTH), BF16)
    return pl.pallas_call(
        _even_in_kernel,
        grid=(s // tm,),
        in_specs=[row_spec(D_MODEL), full((1, D_MODEL)), full((D_MODEL, n_in)),
                  row_spec(LANES), row_spec(LANES), full((1, B_WIDTH)), full((1, B_WIDTH)),
                  full((B_GROUPS, B_CHUNK, B_CHUNK)), full((B_CHUNK, B_GROUPS))],
        out_specs=[row_spec(A_WIDTH), row_spec(A_WIDTH),
                   pl.BlockSpec((A_HEADS * VT_ROWS, tm), lambda i: (0, i)), row_spec(A_WIDTH)],
        out_shape=[out, out, jax.ShapeDtypeStruct((A_HEADS * VT_ROWS, s), BF16), out],
        compiler_params=_cparams("parallel"),
        name="even_in",
    )(h, g, w, cosf, sins, lng, lnb, sgu_w, sgu_bt)


VT_ONES = LANES


def _flash_core(qi, t, score_fn, vt_ref, s_a, s_b, m_ref, acc_ref):
    m_ref[...] = jnp.full(m_ref.shape, -jnp.inf, F32)
    acc_ref[...] = jnp.zeros(acc_ref.shape, F32)

    def process(s_ref, j, diagonal):
        s = s_ref[...]
        if diagonal:
            key = lax.broadcasted_iota(jnp.int32, s.shape, 0)
            qry = lax.broadcasted_iota(jnp.int32, s.shape, 1) % t
            s = jnp.where(key <= qry, s, -jnp.inf)
        m_prev = m_ref[...]
        m_new = jnp.maximum(m_prev, jnp.max(s, axis=0, keepdims=True))
        alpha = jnp.exp2(m_prev - m_new)
        p = jnp.exp2(s - m_new).astype(BF16)
        off = pl.multiple_of(j * t, t)
        acc_ref[...] = acc_ref[...] * alpha + _dot(vt_ref[:, pl.ds(off, t)], p)
        m_ref[...] = m_new

    score_fn(0, s_a)

    def pair(j):
        score_fn(j + 1, s_b)
        process(s_a, j, False)
        score_fn(j + 2, s_a)
        process(s_b, j + 1, False)

    def body4(it, carry):
        pair(4 * it)
        pair(4 * it + 2)
        return carry

    def body2(it, carry):
        pair(2 * it)
        return carry

    lax.fori_loop(0, qi // 4, body4, 0)
    lax.fori_loop(qi // 4 * 2, qi // 2, body2, 0)

    @pl.when(qi % 2 == 1)
    def _():
        score_fn(qi, s_b)
        process(s_a, qi - 1, False)
        process(s_b, qi, True)

    @pl.when(qi % 2 == 0)
    def _():
        process(s_a, qi, True)


def _attn_scratch(t):
    return [pltpu.VMEM((LANES, 2 * t), BF16), pltpu.VMEM((t, 2 * t), F32), pltpu.VMEM((t, 2 * t), F32),
            pltpu.VMEM((1, 2 * t), F32), pltpu.VMEM((VT_ROWS, 2 * t), F32)]


def _diff_attn_kernel(q_ref, k_ref, vt_ref, lq1_ref, lk1_ref, lq2_ref, lk2_ref, sg_ref, o_ref,
                      qt_ref, s_a, s_b, m_ref, acc_ref, *, lam_init):
    t = q_ref.shape[0]
    qi = pl.program_id(1)
    qt = q_ref[...].astype(F32).T
    row = lax.broadcasted_iota(jnp.int32, qt.shape, 0)
    qt_ref[:, 0:t] = jnp.where(row < A_QK_DIM, qt, 0.0).astype(BF16)
    qt_ref[:, t:2 * t] = jnp.where(row >= A_QK_DIM, qt, 0.0).astype(BF16)

    def score_fn(j, dst_ref):
        off = pl.multiple_of(j * t, t)
        dst_ref[...] = _dot(k_ref[pl.ds(off, t), :], qt_ref[...])

    _flash_core(qi, t, score_fn, vt_ref, s_a, s_b, m_ref, acc_ref)

    lam = (jnp.exp(jnp.sum(lq1_ref[...] * lk1_ref[...], axis=-1, keepdims=True))
           - jnp.exp(jnp.sum(lq2_ref[...] * lk2_ref[...], axis=-1, keepdims=True)) + lam_init)
    o = (acc_ref[0:LANES, 0:t] / acc_ref[VT_ONES:VT_ONES + 1, 0:t]
         - lam * (acc_ref[0:LANES, t:2 * t] / acc_ref[VT_ONES:VT_ONES + 1, t:2 * t]))
    o = o * lax.rsqrt(jnp.mean(o * o, axis=0, keepdims=True) + NORM_EPS) * sg_ref[...] * (1.0 - lam_init)
    o_ref[...] = o.T.astype(BF16)


def _diff_attn(q, k, vt, lq1, lk1, lq2, lk2, subln_g, lam_init):
    s = q.shape[0]
    t = min(ATTN_TILE, s)
    hw = 2 * A_QK_DIM
    small = lambda n: pl.BlockSpec((1, n), lambda h, i: (0, 0))
    return pl.pallas_call(
        functools.partial(_diff_attn_kernel, lam_init=lam_init),
        grid=(A_HEADS, s // t),
        in_specs=[pl.BlockSpec((t, hw), lambda h, i: (i, h)),
                  pl.BlockSpec((s, hw), lambda h, i: (0, h)),
                  pl.BlockSpec((VT_ROWS, s), lambda h, i: (h, 0)),
                  small(A_QK_DIM), small(A_QK_DIM), small(A_QK_DIM), small(A_QK_DIM),
                  pl.BlockSpec((hw, 1), lambda h, i: (0, 0))],
        out_specs=pl.BlockSpec((t, hw), lambda h, i: (i, h)),
        out_shape=jax.ShapeDtypeStruct((s, A_WIDTH), BF16),
        scratch_shapes=_attn_scratch(t),
        compiler_params=_cparams("parallel", "parallel"),
        name="diff_attn",
    )(q, k, vt, lq1, lk1, lq2, lk2, subln_g)


def _mla_attn_kernel(q_ref, k_ref, vt_ref, o_ref, qt_ref, s_a, s_b, m_ref, acc_ref):
    t = q_ref.shape[0]
    qi = pl.program_id(1)
    qt_ref[:, 0:t] = q_ref[:, 0:D_HEAD_PAD].astype(F32).T.astype(BF16)
    qt_ref[:, t:2 * t] = q_ref[:, D_HEAD_PAD:2 * D_HEAD_PAD].astype(F32).T.astype(BF16)

    def score_fn(j, dst_ref):
        off = pl.multiple_of(j * t, t)
        dst_ref[:, 0:t] = _dot(k_ref[pl.ds(off, t), 0:D_HEAD_PAD], qt_ref[:, 0:t])
        dst_ref[:, t:2 * t] = _dot(k_ref[pl.ds(off, t), D_HEAD_PAD:2 * D_HEAD_PAD], qt_ref[:, t:2 * t])

    _flash_core(qi, t, score_fn, vt_ref, s_a, s_b, m_ref, acc_ref)

    oa = acc_ref[0:D_V, 0:t] / acc_ref[VT_ONES:VT_ONES + 1, 0:t]
    ob = acc_ref[D_V:2 * D_V, t:2 * t] / acc_ref[VT_ONES:VT_ONES + 1, t:2 * t]
    o_ref[...] = jnp.concatenate([oa, ob], axis=0).T.astype(BF16)


def _mla_attn(q, k, vt):
    s = q.shape[0]
    t = min(ATTN_TILE, s)
    return pl.pallas_call(
        _mla_attn_kernel,
        grid=(D_HEADS // 2, s // t),
        in_specs=[pl.BlockSpec((t, 2 * D_HEAD_PAD), lambda h, i: (i, h)),
                  pl.BlockSpec((s, 2 * D_HEAD_PAD), lambda h, i: (0, h)),
                  pl.BlockSpec((VT_ROWS, s), lambda h, i: (h, 0))],
        out_specs=pl.BlockSpec((t, 2 * D_V), lambda h, i: (i, h)),
        out_shape=jax.ShapeDtypeStruct((s, D_HEADS * D_V), BF16),
        scratch_shapes=_attn_scratch(t),
        compiler_params=_cparams("parallel", "parallel"),
        name="mla_attn",
    )(q, k, vt)


def _out_proj_kernel(h_ref, a_ref, b_ref, w_ref, o_ref):
    half = a_ref.shape[1]
    o_ref[...] = h_ref[...] + _dot(a_ref[...], w_ref[0:half, :]) + _dot(b_ref[...], w_ref[half:2 * half, :])


def _out_proj(h, a, b, w):
    s = h.shape[0]
    tm = min(ROW_TILE, s)
    half = a.shape[1]
    return pl.pallas_call(
        _out_proj_kernel,
        grid=(s // tm,),
        in_specs=[pl.BlockSpec((tm, D_MODEL), lambda i: (i, 0)),
                  pl.BlockSpec((tm, half), lambda i: (i, 0)),
                  pl.BlockSpec((tm, half), lambda i: (i, 0)),
                  pl.BlockSpec((2 * half, D_MODEL), lambda i: (0, 0))],
        out_specs=pl.BlockSpec((tm, D_MODEL), lambda i: (i, 0)),
        out_shape=jax.ShapeDtypeStruct((s, D_MODEL), F32),
        compiler_params=_cparams("parallel"),
        name="out_proj",
    )(h, a, b, w)


def _dense_ffn_kernel(h_ref, g_ref, wg_ref, wu_ref, wd_ref, o_ref, xn_ref):
    f = pl.program_id(1)

    @pl.when(f == 0)
    def _():
        x = h_ref[...]
        xn_ref[...] = _rms(x, g_ref[...]).astype(BF16)
        o_ref[...] = x

    xn = xn_ref[...]
    a = jax.nn.silu(_dot(xn, wg_ref[...])) * _dot(xn, wu_ref[...])
    o_ref[...] += _dot(a.astype(BF16), wd_ref[...])


def _dense_ffn(h, g, wg, wu, wd, f_tile):
    s = h.shape[0]
    tm = min(ROW_TILE, s)
    ffn = wg.shape[1]
    return pl.pallas_call(
        _dense_ffn_kernel,
        grid=(s // tm, ffn // f_tile),
        in_specs=[pl.BlockSpec((tm, D_MODEL), lambda i, f: (i, 0)),
                  pl.BlockSpec((1, D_MODEL), lambda i, f: (0, 0)),
                  pl.BlockSpec((D_MODEL, f_tile), lambda i, f: (0, f)),
                  pl.BlockSpec((D_MODEL, f_tile), lambda i, f: (0, f)),
                  pl.BlockSpec((f_tile, D_MODEL), lambda i, f: (f, 0))],
        out_specs=pl.BlockSpec((tm, D_MODEL), lambda i, f: (i, 0)),
        out_shape=jax.ShapeDtypeStruct((s, D_MODEL), F32),
        scratch_shapes=[pltpu.VMEM((tm, D_MODEL), BF16)],
        compiler_params=_cparams("parallel", "arbitrary"),
        name="dense_ffn",
    )(h, g, wg, wu, wd)


def _row_copy(src, dst, sem):
    return pltpu.make_async_copy(src, dst, sem)


def _expert_ffn_kernel(te_ref, nu_ref, src_ref, x_hbm, wg_ref, wu_ref, wd_ref, o_ref, xbuf_ref, xb_ref, sem,
                       *, n_f):
    r = pl.program_id(0)
    f = pl.program_id(1)
    tm = o_ref.shape[0]
    n_used = nu_ref[0]

    def tile_start(tile, slot):
        for t in range(tm):
            _row_copy(x_hbm.at[pl.ds(src_ref[tile * tm + t], 1)], xbuf_ref.at[slot, pl.ds(t, 1)],
                      sem.at[slot]).start()

    def tile_wait(slot):
        _row_copy(x_hbm.at[pl.ds(0, tm)], xbuf_ref.at[slot], sem.at[slot]).wait()

    @pl.when(r < n_used)
    def _():
        @pl.when(r == 0)
        def _():
            @pl.when(f == 0)
            def _():
                tile_start(0, 0)

        for slot in range(2):
            @pl.when(jnp.logical_and(f == 0, r % 2 == slot))
            def _():
                tile_wait(slot)
                xb_ref[...] = xbuf_ref[slot].astype(BF16)
                o_ref[...] = jnp.zeros(o_ref.shape, F32)
                tile_start(jnp.minimum(r + 1, pl.num_programs(0) - 1), 1 - slot)

        xb = xb_ref[...]
        a = jax.nn.silu(_dot(xb, wg_ref[0])) * _dot(xb, wu_ref[0])
        o_ref[...] += _dot(a.astype(BF16), wd_ref[0])

        for slot in range(2):
            @pl.when(jnp.logical_and(jnp.logical_and(r == n_used - 1, f == n_f - 1), r % 2 == slot))
            def _():
                tile_wait(1 - slot)

    @pl.when(jnp.logical_and(r >= n_used, f == 0))
    def _():
        o_ref[...] = jnp.zeros(o_ref.shape, F32)


def _expert_ffn(tile_expert, n_used, src, xn, wg, wu, wd, f_tile):
    rows = src.shape[0]
    tm = MOE_TILE
    ffn = wg.shape[2]
    grid_spec = pltpu.PrefetchScalarGridSpec(
        num_scalar_prefetch=3,
        grid=(rows // tm, ffn // f_tile),
        in_specs=[pl.BlockSpec(memory_space=pl.ANY),
                  pl.BlockSpec((1, D_MODEL, f_tile), lambda r, f, te, nu, sr: (te[r], 0, jnp.where(r < nu[0], f, 0))),
                  pl.BlockSpec((1, D_MODEL, f_tile), lambda r, f, te, nu, sr: (te[r], 0, jnp.where(r < nu[0], f, 0))),
                  pl.BlockSpec((1, f_tile, D_MODEL), lambda r, f, te, nu, sr: (te[r], jnp.where(r < nu[0], f, 0), 0))],
        out_specs=pl.BlockSpec((tm, D_MODEL), lambda r, f, te, nu, sr: (r, 0)),
        scratch_shapes=[pltpu.VMEM((2, tm, D_MODEL), F32), pltpu.VMEM((tm, D_MODEL), BF16),
                        pltpu.SemaphoreType.DMA((2,))],
    )
    return pl.pallas_call(
        functools.partial(_expert_ffn_kernel, n_f=ffn // f_tile),
        grid_spec=grid_spec,
        out_shape=jax.ShapeDtypeStruct((rows, D_MODEL), F32),
        compiler_params=_cparams("arbitrary", "arbitrary"),
        name="expert_ffn",
    )(tile_expert, n_used, src, xn, wg, wu, wd)


def _odd_in_kernel(h_ref, g_ref, w_ref, wkpe_ref, cos_ref, sin_ref, pw_ref, ps_ref,
                   qg_ref, wuq_ref, kvg_ref, wk_ref, wv_ref,
                   oc_ref, q_ref, k_ref, v_ref, halo_ref):
    tm = h_ref.shape[0]
    i = pl.program_id(0)
    xn = _rms(h_ref[...], g_ref[...]).astype(BF16)
    cosf = cos_ref[...]
    sins = sin_ref[...]
    lane = lax.broadcasted_iota(jnp.int32, (tm, LANES), 1)
    lo_mask = lane < (D_NOPE + D_ROPE // 2)

    @pl.when(i == 0)
    def _():
        halo_ref[...] = jnp.zeros(halo_ref.shape, F32)

    xc = _dot(xn, w_ref[:, 0:C_WIDTH])
    ext = jnp.concatenate([halo_ref[...], xc], axis=0)
    halo_ref[...] = xc[tm - C_HALO:tm, :]
    pos = i * tm + lax.broadcasted_iota(jnp.int32, (tm, 1), 0)
    gc = C_WIDTH // len(C_WINDOWS)
    for g, win in enumerate(C_WINDOWS):
        cs = slice(g * gc, (g + 1) * gc)
        a = ext[:, cs]
        span = 1
        while span < win:
            n = a.shape[0] - span
            a = a[span:span + n, :] + a[0:n, :]
            span *= 2
        start = C_HALO - (win - 1)
        wsum = a[start:start + tm, :]
        cnt = jnp.minimum(pos + 1, win).astype(F32)
        pooled = (wsum / cnt - xc[:, cs]).astype(BF16)
        oc_ref[:, cs] = (_dot(pooled, pw_ref[g]) * ps_ref[:, cs]).astype(BF16)

    cq = _dot(xn, w_ref[:, C_WIDTH:C_WIDTH + D_Q_RANK])
    q = _dot(_rms(cq, qg_ref[...]).astype(BF16), wuq_ref[...])
    q = _rope_wide(q, cosf, sins, D_ROPE // 2, lo_mask) * ((D_NOPE + D_ROPE) ** -0.5 * LOG2E)
    q_ref[...] = q.astype(BF16)

    ckv = _dot(xn, w_ref[:, C_WIDTH + D_Q_RANK:C_WIDTH + D_Q_RANK + D_KV_RANK])
    ckvn = _rms(ckv, kvg_ref[...]).astype(BF16)
    kpe = _rope_block(_dot(xn, wkpe_ref[...]), cosf, sins, D_ROPE // 2, lo_mask)
    kn = _dot(ckvn, wk_ref[...])
    k_ref[...] = jnp.concatenate(
        [kn[:, b * LANES:(b + 1) * LANES] + kpe for b in range(D_HEADS)], axis=1).astype(BF16)
    _store_vt(v_ref, _dot(ckvn, wv_ref[...]))


def _odd_in(h, g, w, wkpe, cosf, sins, pool_w, pool_scale, qg, wuq, kvg, wk, wv):
    s = h.shape[0]
    tm = min(ROW_TILE, s)
    row_spec = lambda width: pl.BlockSpec((tm, width), lambda i: (i, 0))
    full = lambda shape: pl.BlockSpec(shape, lambda i: (0,) * len(shape))
    return pl.pallas_call(
        _odd_in_kernel,
        grid=(s // tm,),
        in_specs=[row_spec(D_MODEL), full((1, D_MODEL)), full(w.shape), full(wkpe.shape),
                  row_spec(LANES), row_spec(LANES), full(pool_w.shape), full((1, C_WIDTH)),
                  full((1, D_Q_RANK)), full(wuq.shape), full((1, D_KV_RANK)), full(wk.shape), full(wv.shape)],
        out_specs=[row_spec(C_WIDTH), row_spec(D_HEADS * D_HEAD_PAD), row_spec(D_HEADS * D_HEAD_PAD),
                   pl.BlockSpec((D_HEADS // 2 * VT_ROWS, tm), lambda i: (0, i))],
        out_shape=[jax.ShapeDtypeStruct((s, C_WIDTH), BF16),
                   jax.ShapeDtypeStruct((s, D_HEADS * D_HEAD_PAD), BF16),
                   jax.ShapeDtypeStruct((s, D_HEADS * D_HEAD_PAD), BF16),
                   jax.ShapeDtypeStruct((D_HEADS // 2 * VT_ROWS, s), BF16)],
        scratch_shapes=[pltpu.VMEM((C_HALO, C_WIDTH), F32)],
        compiler_params=_cparams("arbitrary"),
        name="odd_in",
    )(h, g, w, wkpe, cosf, sins, pool_w, pool_scale, qg, wuq, kvg, wk, wv)


def _router_kernel(h_ref, g_ref, rt_ref, xn_ref, eidx_ref, rank_ref, wcol_ref, cnt_ref, run_ref):
    tm = h_ref.shape[0]
    i = pl.program_id(0)

    @pl.when(i == 0)
    def _():
        run_ref[...] = jnp.zeros(run_ref.shape, F32)

    xn = _rms(h_ref[...], g_ref[...])
    xn_ref[...] = xn
    lg = lax.dot_general(rt_ref[...], xn, (((1,), (1,)), ((), ())),
                         preferred_element_type=F32, precision=lax.Precision.HIGHEST)
    eio = lax.broadcasted_iota(jnp.int32, lg.shape, 0)
    m1 = jnp.max(lg, axis=0, keepdims=True)
    i1 = jnp.min(jnp.where(lg == m1, eio, N_EXPERTS), axis=0, keepdims=True)
    lg2 = jnp.where(eio == i1, -jnp.inf, lg)
    m2 = jnp.max(lg2, axis=0, keepdims=True)
    i2 = jnp.min(jnp.where(lg2 == m2, eio, N_EXPERTS), axis=0, keepdims=True)
    e = jnp.exp(m2 - m1)
    w1 = 1.0 / (1.0 + e)
    w2 = e / (1.0 + e)

    sel1 = eio == i1
    sel2 = eio == i2
    onehot = jnp.logical_or(sel1, sel2)
    tr = lax.broadcasted_iota(jnp.int32, (tm, tm), 0)
    tc = lax.broadcasted_iota(jnp.int32, (tm, tm), 1)
    upper = (tr < tc).astype(BF16)
    before = _dot(onehot.astype(BF16), upper) + run_ref[:, 0:1]
    r1 = jnp.sum(jnp.where(sel1, before, 0.0), axis=0, keepdims=True)
    r2 = jnp.sum(jnp.where(sel2, before, 0.0), axis=0, keepdims=True)
    run_new = run_ref[:, 0:1] + jnp.sum(onehot.astype(F32), axis=1, keepdims=True)
    run_ref[...] = jnp.broadcast_to(run_new, run_ref.shape)
    cnt_ref[...] = jnp.broadcast_to(run_new, cnt_ref.shape).astype(jnp.int32)

    eidx_ref[...] = jnp.concatenate([i1, i2], axis=0)
    rank_ref[...] = jnp.concatenate([r1, r2], axis=0).astype(jnp.int32)
    w8 = jnp.concatenate([w1, w2, jnp.zeros((N_EXPERTS - 2, tm), F32)], axis=0)
    eye = (tr == tc).astype(F32)
    wcol_ref[...] = lax.dot_general(eye, w8, (((1,), (1,)), ((), ())),
                                    preferred_element_type=F32, precision=lax.Precision.HIGHEST)


def _router(h, g, router_t):
    s = h.shape[0]
    tm = min(ROW_TILE, s)
    return pl.pallas_call(
        _router_kernel,
        grid=(s // tm,),
        in_specs=[pl.BlockSpec((tm, D_MODEL), lambda i: (i, 0)),
                  pl.BlockSpec((1, D_MODEL), lambda i: (0, 0)),
                  pl.BlockSpec((N_EXPERTS, D_MODEL), lambda i: (0, 0))],
        out_specs=[pl.BlockSpec((tm, D_MODEL), lambda i: (i, 0)),
                   pl.BlockSpec((2, tm), lambda i: (0, i)),
                   pl.BlockSpec((2, tm), lambda i: (0, i)),
                   pl.BlockSpec((tm, N_EXPERTS), lambda i: (i, 0)),
                   pl.BlockSpec((N_EXPERTS, LANES), lambda i: (0, 0))],
        out_shape=[jax.ShapeDtypeStruct((s, D_MODEL), F32),
                   jax.ShapeDtypeStruct((2, s), jnp.int32),
                   jax.ShapeDtypeStruct((2, s), jnp.int32),
                   jax.ShapeDtypeStruct((s, N_EXPERTS), F32),
                   jax.ShapeDtypeStruct((N_EXPERTS, LANES), jnp.int32)],
        scratch_shapes=[pltpu.VMEM((N_EXPERTS, LANES), F32)],
        compiler_params=_cparams("arbitrary"),
        name="router",
    )(h, g, router_t)


def _invert_kernel(pos_ref, src_ref):
    n_tok = pos_ref.shape[0] // 2

    def clear(i, carry):
        src_ref[i] = 0
        return carry

    lax.fori_loop(0, src_ref.shape[0], clear, 0, unroll=8)

    def put(t, carry):
        src_ref[pos_ref[t]] = t
        src_ref[pos_ref[n_tok + t]] = t
        return carry

    lax.fori_loop(0, n_tok, put, 0, unroll=8)


def _invert(pos_flat, n_slots):
    return pl.pallas_call(
        _invert_kernel,
        in_specs=[pl.BlockSpec(memory_space=pltpu.SMEM)],
        out_specs=pl.BlockSpec(memory_space=pltpu.SMEM),
        out_shape=jax.ShapeDtypeStruct((n_slots,), jnp.int32),
        name="invert",
    )(pos_flat)


def _combine_kernel(pos_ref, ys_hbm, h_ref, wcol_ref, fg_ref, o_ref, buf_ref, sem, *, final_norm):
    tm = h_ref.shape[0]

    for t in range(tm):
        for k in range(2):
            _row_copy(ys_hbm.at[pl.ds(pos_ref[k, t], 1)], buf_ref.at[k, pl.ds(t, 1)], sem).start()
    for k in range(2):
        _row_copy(ys_hbm.at[pl.ds(0, tm)], buf_ref.at[k], sem).wait()
    out = h_ref[...] + wcol_ref[:, 0:1] * buf_ref[0] + wcol_ref[:, 1:2] * buf_ref[1]
    if final_norm:
        out = _rms(out, fg_ref[...])
    o_ref[...] = out


def _combine(pos, ys, h, wcol, final_g, final_norm):
    s = h.shape[0]
    tm = min(ROW_TILE, s)
    return pl.pallas_call(
        functools.partial(_combine_kernel, final_norm=final_norm),
        grid=(s // tm,),
        in_specs=[pl.BlockSpec((2, tm), lambda i: (0, i), memory_space=pltpu.SMEM),
                  pl.BlockSpec(memory_space=pl.ANY),
                  pl.BlockSpec((tm, D_MODEL), lambda i: (i, 0)),
                  pl.BlockSpec((tm, N_EXPERTS), lambda i: (i, 0)),
                  pl.BlockSpec((1, D_MODEL), lambda i: (0, 0))],
        out_specs=pl.BlockSpec((tm, D_MODEL), lambda i: (i, 0)),
        out_shape=jax.ShapeDtypeStruct((s, D_MODEL), F32),
        scratch_shapes=[pltpu.VMEM((2, tm, D_MODEL), F32), pltpu.SemaphoreType.DMA(())],
        compiler_params=_cparams("arbitrary"),
        name="combine",
    )(pos, ys, h, wcol, final_g)


def _final_norm_kernel(h_ref, g_ref, o_ref):
    o_ref[...] = _rms(h_ref[...], g_ref[...])


def _final_norm(h, g):
    s = h.shape[0]
    tm = min(ROW_TILE, s)
    return pl.pallas_call(
        _final_norm_kernel,
        grid=(s // tm,),
        in_specs=[pl.BlockSpec((tm, D_MODEL), lambda i: (i, 0)), pl.BlockSpec((1, D_MODEL), lambda i: (0, 0))],
        out_specs=pl.BlockSpec((tm, D_MODEL), lambda i: (i, 0)),
        out_shape=jax.ShapeDtypeStruct((s, D_MODEL), F32),
        compiler_params=_cparams("parallel"),
        name="final_norm",
    )(h, g)


def _rope_lane_tables(seq, rot_dim, first_lane, period):
    half = rot_dim // 2
    pos = jnp.arange(seq, dtype=F32)
    inv = ROPE_THETA ** (-jnp.arange(0, rot_dim, 2, dtype=F32) / rot_dim)
    ang = pos[:, None] * inv[None, :]
    cos, sin = jnp.cos(ang), jnp.sin(ang)
    lane = jnp.arange(LANES) % period - first_lane
    lo = (lane >= 0) & (lane < half)
    hi = (lane >= half) & (lane < rot_dim)
    idx = jnp.clip(jnp.where(hi, lane - half, lane), 0, half - 1)
    cosf = jnp.where((lo | hi)[None, :], cos[:, idx], 1.0)
    sins = jnp.where(lo[None, :], -sin[:, idx], jnp.where(hi[None, :], sin[:, idx], 0.0))
    return cosf, sins


def _even_layer(h, l, p, i, rope_a):
    w_in = p["w_in_even"][i].astype(BF16)
    q, k, v, ob = _even_in(
        h, p["norm_mix_even"][i][None, :], w_in, rope_a[0], rope_a[1],
        p["sgu_ln_g"][i][None, :], p["sgu_ln_b"][i][None, :], p["sgu_w"][i], p["sgu_b"][i].T)
    lam_init = 0.8 - 0.6 * math.exp(-0.3 * l)
    oa = _diff_attn(q, k, v, p["lam_q1"][i][None, :], p["lam_k1"][i][None, :],
                    p["lam_q2"][i][None, :], p["lam_k2"][i][None, :], p["subln_g"][i][:, None], lam_init)
    h = _out_proj(h, oa, ob, p["w_out_even"][i].astype(BF16))
    ffn = p["ffn_wg"].shape[2]
    return _dense_ffn(h, p["norm_ffn_even"][i][None, :], p["ffn_wg"][i].astype(BF16),
                      p["ffn_wu"][i].astype(BF16), p["ffn_wd"][i].astype(BF16), ffn // 2)


def _odd_layer(h, p, i, rope_d, final_g, final_norm):
    s = h.shape[0]
    w_in = p["w_in_odd"][i]
    n_main = C_WIDTH + D_Q_RANK + D_KV_RANK
    w_main = w_in[:, :n_main].astype(BF16)
    wkpe = jnp.zeros((D_MODEL, LANES), F32).at[:, D_NOPE:D_NOPE + D_ROPE].set(w_in[:, n_main:]).astype(BF16)
    wuq = jnp.pad(p["w_uq"][i].reshape(D_Q_RANK, D_HEADS, D_NOPE + D_ROPE),
                  ((0, 0), (0, 0), (0, D_HEAD_PAD - D_NOPE - D_ROPE))).reshape(D_Q_RANK, -1).astype(BF16)
    wukv = p["w_ukv"][i].reshape(D_KV_RANK, D_HEADS, D_NOPE + D_V)
    wk = jnp.pad(wukv[:, :, :D_NOPE], ((0, 0), (0, 0), (0, D_HEAD_PAD - D_NOPE))).reshape(D_KV_RANK, -1).astype(BF16)
    wv = wukv[:, :, D_NOPE:].reshape(D_KV_RANK, -1).astype(BF16)
    oc, q, k, v = _odd_in(
        h, p["norm_mix_odd"][i][None, :], w_main, wkpe, rope_d[0], rope_d[1],
        p["pool_w"][i].astype(BF16), p["pool_scale"][i][None, :],
        p["q_norm_g"][i][None, :], wuq, p["kv_norm_g"][i][None, :], wk, wv)
    od = _mla_attn(q, k, v)
    h = _out_proj(h, oc, od, p["w_out_odd"][i].astype(BF16))

    xn, eidx, rank, wcol, cnt = _router(h, p["norm_ffn_odd"][i][None, :], p["router"][i].T)
    counts = cnt[:, 0]
    padded = (counts + MOE_TILE - 1) // MOE_TILE * MOE_TILE
    ends = jnp.cumsum(padded)
    base = ends - padded
    pos = rank
    for e in range(N_EXPERTS):
        pos = pos + jnp.where(eidx == e, base[e], 0)
    n_tiles = (2 * s) // MOE_TILE + N_EXPERTS
    tile_start = jnp.arange(n_tiles, dtype=jnp.int32) * MOE_TILE
    tile_expert = jnp.minimum(jnp.sum(tile_start[:, None] >= ends[None, :], axis=1), N_EXPERTS - 1).astype(jnp.int32)
    n_used = (ends[-1:] // MOE_TILE).astype(jnp.int32)
    src = _invert(pos.reshape(-1), n_tiles * MOE_TILE)
    ffn = p["moe_wg"].shape[3]
    ys = _expert_ffn(tile_expert, n_used, src, xn, p["moe_wg"][i].astype(BF16), p["moe_wu"][i].astype(BF16),
                     p["moe_wd"][i].astype(BF16), ffn // 2)
    return _combine(pos, ys, h, wcol, final_g, final_norm)


def kernel(x, norm_mix_even, w_in_even, lam_q1, lam_k1, lam_q2, lam_k2, subln_g, sgu_ln_g, sgu_ln_b, sgu_w, sgu_b, w_out_even, norm_ffn_even, ffn_wg, ffn_wu, ffn_wd, norm_mix_odd, w_in_odd, pool_w, pool_scale, q_norm_g, w_uq, kv_norm_g, w_ukv, w_out_odd, norm_ffn_odd, router, moe_wg, moe_wu, moe_wd, final_norm):
    p = dict(norm_mix_even=norm_mix_even, w_in_even=w_in_even, lam_q1=lam_q1, lam_k1=lam_k1, lam_q2=lam_q2,
             lam_k2=lam_k2, subln_g=subln_g, sgu_ln_g=sgu_ln_g, sgu_ln_b=sgu_ln_b, sgu_w=sgu_w, sgu_b=sgu_b,
             w_out_even=w_out_even, norm_ffn_even=norm_ffn_even, ffn_wg=ffn_wg, ffn_wu=ffn_wu, ffn_wd=ffn_wd,
             norm_mix_odd=norm_mix_odd, w_in_odd=w_in_odd, pool_w=pool_w, pool_scale=pool_scale,
             q_norm_g=q_norm_g, w_uq=w_uq, kv_norm_g=kv_norm_g, w_ukv=w_ukv, w_out_odd=w_out_odd,
             norm_ffn_odd=norm_ffn_odd, router=router, moe_wg=moe_wg, moe_wu=moe_wu, moe_wd=moe_wd)
    b, s, d = x.shape
    depth = norm_mix_even.shape[0] + norm_mix_odd.shape[0]
    rope_a = _rope_lane_tables(s, A_ROT, 0, A_QK_DIM)
    rope_d = _rope_lane_tables(s, D_ROPE, D_NOPE, LANES)
    final_g = final_norm[None, :]
    outs = []
    for bi in range(b):
        h = x[bi]
        for l in range(depth):
            if l % 2 == 0:
                h = _even_layer(h, l, p, l // 2, rope_a)
            else:
                h = _odd_layer(h, p, l // 2, rope_d, final_g, final_norm=(l == depth - 1))
        if depth % 2 == 1:
            h = _final_norm(h, final_g)
        outs.append(h)
    return jnp.stack(outs)
```

```python
import functools
import math

import jax
import jax.numpy as jnp
from jax import lax
from jax.experimental import pallas as pl
from jax.experimental.pallas import tpu as pltpu

F32 = jnp.float32
BF16 = jnp.bfloat16

ROPE_THETA = 500000.0
NORM_EPS = 1e-6
LN_EPS = 1e-5
LOG2E = 1.4426950408889634

D_MODEL = 1024
A_HEADS = 4
A_QK_DIM = 64
A_ROT = 16
A_WIDTH = 512
B_WIDTH = 512
B_GROUPS = 4
B_CHUNK = 128
C_WIDTH = 512
C_WINDOWS = (2, 4, 8, 16)
C_HALO = 16
D_HEADS = 8
D_NOPE = 64
D_ROPE = 32
D_V = 64
D_Q_RANK = 384
D_KV_RANK = 256
D_HEAD_PAD = 128
VT_ROWS = 144
N_EXPERTS = 8

LANES = 128
ROW_TILE = 512
ATTN_TILE = 512
MOE_TILE = 1024
MOE_F_TILE = 512
VMEM_LIMIT = 56 * 1024 * 1024


def _cparams(*sem):
    return pltpu.CompilerParams(dimension_semantics=sem, vmem_limit_bytes=VMEM_LIMIT)


def _rms(x, g):
    return x * lax.rsqrt(jnp.mean(x * x, axis=-1, keepdims=True) + NORM_EPS) * g


def _dot(a, b):
    return jnp.dot(a, b, preferred_element_type=F32)


def _rope_block(x, cosf, sins, shift, lo_mask):
    up = pltpu.roll(x, LANES - shift, 1)
    dn = pltpu.roll(x, shift, 1)
    return x * cosf + jnp.where(lo_mask, up, dn) * sins


def _rope_wide(x, cosf, sins, shift, lo_mask):
    nb = x.shape[1] // LANES
    return jnp.concatenate(
        [_rope_block(x[:, b * LANES:(b + 1) * LANES], cosf, sins, shift, lo_mask) for b in range(nb)],
        axis=1)


def _store_vt(vt_ref, v):
    rows = v.shape[0]
    vt = v.T
    r = lax.broadcasted_iota(jnp.int32, (VT_ROWS - LANES, rows), 0)
    tail = jnp.where(r == 0, 1.0, 0.0).astype(BF16)
    for b in range(v.shape[1] // LANES):
        vt_ref[b * VT_ROWS:b * VT_ROWS + LANES, :] = vt[b * LANES:(b + 1) * LANES, :].astype(BF16)
        vt_ref[b * VT_ROWS + LANES:(b + 1) * VT_ROWS, :] = tail


def _even_in_kernel(h_ref, g_ref, w_ref, cos_ref, sin_ref, lng_ref, lnb_ref, sw_ref, sbt_ref,
                    q_ref, k_ref, v_ref, ob_ref):
    tm = h_ref.shape[0]
    xn = _rms(h_ref[...], g_ref[...]).astype(BF16)
    cosf = cos_ref[...]
    sins = sin_ref[...]
    lane = lax.broadcasted_iota(jnp.int32, (tm, LANES), 1)
    lo_mask = (lane % A_QK_DIM) < (A_ROT // 2)

    q = _dot(xn, w_ref[:, 0:A_WIDTH])
    q = _rope_wide(q, cosf, sins, A_ROT // 2, lo_mask) * (A_QK_DIM ** -0.5 * LOG2E)
    q_ref[...] = q.astype(BF16)
    k = _dot(xn, w_ref[:, A_WIDTH:2 * A_WIDTH])
    k_ref[...] = _rope_wide(k, cosf, sins, A_ROT // 2, lo_mask).astype(BF16)
    _store_vt(v_ref, _dot(xn, w_ref[:, 2 * A_WIDTH:3 * A_WIDTH]))

    ub = jax.nn.gelu(_dot(xn, w_ref[:, 3 * A_WIDTH:3 * A_WIDTH + B_WIDTH]))
    vb = jax.nn.gelu(_dot(xn, w_ref[:, 3 * A_WIDTH + B_WIDTH:3 * A_WIDTH + 2 * B_WIDTH]))
    mu = jnp.mean(vb, axis=-1, keepdims=True)
    var = jnp.mean(jnp.square(vb - mu), axis=-1, keepdims=True)
    vn = ((vb - mu) * lax.rsqrt(var + LN_EPS) * lng_ref[...] + lnb_ref[...]).astype(BF16)

    row = lax.broadcasted_iota(jnp.int32, (B_CHUNK, B_CHUNK), 0)
    col = lax.broadcasted_iota(jnp.int32, (B_CHUNK, B_CHUNK), 1)
    causal = col <= row
    gc = B_WIDTH // B_GROUPS
    for g in range(B_GROUPS):
        wg = jnp.where(causal, sw_ref[g], 0.0).astype(BF16)
        bias = sbt_ref[:, g:g + 1]
        for c in range(tm // B_CHUNK):
            rs = slice(c * B_CHUNK, (c + 1) * B_CHUNK)
            cs = slice(g * gc, (g + 1) * gc)
            mixed = _dot(wg, vn[rs, cs]) + bias
            ob_ref[rs, cs] = (ub[rs, cs] * mixed).astype(BF16)


def _even_in(h, g, w, cosf, sins, lng, lnb, sgu_w, sgu_bt):
    s = h.shape[0]
    tm = min(ROW_TILE, s)
    n_in = w.shape[1]
    row_spec = lambda width: pl.BlockSpec((tm, width), lambda i: (i, 0))
    full = lambda shape: pl.BlockSpec(shape, lambda i: (0,) * len(shape))
    out = jax.ShapeDtypeStruct((s, A_WIDTH), BF16)
    return pl.pallas_call(
        _even_in_kernel,
        grid=(s // tm,),
        in_specs=[row_spec(D_MODEL), full((1, D_MODEL)), full((D_MODEL, n_in)),
                  row_spec(LANES), row_spec(LANES), full((1, B_WIDTH)), full((1, B_WIDTH)),
                  full((B_GROUPS, B_CHUNK, B_CHUNK)), full((B_CHUNK, B_GROUPS))],
        out_specs=[row_spec(A_WIDTH), row_spec(A_WIDTH),
                   pl.BlockSpec((A_HEADS * VT_ROWS, tm), lambda i: (0, i)), row_spec(A_WIDTH)],
        out_shape=[out, out, jax.ShapeDtypeStruct((A_HEADS * VT_ROWS, s), BF16), out],
        compiler_params=_cparams("parallel"),
        name="even_in",
    )(h, g, w, cosf, sins, lng, lnb, sgu_w, sgu_bt)


VT_ONES = LANES


def _flash_core(qi, t, score_fn, vt_ref, s_a, s_b, m_ref, acc_ref):
    m_ref[...] = jnp.full(m_ref.shape, -jnp.inf, F32)
    acc_ref[...] = jnp.zeros(acc_ref.shape, F32)

    def process(s_ref, j, diagonal):
        s = s_ref[...]
        if diagonal:
            key = lax.broadcasted_iota(jnp.int32, s.shape, 0)
            qry = lax.broadcasted_iota(jnp.int32, s.shape, 1) % t
            s = jnp.where(key <= qry, s, -jnp.inf)
        m_prev = m_ref[...]
        m_new = jnp.maximum(m_prev, jnp.max(s, axis=0, keepdims=True))
        alpha = jnp.exp2(m_prev - m_new)
        p = jnp.exp2(s - m_new).astype(BF16)
        off = pl.multiple_of(j * t, t)
        acc_ref[...] = acc_ref[...] * alpha + _dot(vt_ref[:, pl.ds(off, t)], p)
        m_ref[...] = m_new

    score_fn(0, s_a)

    def pair(j):
        score_fn(j + 1, s_b)
        process(s_a, j, False)
        score_fn(j + 2, s_a)
        process(s_b, j + 1, False)

    def body4(it, carry):
        pair(4 * it)
        pair(4 * it + 2)
        return carry

    def body2(it, carry):
        pair(2 * it)
        return carry

    lax.fori_loop(0, qi // 4, body4, 0)
    lax.fori_loop(qi // 4 * 2, qi // 2, body2, 0)

    @pl.when(qi % 2 == 1)
    def _():
        score_fn(qi, s_b)
        process(s_a, qi - 1, False)
        process(s_b, qi, True)

    @pl.when(qi % 2 == 0)
    def _():
        process(s_a, qi, True)


def _attn_scratch(t):
    return [pltpu.VMEM((LANES, 2 * t), BF16), pltpu.VMEM((t, 2 * t), F32), pltpu.VMEM((t, 2 * t), F32),
            pltpu.VMEM((1, 2 * t), F32), pltpu.VMEM((VT_ROWS, 2 * t), F32)]


def _diff_attn_kernel(q_ref, k_ref, vt_ref, lq1_ref, lk1_ref, lq2_ref, lk2_ref, sg_ref, o_ref,
                      qt_ref, s_a, s_b, m_ref, acc_ref, *, lam_init):
    t = q_ref.shape[0]
    qi = pl.program_id(1)
    qt = q_ref[...].astype(F32).T
    row = lax.broadcasted_iota(jnp.int32, qt.shape, 0)
    qt_ref[:, 0:t] = jnp.where(row < A_QK_DIM, qt, 0.0).astype(BF16)
    qt_ref[:, t:2 * t] = jnp.where(row >= A_QK_DIM, qt, 0.0).astype(BF16)

    def score_fn(j, dst_ref):
        off = pl.multiple_of(j * t, t)
        dst_ref[...] = _dot(k_ref[pl.ds(off, t), :], qt_ref[...])

    _flash_core(qi, t, score_fn, vt_ref, s_a, s_b, m_ref, acc_ref)

    lam = (jnp.exp(jnp.sum(lq1_ref[...] * lk1_ref[...], axis=-1, keepdims=True))
           - jnp.exp(jnp.sum(lq2_ref[...] * lk2_ref[...], axis=-1, keepdims=True)) + lam_init)
    o = (acc_ref[0:LANES, 0:t] / acc_ref[VT_ONES:VT_ONES + 1, 0:t]
         - lam * (acc_ref[0:LANES, t:2 * t] / acc_ref[VT_ONES:VT_ONES + 1, t:2 * t]))
    o = o * lax.rsqrt(jnp.mean(o * o, axis=0, keepdims=True) + NORM_EPS) * sg_ref[...] * (1.0 - lam_init)
    o_ref[...] = o.T.astype(BF16)


def _diff_attn(q, k, vt, lq1, lk1, lq2, lk2, subln_g, lam_init):
    s = q.shape[0]
    t = min(ATTN_TILE, s)
    hw = 2 * A_QK_DIM
    small = lambda n: pl.BlockSpec((1, n), lambda h, i: (0, 0))
    return pl.pallas_call(
        functools.partial(_diff_attn_kernel, lam_init=lam_init),
        grid=(A_HEADS, s // t),
        in_specs=[pl.BlockSpec((t, hw), lambda h, i: (i, h)),
                  pl.BlockSpec((s, hw), lambda h, i: (0, h)),
                  pl.BlockSpec((VT_ROWS, s), lambda h, i: (h, 0)),
                  small(A_QK_DIM), small(A_QK_DIM), small(A_QK_DIM), small(A_QK_DIM),
                  pl.BlockSpec((hw, 1), lambda h, i: (0, 0))],
        out_specs=pl.BlockSpec((t, hw), lambda h, i: (i, h)),
        out_shape=jax.ShapeDtypeStruct((s, A_WIDTH), BF16),
        scratch_shapes=_attn_scratch(t),
        compiler_params=_cparams("parallel", "parallel"),
        name="diff_attn",
    )(q, k, vt, lq1, lk1, lq2, lk2, subln_g)


def _mla_attn_kernel(q_ref, k_ref, vt_ref, o_ref, qt_ref, s_a, s_b, m_ref, acc_ref):
    t = q_ref.shape[0]
    qi = pl.program_id(1)
    qt_ref[:, 0:t] = q_ref[:, 0:D_HEAD_PAD].astype(F32).T.astype(BF16)
    qt_ref[:, t:2 * t] = q_ref[:, D_HEAD_PAD:2 * D_HEAD_PAD].astype(F32).T.astype(BF16)

    def score_fn(j, dst_ref):
        off = pl.multiple_of(j * t, t)
        dst_ref[:, 0:t] = _dot(k_ref[pl.ds(off, t), 0:D_HEAD_PAD], qt_ref[:, 0:t])
        dst_ref[:, t:2 * t] = _dot(k_ref[pl.ds(off, t), D_HEAD_PAD:2 * D_HEAD_PAD], qt_ref[:, t:2 * t])

    _flash_core(qi, t, score_fn, vt_ref, s_a, s_b, m_ref, acc_ref)

    oa = acc_ref[0:D_V, 0:t] / acc_ref[VT_ONES:VT_ONES + 1, 0:t]
    ob = acc_ref[D_V:2 * D_V, t:2 * t] / acc_ref[VT_ONES:VT_ONES + 1, t:2 * t]
    o_ref[...] = jnp.concatenate([oa, ob], axis=0).T.astype(BF16)


def _mla_attn(q, k, vt):
    s = q.shape[0]
    t = min(ATTN_TILE, s)
    return pl.pallas_call(
        _mla_attn_kernel,
        grid=(D_HEADS // 2, s // t),
        in_specs=[pl.BlockSpec((t, 2 * D_HEAD_PAD), lambda h, i: (i, h)),
                  pl.BlockSpec((s, 2 * D_HEAD_PAD), lambda h, i: (0, h)),
                  pl.BlockSpec((VT_ROWS, s), lambda h, i: (h, 0))],
        out_specs=pl.BlockSpec((t, 2 * D_V), lambda h, i: (i, h)),
        out_shape=jax.ShapeDtypeStruct((s, D_HEADS * D_V), BF16),
        scratch_shapes=_attn_scratch(t),
        compiler_params=_cparams("parallel", "parallel"),
        name="mla_attn",
    )(q, k, vt)


def _mixer_out(h_ref, a_ref, b_ref, wo_ref):
    half = a_ref.shape[1]
    return h_ref[...] + _dot(a_ref[...], wo_ref[0:half, :]) + _dot(b_ref[...], wo_ref[half:2 * half, :])


def _dense_ffn_kernel(h_ref, a_ref, b_ref, wo_ref, g_ref, wg_ref, wu_ref, wd_ref, o_ref, xn_ref):
    f = pl.program_id(1)

    @pl.when(f == 0)
    def _():
        x = _mixer_out(h_ref, a_ref, b_ref, wo_ref)
        xn_ref[...] = _rms(x, g_ref[...]).astype(BF16)
        o_ref[...] = x

    xn = xn_ref[...]
    a = jax.nn.silu(_dot(xn, wg_ref[...])) * _dot(xn, wu_ref[...])
    o_ref[...] += _dot(a.astype(BF16), wd_ref[...])


def _dense_ffn(h, a, b, wo, g, wg, wu, wd, f_tile):
    s = h.shape[0]
    tm = min(ROW_TILE, s)
    ffn = wg.shape[1]
    half = a.shape[1]
    return pl.pallas_call(
        _dense_ffn_kernel,
        grid=(s // tm, ffn // f_tile),
        in_specs=[pl.BlockSpec((tm, D_MODEL), lambda i, f: (i, 0)),
                  pl.BlockSpec((tm, half), lambda i, f: (i, 0)),
                  pl.BlockSpec((tm, half), lambda i, f: (i, 0)),
                  pl.BlockSpec((2 * half, D_MODEL), lambda i, f: (0, 0)),
                  pl.BlockSpec((1, D_MODEL), lambda i, f: (0, 0)),
                  pl.BlockSpec((D_MODEL, f_tile), lambda i, f: (0, f)),
                  pl.BlockSpec((D_MODEL, f_tile), lambda i, f: (0, f)),
                  pl.BlockSpec((f_tile, D_MODEL), lambda i, f: (f, 0))],
        out_specs=pl.BlockSpec((tm, D_MODEL), lambda i, f: (i, 0)),
        out_shape=jax.ShapeDtypeStruct((s, D_MODEL), F32),
        scratch_shapes=[pltpu.VMEM((tm, D_MODEL), BF16)],
        compiler_params=_cparams("parallel", "arbitrary"),
        name="dense_ffn",
    )(h, a, b, wo, g, wg, wu, wd)


def _row_copy(src, dst, sem):
    return pltpu.make_async_copy(src, dst, sem)


def _expert_ffn_kernel(te_ref, nu_ref, src_ref, x_hbm, wg_ref, wu_ref, wd_ref, o_ref, xbuf_ref, xb_ref, sem,
                       *, n_f):
    r = pl.program_id(0)
    f = pl.program_id(1)
    tm = o_ref.shape[0]
    n_used = nu_ref[0]

    def tile_start(tile, slot):
        for t in range(tm):
            _row_copy(x_hbm.at[pl.ds(src_ref[tile * tm + t], 1)], xbuf_ref.at[slot, pl.ds(t, 1)],
                      sem.at[slot]).start()

    def tile_wait(slot):
        _row_copy(x_hbm.at[pl.ds(0, tm)], xbuf_ref.at[slot], sem.at[slot]).wait()

    @pl.when(r < n_used)
    def _():
        @pl.when(jnp.logical_and(r == 0, f == 0))
        def _():
            def first(t, carry):
                _row_copy(x_hbm.at[pl.ds(src_ref[t], 1)], xbuf_ref.at[0, pl.ds(t, 1)], sem.at[0]).start()
                return carry

            lax.fori_loop(0, tm, first, 0)

        for slot in range(2):
            @pl.when(jnp.logical_and(f == 0, r % 2 == slot))
            def _():
                tile_wait(slot)
                xb_ref[...] = xbuf_ref[slot].astype(BF16)
                o_ref[...] = jnp.zeros(o_ref.shape, F32)
                tile_start(jnp.minimum(r + 1, pl.num_programs(0) - 1), 1 - slot)

        xb = xb_ref[...]
        a = jax.nn.silu(_dot(xb, wg_ref[...].astype(BF16))) * _dot(xb, wu_ref[...].astype(BF16))
        o_ref[...] += _dot(a.astype(BF16), wd_ref[...].astype(BF16))

        for slot in range(2):
            @pl.when(jnp.logical_and(jnp.logical_and(r == n_used - 1, f == n_f - 1), r % 2 == slot))
            def _():
                tile_wait(1 - slot)

    @pl.when(jnp.logical_and(r >= n_used, f == 0))
    def _():
        o_ref[...] = jnp.zeros(o_ref.shape, F32)


def _expert_ffn(tile_expert, n_used, src, xn, wg, wu, wd, layer, f_tile):
    rows = src.shape[0]
    tm = MOE_TILE
    ffn = wg.shape[3]
    grid_spec = pltpu.PrefetchScalarGridSpec(
        num_scalar_prefetch=3,
        grid=(rows // tm, ffn // f_tile),
        in_specs=[pl.BlockSpec(memory_space=pl.ANY),
                  pl.BlockSpec((None, None, D_MODEL, f_tile),
                               lambda r, f, te, nu, sr: (layer, te[r], 0, jnp.where(r < nu[0], f, 0))),
                  pl.BlockSpec((None, None, D_MODEL, f_tile),
                               lambda r, f, te, nu, sr: (layer, te[r], 0, jnp.where(r < nu[0], f, 0))),
                  pl.BlockSpec((None, None, f_tile, D_MODEL),
                               lambda r, f, te, nu, sr: (layer, te[r], jnp.where(r < nu[0], f, 0), 0))],
        out_specs=pl.BlockSpec((tm, D_MODEL), lambda r, f, te, nu, sr: (r, 0)),
        scratch_shapes=[pltpu.VMEM((2, tm, D_MODEL), F32), pltpu.VMEM((tm, D_MODEL), BF16),
                        pltpu.SemaphoreType.DMA((2,))],
    )
    return pl.pallas_call(
        functools.partial(_expert_ffn_kernel, n_f=ffn // f_tile),
        grid_spec=grid_spec,
        out_shape=jax.ShapeDtypeStruct((rows, D_MODEL), F32),
        compiler_params=_cparams("arbitrary", "arbitrary"),
        name="expert_ffn",
    )(tile_expert, n_used, src, xn, wg, wu, wd)


def _odd_in_kernel(h_ref, g_ref, w_ref, wkpe_ref, cos_ref, sin_ref, pw_ref, ps_ref,
                   qg_ref, wuq_ref, kvg_ref, wk_ref, wv_ref,
                   oc_ref, q_ref, k_ref, v_ref, halo_ref):
    tm = h_ref.shape[0]
    i = pl.program_id(0)
    xn = _rms(h_ref[...], g_ref[...]).astype(BF16)
    cosf = cos_ref[...]
    sins = sin_ref[...]
    lane = lax.broadcasted_iota(jnp.int32, (tm, LANES), 1)
    lo_mask = lane < (D_NOPE + D_ROPE // 2)

    @pl.when(i == 0)
    def _():
        halo_ref[...] = jnp.zeros(halo_ref.shape, F32)

    xc = _dot(xn, w_ref[:, 0:C_WIDTH])
    ext = jnp.concatenate([halo_ref[...], xc], axis=0)
    halo_ref[...] = xc[tm - C_HALO:tm, :]
    pos = i * tm + lax.broadcasted_iota(jnp.int32, (tm, 1), 0)
    gc = C_WIDTH // len(C_WINDOWS)
    for g, win in enumerate(C_WINDOWS):
        cs = slice(g * gc, (g + 1) * gc)
        a = ext[:, cs]
        span = 1
        while span < win:
            n = a.shape[0] - span
            a = a[span:span + n, :] + a[0:n, :]
            span *= 2
        start = C_HALO - (win - 1)
        wsum = a[start:start + tm, :]
        cnt = jnp.minimum(pos + 1, win).astype(F32)
        pooled = (wsum / cnt - xc[:, cs]).astype(BF16)
        oc_ref[:, cs] = (_dot(pooled, pw_ref[g]) * ps_ref[:, cs]).astype(BF16)

    cq = _dot(xn, w_ref[:, C_WIDTH:C_WIDTH + D_Q_RANK])
    q = _dot(_rms(cq, qg_ref[...]).astype(BF16), wuq_ref[...])
    q = _rope_wide(q, cosf, sins, D_ROPE // 2, lo_mask) * ((D_NOPE + D_ROPE) ** -0.5 * LOG2E)
    q_ref[...] = q.astype(BF16)

    ckv = _dot(xn, w_ref[:, C_WIDTH + D_Q_RANK:C_WIDTH + D_Q_RANK + D_KV_RANK])
    ckvn = _rms(ckv, kvg_ref[...]).astype(BF16)
    kpe = _rope_block(_dot(xn, wkpe_ref[...]), cosf, sins, D_ROPE // 2, lo_mask)
    kn = _dot(ckvn, wk_ref[...])
    k_ref[...] = jnp.concatenate(
        [kn[:, b * LANES:(b + 1) * LANES] + kpe for b in range(D_HEADS)], axis=1).astype(BF16)
    _store_vt(v_ref, _dot(ckvn, wv_ref[...]))


def _odd_in(h, g, w, wkpe, cosf, sins, pool_w, pool_scale, qg, wuq, kvg, wk, wv):
    s = h.shape[0]
    tm = min(ROW_TILE, s)
    row_spec = lambda width: pl.BlockSpec((tm, width), lambda i: (i, 0))
    full = lambda shape: pl.BlockSpec(shape, lambda i: (0,) * len(shape))
    return pl.pallas_call(
        _odd_in_kernel,
        grid=(s // tm,),
        in_specs=[row_spec(D_MODEL), full((1, D_MODEL)), full(w.shape), full(wkpe.shape),
                  row_spec(LANES), row_spec(LANES), full(pool_w.shape), full((1, C_WIDTH)),
                  full((1, D_Q_RANK)), full(wuq.shape), full((1, D_KV_RANK)), full(wk.shape), full(wv.shape)],
        out_specs=[row_spec(C_WIDTH), row_spec(D_HEADS * D_HEAD_PAD), row_spec(D_HEADS * D_HEAD_PAD),
                   pl.BlockSpec((D_HEADS // 2 * VT_ROWS, tm), lambda i: (0, i))],
        out_shape=[jax.ShapeDtypeStruct((s, C_WIDTH), BF16),
                   jax.ShapeDtypeStruct((s, D_HEADS * D_HEAD_PAD), BF16),
                   jax.ShapeDtypeStruct((s, D_HEADS * D_HEAD_PAD), BF16),
                   jax.ShapeDtypeStruct((D_HEADS // 2 * VT_ROWS, s), BF16)],
        scratch_shapes=[pltpu.VMEM((C_HALO, C_WIDTH), F32)],
        compiler_params=_cparams("arbitrary"),
        name="odd_in",
    )(h, g, w, wkpe, cosf, sins, pool_w, pool_scale, qg, wuq, kvg, wk, wv)


def _router_kernel(h_ref, a_ref, b_ref, wo_ref, g_ref, rt_ref,
                   h2_ref, xn_ref, eidx_ref, rank_ref, wcol_ref, cnt_ref, run_ref):
    tm = h_ref.shape[0]
    i = pl.program_id(0)

    @pl.when(i == 0)
    def _():
        run_ref[...] = jnp.zeros(run_ref.shape, F32)

    x = _mixer_out(h_ref, a_ref, b_ref, wo_ref)
    h2_ref[...] = x
    xn = _rms(x, g_ref[...])
    xn_ref[...] = xn
    lg = lax.dot_general(rt_ref[...], xn, (((1,), (1,)), ((), ())),
                         preferred_element_type=F32, precision=lax.Precision.HIGHEST)
    eio = lax.broadcasted_iota(jnp.int32, lg.shape, 0)
    m1 = jnp.max(lg, axis=0, keepdims=True)
    i1 = jnp.min(jnp.where(lg == m1, eio, N_EXPERTS), axis=0, keepdims=True)
    lg2 = jnp.where(eio == i1, -jnp.inf, lg)
    m2 = jnp.max(lg2, axis=0, keepdims=True)
    i2 = jnp.min(jnp.where(lg2 == m2, eio, N_EXPERTS), axis=0, keepdims=True)
    e = jnp.exp(m2 - m1)
    w1 = 1.0 / (1.0 + e)
    w2 = e / (1.0 + e)

    sel1 = eio == i1
    sel2 = eio == i2
    onehot = jnp.logical_or(sel1, sel2)
    tr = lax.broadcasted_iota(jnp.int32, (tm, tm), 0)
    tc = lax.broadcasted_iota(jnp.int32, (tm, tm), 1)
    upper = (tr < tc).astype(BF16)
    before = _dot(onehot.astype(BF16), upper) + run_ref[:, 0:1]
    r1 = jnp.sum(jnp.where(sel1, before, 0.0), axis=0, keepdims=True)
    r2 = jnp.sum(jnp.where(sel2, before, 0.0), axis=0, keepdims=True)
    run_new = run_ref[:, 0:1] + jnp.sum(onehot.astype(F32), axis=1, keepdims=True)
    run_ref[...] = jnp.broadcast_to(run_new, run_ref.shape)
    cnt_ref[...] = jnp.broadcast_to(run_new, cnt_ref.shape).astype(jnp.int32)

    eidx_ref[...] = jnp.concatenate([i1, i2], axis=0)
    rank_ref[...] = jnp.concatenate([r1, r2], axis=0).astype(jnp.int32)
    w8 = jnp.concatenate([w1, w2, jnp.zeros((N_EXPERTS - 2, tm), F32)], axis=0)
    eye = (tr == tc).astype(F32)
    wcol_ref[...] = lax.dot_general(eye, w8, (((1,), (1,)), ((), ())),
                                    preferred_element_type=F32, precision=lax.Precision.HIGHEST)


def _router(h, a, b, wo, g, router_t):
    s = h.shape[0]
    tm = min(ROW_TILE, s)
    half = a.shape[1]
    return pl.pallas_call(
        _router_kernel,
        grid=(s // tm,),
        in_specs=[pl.BlockSpec((tm, D_MODEL), lambda i: (i, 0)),
                  pl.BlockSpec((tm, half), lambda i: (i, 0)),
                  pl.BlockSpec((tm, half), lambda i: (i, 0)),
                  pl.BlockSpec((2 * half, D_MODEL), lambda i: (0, 0)),
                  pl.BlockSpec((1, D_MODEL), lambda i: (0, 0)),
                  pl.BlockSpec((N_EXPERTS, D_MODEL), lambda i: (0, 0))],
        out_specs=[pl.BlockSpec((tm, D_MODEL), lambda i: (i, 0)),
                   pl.BlockSpec((tm, D_MODEL), lambda i: (i, 0)),
                   pl.BlockSpec((2, tm), lambda i: (0, i)),
                   pl.BlockSpec((2, tm), lambda i: (0, i)),
                   pl.BlockSpec((tm, N_EXPERTS), lambda i: (i, 0)),
                   pl.BlockSpec((N_EXPERTS, LANES), lambda i: (0, 0))],
        out_shape=[jax.ShapeDtypeStruct((s, D_MODEL), F32),
                   jax.ShapeDtypeStruct((s, D_MODEL), F32),
                   jax.ShapeDtypeStruct((2, s), jnp.int32),
                   jax.ShapeDtypeStruct((2, s), jnp.int32),
                   jax.ShapeDtypeStruct((s, N_EXPERTS), F32),
                   jax.ShapeDtypeStruct((N_EXPERTS, LANES), jnp.int32)],
        scratch_shapes=[pltpu.VMEM((N_EXPERTS, LANES), F32)],
        compiler_params=_cparams("arbitrary"),
        name="router",
    )(h, a, b, wo, g, router_t)


def _invert_kernel(pos_ref, src_ref):
    n_tok = pos_ref.shape[0] // 2

    def clear(i, carry):
        src_ref[i] = 0
        return carry

    lax.fori_loop(0, src_ref.shape[0], clear, 0, unroll=8)

    def put(t, carry):
        src_ref[pos_ref[t]] = t
        src_ref[pos_ref[n_tok + t]] = t
        return carry

    lax.fori_loop(0, n_tok, put, 0, unroll=8)


def _invert(pos_flat, n_slots):
    return pl.pallas_call(
        _invert_kernel,
        in_specs=[pl.BlockSpec(memory_space=pltpu.SMEM)],
        out_specs=pl.BlockSpec(memory_space=pltpu.SMEM),
        out_shape=jax.ShapeDtypeStruct((n_slots,), jnp.int32),
        name="invert",
    )(pos_flat)


def _combine_kernel(pos_ref, ys_hbm, h_ref, wcol_ref, fg_ref, o_ref, buf_ref, sem, *, final_norm):
    tm = h_ref.shape[0]

    for t in range(tm):
        for k in range(2):
            _row_copy(ys_hbm.at[pl.ds(pos_ref[k, t], 1)], buf_ref.at[k, pl.ds(t, 1)], sem).start()
    for k in range(2):
        _row_copy(ys_hbm.at[pl.ds(0, tm)], buf_ref.at[k], sem).wait()
    out = h_ref[...] + wcol_ref[:, 0:1] * buf_ref[0] + wcol_ref[:, 1:2] * buf_ref[1]
    if final_norm:
        out = _rms(out, fg_ref[...])
    o_ref[...] = out


def _combine(pos, ys, h, wcol, final_g, final_norm):
    s = h.shape[0]
    tm = min(ROW_TILE, s)
    return pl.pallas_call(
        functools.partial(_combine_kernel, final_norm=final_norm),
        grid=(s // tm,),
        in_specs=[pl.BlockSpec((2, tm), lambda i: (0, i), memory_space=pltpu.SMEM),
                  pl.BlockSpec(memory_space=pl.ANY),
                  pl.BlockSpec((tm, D_MODEL), lambda i: (i, 0)),
                  pl.BlockSpec((tm, N_EXPERTS), lambda i: (i, 0)),
                  pl.BlockSpec((1, D_MODEL), lambda i: (0, 0))],
        out_specs=pl.BlockSpec((tm, D_MODEL), lambda i: (i, 0)),
        out_shape=jax.ShapeDtypeStruct((s, D_MODEL), F32),
        scratch_shapes=[pltpu.VMEM((2, tm, D_MODEL), F32), pltpu.SemaphoreType.DMA(())],
        compiler_params=_cparams("arbitrary"),
        name="combine",
    )(pos, ys, h, wcol, final_g)


def _final_norm_kernel(h_ref, g_ref, o_ref):
    o_ref[...] = _rms(h_ref[...], g_ref[...])


def _final_norm(h, g):
    s = h.shape[0]
    tm = min(ROW_TILE, s)
    return pl.pallas_call(
        _final_norm_kernel,
        grid=(s // tm,),
        in_specs=[pl.BlockSpec((tm, D_MODEL), lambda i: (i, 0)), pl.BlockSpec((1, D_MODEL), lambda i: (0, 0))],
        out_specs=pl.BlockSpec((tm, D_MODEL), lambda i: (i, 0)),
        out_shape=jax.ShapeDtypeStruct((s, D_MODEL), F32),
        compiler_params=_cparams("parallel"),
        name="final_norm",
    )(h, g)


def _rope_lane_tables(seq, rot_dim, first_lane, period):
    half = rot_dim // 2
    pos = jnp.arange(seq, dtype=F32)
    inv = ROPE_THETA ** (-jnp.arange(0, rot_dim, 2, dtype=F32) / rot_dim)
    ang = pos[:, None] * inv[None, :]
    cos, sin = jnp.cos(ang), jnp.sin(ang)
    lane = jnp.arange(LANES) % period - first_lane
    lo = (lane >= 0) & (lane < half)
    hi = (lane >= half) & (lane < rot_dim)
    idx = jnp.clip(jnp.where(hi, lane - half, lane), 0, half - 1)
    cosf = jnp.where((lo | hi)[None, :], cos[:, idx], 1.0)
    sins = jnp.where(lo[None, :], -sin[:, idx], jnp.where(hi[None, :], sin[:, idx], 0.0))
    return cosf, sins


def _even_layer(h, l, p, i, rope_a):
    w_in = p["w_in_even"][i].astype(BF16)
    q, k, v, ob = _even_in(
        h, p["norm_mix_even"][i][None, :], w_in, rope_a[0], rope_a[1],
        p["sgu_ln_g"][i][None, :], p["sgu_ln_b"][i][None, :], p["sgu_w"][i], p["sgu_b"][i].T)
    lam_init = 0.8 - 0.6 * math.exp(-0.3 * l)
    oa = _diff_attn(q, k, v, p["lam_q1"][i][None, :], p["lam_k1"][i][None, :],
                    p["lam_q2"][i][None, :], p["lam_k2"][i][None, :], p["subln_g"][i][:, None], lam_init)
    ffn = p["ffn_wg"].shape[2]
    return _dense_ffn(h, oa, ob, p["w_out_even"][i].astype(BF16), p["norm_ffn_even"][i][None, :],
                      p["ffn_wg"][i].astype(BF16), p["ffn_wu"][i].astype(BF16), p["ffn_wd"][i].astype(BF16),
                      ffn // 2)


def _odd_layer(h, p, i, rope_d, final_g, final_norm):
    s = h.shape[0]
    w_in = p["w_in_odd"][i]
    n_main = C_WIDTH + D_Q_RANK + D_KV_RANK
    w_main = w_in[:, :n_main].astype(BF16)
    wkpe = jnp.zeros((D_MODEL, LANES), F32).at[:, D_NOPE:D_NOPE + D_ROPE].set(w_in[:, n_main:]).astype(BF16)
    wuq = jnp.pad(p["w_uq"][i].reshape(D_Q_RANK, D_HEADS, D_NOPE + D_ROPE),
                  ((0, 0), (0, 0), (0, D_HEAD_PAD - D_NOPE - D_ROPE))).reshape(D_Q_RANK, -1).astype(BF16)
    wukv = p["w_ukv"][i].reshape(D_KV_RANK, D_HEADS, D_NOPE + D_V)
    wk = jnp.pad(wukv[:, :, :D_NOPE], ((0, 0), (0, 0), (0, D_HEAD_PAD - D_NOPE))).reshape(D_KV_RANK, -1).astype(BF16)
    wv = wukv[:, :, D_NOPE:].reshape(D_KV_RANK, -1).astype(BF16)
    oc, q, k, v = _odd_in(
        h, p["norm_mix_odd"][i][None, :], w_main, wkpe, rope_d[0], rope_d[1],
        p["pool_w"][i].astype(BF16), p["pool_scale"][i][None, :],
        p["q_norm_g"][i][None, :], wuq, p["kv_norm_g"][i][None, :], wk, wv)
    od = _mla_attn(q, k, v)

    h, xn, eidx, rank, wcol, cnt = _router(h, oc, od, p["w_out_odd"][i].astype(BF16),
                                           p["norm_ffn_odd"][i][None, :], p["router"][i].T)
    counts = cnt[:, 0]
    padded = (counts + MOE_TILE - 1) // MOE_TILE * MOE_TILE
    ends = jnp.cumsum(padded)
    base = ends - padded
    pos = rank
    for e in range(N_EXPERTS):
        pos = pos + jnp.where(eidx == e, base[e], 0)
    n_tiles = (2 * s) // MOE_TILE + N_EXPERTS
    tile_start = jnp.arange(n_tiles, dtype=jnp.int32) * MOE_TILE
    tile_expert = jnp.minimum(jnp.sum(tile_start[:, None] >= ends[None, :], axis=1), N_EXPERTS - 1).astype(jnp.int32)
    n_used = (ends[-1:] // MOE_TILE).astype(jnp.int32)
    src = _invert(pos.reshape(-1), n_tiles * MOE_TILE)
    ys = _expert_ffn(tile_expert, n_used, src, xn, p["moe_wg"], p["moe_wu"], p["moe_wd"], i, MOE_F_TILE)
    return _combine(pos, ys, h, wcol, final_g, final_norm)


def kernel(x, norm_mix_even, w_in_even, lam_q1, lam_k1, lam_q2, lam_k2, subln_g, sgu_ln_g, sgu_ln_b, sgu_w, sgu_b, w_out_even, norm_ffn_even, ffn_wg, ffn_wu, ffn_wd, norm_mix_odd, w_in_odd, pool_w, pool_scale, q_norm_g, w_uq, kv_norm_g, w_ukv, w_out_odd, norm_ffn_odd, router, moe_wg, moe_wu, moe_wd, final_norm):
    p = dict(norm_mix_even=norm_mix_even, w_in_even=w_in_even, lam_q1=lam_q1, lam_k1=lam_k1, lam_q2=lam_q2,
             lam_k2=lam_k2, subln_g=subln_g, sgu_ln_g=sgu_ln_g, sgu_ln_b=sgu_ln_b, sgu_w=sgu_w, sgu_b=sgu_b,
             w_out_even=w_out_even, norm_ffn_even=norm_ffn_even, ffn_wg=ffn_wg, ffn_wu=ffn_wu, ffn_wd=ffn_wd,
             norm_mix_odd=norm_mix_odd, w_in_odd=w_in_odd, pool_w=pool_w, pool_scale=pool_scale,
             q_norm_g=q_norm_g, w_uq=w_uq, kv_norm_g=kv_norm_g, w_ukv=w_ukv, w_out_odd=w_out_odd,
             norm_ffn_odd=norm_ffn_odd, router=router, moe_wg=moe_wg, moe_wu=moe_wu, moe_wd=moe_wd)
    b, s, d = x.shape
    depth = norm_mix_even.shape[0] + norm_mix_odd.shape[0]
    rope_a = _rope_lane_tables(s, A_ROT, 0, A_QK_DIM)
    rope_d = _rope_lane_tables(s, D_ROPE, D_NOPE, LANES)
    final_g = final_norm[None, :]
    outs = []
    for bi in range(b):
        h = x[bi]
        for l in range(depth):
            if l % 2 == 0:
                h = _even_layer(h, l, p, l // 2, rope_a)
            else:
                h = _odd_layer(h, p, l // 2, rope_d, final_g, final_norm=(l == depth - 1))
        if depth % 2 == 1:
            h = _final_norm(h, final_g)
        outs.append(h)
    return jnp.stack(outs)
```

```python
import functools
import math

import jax
import jax.numpy as jnp
from jax import lax
from jax.experimental import pallas as pl
from jax.experimental.pallas import tpu as pltpu

F32 = jnp.float32
BF16 = jnp.bfloat16

ROPE_THETA = 500000.0
NORM_EPS = 1e-6
LN_EPS = 1e-5
LOG2E = 1.4426950408889634

D_MODEL = 1024
A_HEADS = 4
A_QK_DIM = 64
A_ROT = 16
A_WIDTH = 512
B_WIDTH = 512
B_GROUPS = 4
B_CHUNK = 128
C_WIDTH = 512
C_WINDOWS = (2, 4, 8, 16)
C_HALO = 16
D_HEADS = 8
D_NOPE = 64
D_ROPE = 32
D_V = 64
D_Q_RANK = 384
D_KV_RANK = 256
D_HEAD_PAD = 128
VT_ROWS = 144
N_EXPERTS = 8

LANES = 128
ROW_TILE = 512
ATTN_TILE = 512
MOE_TILE = 512
MOE_F_TILE = 1792
VMEM_LIMIT = 56 * 1024 * 1024


def _cparams(*sem):
    return pltpu.CompilerParams(dimension_semantics=sem, vmem_limit_bytes=VMEM_LIMIT)


def _rms(x, g):
    return x * lax.rsqrt(jnp.mean(x * x, axis=-1, keepdims=True) + NORM_EPS) * g


def _dot(a, b):
    return jnp.dot(a, b, preferred_element_type=F32)


def _rope_block(x, cosf, sins, shift, lo_mask):
    up = pltpu.roll(x, LANES - shift, 1)
    dn = pltpu.roll(x, shift, 1)
    return x * cosf + jnp.where(lo_mask, up, dn) * sins


def _rope_wide(x, cosf, sins, shift, lo_mask):
    nb = x.shape[1] // LANES
    return jnp.concatenate(
        [_rope_block(x[:, b * LANES:(b + 1) * LANES], cosf, sins, shift, lo_mask) for b in range(nb)],
        axis=1)


def _store_vt(vt_ref, v):
    rows = v.shape[0]
    vt = v.T
    r = lax.broadcasted_iota(jnp.int32, (VT_ROWS - LANES, rows), 0)
    tail = jnp.where(r == 0, 1.0, 0.0).astype(BF16)
    for b in range(v.shape[1] // LANES):
        vt_ref[b * VT_ROWS:b * VT_ROWS + LANES, :] = vt[b * LANES:(b + 1) * LANES, :].astype(BF16)
        vt_ref[b * VT_ROWS + LANES:(b + 1) * VT_ROWS, :] = tail


def _even_in_kernel(h_ref, g_ref, w_ref, cos_ref, sin_ref, lng_ref, lnb_ref, sw_ref, sbt_ref,
                    q_ref, k_ref, v_ref, ob_ref):
    tm = h_ref.shape[0]
    xn = _rms(h_ref[...], g_ref[...]).astype(BF16)
    cosf = cos_ref[...]
    sins = sin_ref[...]
    lane = lax.broadcasted_iota(jnp.int32, (tm, LANES), 1)
    lo_mask = (lane % A_QK_DIM) < (A_ROT // 2)

    q = _dot(xn, w_ref[:, 0:A_WIDTH])
    q = _rope_wide(q, cosf, sins, A_ROT // 2, lo_mask) * (A_QK_DIM ** -0.5 * LOG2E)
    q_ref[...] = q.astype(BF16)
    k = _dot(xn, w_ref[:, A_WIDTH:2 * A_WIDTH])
    k_ref[...] = _rope_wide(k, cosf, sins, A_ROT // 2, lo_mask).astype(BF16)
    _store_vt(v_ref, _dot(xn, w_ref[:, 2 * A_WIDTH:3 * A_WIDTH]))

    ub = jax.nn.gelu(_dot(xn, w_ref[:, 3 * A_WIDTH:3 * A_WIDTH + B_WIDTH]))
    vb = jax.nn.gelu(_dot(xn, w_ref[:, 3 * A_WIDTH + B_WIDTH:3 * A_WIDTH + 2 * B_WIDTH]))
    mu = jnp.mean(vb, axis=-1, keepdims=True)
    var = jnp.mean(jnp.square(vb - mu), axis=-1, keepdims=True)
    vn = ((vb - mu) * lax.rsqrt(var + LN_EPS) * lng_ref[...] + lnb_ref[...]).astype(BF16)

    row = lax.broadcasted_iota(jnp.int32, (B_CHUNK, B_CHUNK), 0)
    col = lax.broadcasted_iota(jnp.int32, (B_CHUNK, B_CHUNK), 1)
    causal = col <= row
    gc = B_WIDTH // B_GROUPS
    for g in range(B_GROUPS):
        wg = jnp.where(causal, sw_ref[g], 0.0).astype(BF16)
        bias = sbt_ref[:, g:g + 1]
        for c in range(tm // B_CHUNK):
            rs = slice(c * B_CHUNK, (c + 1) * B_CHUNK)
            cs = slice(g * gc, (g + 1) * gc)
            mixed = _dot(wg, vn[rs, cs]) + bias
            ob_ref[rs, cs] = (ub[rs, cs] * mixed).astype(BF16)


def _even_in(h, g, w, cosf, sins, lng, lnb, sgu_w, sgu_bt):
    s = h.shape[0]
    tm = min(ROW_TILE, s)
    n_in = w.shape[1]
    row_spec = lambda width: pl.BlockSpec((tm, width), lambda i: (i, 0))
    full = lambda shape: pl.BlockSpec(shape, lambda i: (0,) * len(shape))
    out = jax.ShapeDtypeStruct((s, A_WIDTH), BF16)
    return pl.pallas_call(
        _even_in_kernel,
        grid=(s // tm,),
        in_specs=[row_spec(D_MODEL), full((1, D_MODEL)), full((D_MODEL, n_in)),
                  row_spec(LANES), row_spec(LANES), full((1, B_WIDTH)), full((1, B_WIDTH)),
                  full((B_GROUPS, B_CHUNK, B_CHUNK)), full((B_CHUNK, B_GROUPS))],
        out_specs=[row_spec(A_WIDTH), row_spec(A_WIDTH),
                   pl.BlockSpec((A_HEADS * VT_ROWS, tm), lambda i: (0, i)), row_spec(A_WIDTH)],
        out_shape=[out, out, jax.ShapeDtypeStruct((A_HEADS * VT_ROWS, s), BF16), out],
        compiler_params=_cparams("parallel"),
        name="even_in",
    )(h, g, w, cosf, sins, lng, lnb, sgu_w, sgu_bt)


VT_ONES = LANES


def _flash_core(qi, t, score_fn, vt_ref, s_a, s_b, m_ref, acc_ref):
    m_ref[...] = jnp.full(m_ref.shape, -jnp.inf, F32)
    acc_ref[...] = jnp.zeros(acc_ref.shape, F32)

    def process(s_ref, j, diagonal):
        s = s_ref[...]
        if diagonal:
            key = lax.broadcasted_iota(jnp.int32, s.shape, 0)
            qry = lax.broadcasted_iota(jnp.int32, s.shape, 1) % t
            s = jnp.where(key <= qry, s, -jnp.inf)
        m_prev = m_ref[...]
        m_new = jnp.maximum(m_prev, jnp.max(s, axis=0, keepdims=True))
        alpha = jnp.exp2(m_prev - m_new)
        p = jnp.exp2(s - m_new).astype(BF16)
        off = pl.multiple_of(j * t, t)
        acc_ref[...] = acc_ref[...] * alpha + _dot(vt_ref[:, pl.ds(off, t)], p)
        m_ref[...] = m_new

    score_fn(0, s_a)

    def pair(j):
        score_fn(j + 1, s_b)
        process(s_a, j, False)
        score_fn(j + 2, s_a)
        process(s_b, j + 1, False)

    def body4(it, carry):
        pair(4 * it)
        pair(4 * it + 2)
        return carry

    def body2(it, carry):
        pair(2 * it)
        return carry

    lax.fori_loop(0, qi // 4, body4, 0)
    lax.fori_loop(qi // 4 * 2, qi // 2, body2, 0)

    @pl.when(qi % 2 == 1)
    def _():
        score_fn(qi, s_b)
        process(s_a, qi - 1, False)
        process(s_b, qi, True)

    @pl.when(qi % 2 == 0)
    def _():
        process(s_a, qi, True)


def _attn_scratch(t):
    return [pltpu.VMEM((LANES, 2 * t), BF16), pltpu.VMEM((t, 2 * t), F32), pltpu.VMEM((t, 2 * t), F32),
            pltpu.VMEM((1, 2 * t), F32), pltpu.VMEM((VT_ROWS, 2 * t), F32)]


def _diff_attn_kernel(q_ref, k_ref, vt_ref, lq1_ref, lk1_ref, lq2_ref, lk2_ref, sg_ref, w0_ref, w1_ref, w2_ref,
                      o_ref, c0_ref, c1_ref, c2_ref, qt_ref, s_a, s_b, m_ref, acc_ref, *, lam_init):
    t = q_ref.shape[0]
    qi = pl.program_id(1)
    c0_ref[...] = w0_ref[...].astype(BF16)
    c1_ref[...] = w1_ref[...].astype(BF16)
    c2_ref[...] = w2_ref[...].astype(BF16)
    qt = q_ref[...].astype(F32).T
    row = lax.broadcasted_iota(jnp.int32, qt.shape, 0)
    qt_ref[:, 0:t] = jnp.where(row < A_QK_DIM, qt, 0.0).astype(BF16)
    qt_ref[:, t:2 * t] = jnp.where(row >= A_QK_DIM, qt, 0.0).astype(BF16)

    def score_fn(j, dst_ref):
        off = pl.multiple_of(j * t, t)
        dst_ref[...] = _dot(k_ref[pl.ds(off, t), :], qt_ref[...])

    _flash_core(qi, t, score_fn, vt_ref, s_a, s_b, m_ref, acc_ref)

    lam = (jnp.exp(jnp.sum(lq1_ref[...] * lk1_ref[...], axis=-1, keepdims=True))
           - jnp.exp(jnp.sum(lq2_ref[...] * lk2_ref[...], axis=-1, keepdims=True)) + lam_init)
    o = (acc_ref[0:LANES, 0:t] / acc_ref[VT_ONES:VT_ONES + 1, 0:t]
         - lam * (acc_ref[0:LANES, t:2 * t] / acc_ref[VT_ONES:VT_ONES + 1, t:2 * t]))
    o = o * lax.rsqrt(jnp.mean(o * o, axis=0, keepdims=True) + NORM_EPS) * sg_ref[...] * (1.0 - lam_init)
    o_ref[...] = o.T.astype(BF16)


def _diff_attn(q, k, vt, lq1, lk1, lq2, lk2, subln_g, lam_init, cast_ws, cast_layer):
    s = q.shape[0]
    t = min(ATTN_TILE, s)
    nq = s // t
    hw = 2 * A_QK_DIM
    small = lambda n: pl.BlockSpec((1, n), lambda h, i: (0, 0))
    steps = A_HEADS * nq
    flat = [w.reshape(w.shape[0], w.shape[1] * w.shape[2], w.shape[3]) for w in cast_ws]
    cast_in = [pl.BlockSpec((None, w.shape[1] // steps, w.shape[2]), lambda h, i: (cast_layer, h * nq + i, 0))
               for w in flat]
    cast_out = [pl.BlockSpec((w.shape[1] // steps, w.shape[2]), lambda h, i: (h * nq + i, 0)) for w in flat]
    outs = pl.pallas_call(
        functools.partial(_diff_attn_kernel, lam_init=lam_init),
        grid=(A_HEADS, nq),
        in_specs=[pl.BlockSpec((t, hw), lambda h, i: (i, h)),
                  pl.BlockSpec((s, hw), lambda h, i: (0, h)),
                  pl.BlockSpec((VT_ROWS, s), lambda h, i: (h, 0)),
                  small(A_QK_DIM), small(A_QK_DIM), small(A_QK_DIM), small(A_QK_DIM),
                  pl.BlockSpec((hw, 1), lambda h, i: (0, 0))] + cast_in,
        out_specs=[pl.BlockSpec((t, hw), lambda h, i: (i, h))] + cast_out,
        out_shape=[jax.ShapeDtypeStruct((s, A_WIDTH), BF16)]
        + [jax.ShapeDtypeStruct(w.shape[1:], BF16) for w in flat],
        scratch_shapes=_attn_scratch(t),
        compiler_params=_cparams("parallel", "parallel"),
        name="diff_attn",
    )(q, k, vt, lq1, lk1, lq2, lk2, subln_g, *flat)
    return outs[0], [c.reshape(w.shape[1:]) for c, w in zip(outs[1:], cast_ws)]


def _mla_attn_kernel(q_ref, k_ref, vt_ref, o_ref, qt_ref, s_a, s_b, m_ref, acc_ref):
    t = q_ref.shape[0]
    qi = pl.program_id(1)
    qt_ref[:, 0:t] = q_ref[:, 0:D_HEAD_PAD].astype(F32).T.astype(BF16)
    qt_ref[:, t:2 * t] = q_ref[:, D_HEAD_PAD:2 * D_HEAD_PAD].astype(F32).T.astype(BF16)

    def score_fn(j, dst_ref):
        off = pl.multiple_of(j * t, t)
        dst_ref[:, 0:t] = _dot(k_ref[pl.ds(off, t), 0:D_HEAD_PAD], qt_ref[:, 0:t])
        dst_ref[:, t:2 * t] = _dot(k_ref[pl.ds(off, t), D_HEAD_PAD:2 * D_HEAD_PAD], qt_ref[:, t:2 * t])

    _flash_core(qi, t, score_fn, vt_ref, s_a, s_b, m_ref, acc_ref)

    oa = acc_ref[0:D_V, 0:t] / acc_ref[VT_ONES:VT_ONES + 1, 0:t]
    ob = acc_ref[D_V:2 * D_V, t:2 * t] / acc_ref[VT_ONES:VT_ONES + 1, t:2 * t]
    o_ref[...] = jnp.concatenate([oa, ob], axis=0).T.astype(BF16)


def _mla_attn(q, k, vt):
    s = q.shape[0]
    t = min(ATTN_TILE, s)
    return pl.pallas_call(
        _mla_attn_kernel,
        grid=(D_HEADS // 2, s // t),
        in_specs=[pl.BlockSpec((t, 2 * D_HEAD_PAD), lambda h, i: (i, h)),
                  pl.BlockSpec((s, 2 * D_HEAD_PAD), lambda h, i: (0, h)),
                  pl.BlockSpec((VT_ROWS, s), lambda h, i: (h, 0))],
        out_specs=pl.BlockSpec((t, 2 * D_V), lambda h, i: (i, h)),
        out_shape=jax.ShapeDtypeStruct((s, D_HEADS * D_V), BF16),
        scratch_shapes=_attn_scratch(t),
        compiler_params=_cparams("parallel", "parallel"),
        name="mla_attn",
    )(q, k, vt)


def _mixer_out(h_ref, a_ref, b_ref, wo_ref):
    half = a_ref.shape[1]
    return h_ref[...] + _dot(a_ref[...], wo_ref[0:half, :]) + _dot(b_ref[...], wo_ref[half:2 * half, :])


def _dense_ffn_kernel(h_ref, a_ref, b_ref, wo_ref, g_ref, wg_ref, wu_ref, wd_ref, o_ref, xn_ref):
    f = pl.program_id(1)

    @pl.when(f == 0)
    def _():
        x = _mixer_out(h_ref, a_ref, b_ref, wo_ref)
        xn_ref[...] = _rms(x, g_ref[...]).astype(BF16)
        o_ref[...] = x

    xn = xn_ref[...]
    a = jax.nn.silu(_dot(xn, wg_ref[...])) * _dot(xn, wu_ref[...])
    o_ref[...] += _dot(a.astype(BF16), wd_ref[...])


def _dense_ffn(h, a, b, wo, g, wg, wu, wd, f_tile):
    s = h.shape[0]
    tm = min(ROW_TILE, s)
    ffn = wg.shape[1]
    half = a.shape[1]
    return pl.pallas_call(
        _dense_ffn_kernel,
        grid=(s // tm, ffn // f_tile),
        in_specs=[pl.BlockSpec((tm, D_MODEL), lambda i, f: (i, 0)),
                  pl.BlockSpec((tm, half), lambda i, f: (i, 0)),
                  pl.BlockSpec((tm, half), lambda i, f: (i, 0)),
                  pl.BlockSpec((2 * half, D_MODEL), lambda i, f: (0, 0)),
                  pl.BlockSpec((1, D_MODEL), lambda i, f: (0, 0)),
                  pl.BlockSpec((D_MODEL, f_tile), lambda i, f: (0, f)),
                  pl.BlockSpec((D_MODEL, f_tile), lambda i, f: (0, f)),
                  pl.BlockSpec((f_tile, D_MODEL), lambda i, f: (f, 0))],
        out_specs=pl.BlockSpec((tm, D_MODEL), lambda i, f: (i, 0)),
        out_shape=jax.ShapeDtypeStruct((s, D_MODEL), F32),
        scratch_shapes=[pltpu.VMEM((tm, D_MODEL), BF16)],
        compiler_params=_cparams("parallel", "arbitrary"),
        name="dense_ffn",
    )(h, a, b, wo, g, wg, wu, wd)


def _row_copy(src, dst, sem):
    return pltpu.make_async_copy(src, dst, sem)


def _expert_ffn_kernel(te_ref, nu_ref, src_ref, x_hbm, wg_ref, wu_ref, wd_ref, o_ref, xbuf_ref, xb_ref, sem,
                       *, n_f):
    r = pl.program_id(0)
    f = pl.program_id(1)
    tm = o_ref.shape[0]
    n_used = nu_ref[0]

    def tile_start(tile, slot):
        for t in range(tm):
            _row_copy(x_hbm.at[pl.ds(src_ref[tile * tm + t], 1)], xbuf_ref.at[slot, pl.ds(t, 1)],
                      sem.at[slot]).start()

    def tile_wait(slot):
        _row_copy(x_hbm.at[pl.ds(0, tm)], xbuf_ref.at[slot], sem.at[slot]).wait()

    @pl.when(r < n_used)
    def _():
        @pl.when(jnp.logical_and(r == 0, f == 0))
        def _():
            def first(t, carry):
                _row_copy(x_hbm.at[pl.ds(src_ref[t], 1)], xbuf_ref.at[0, pl.ds(t, 1)], sem.at[0]).start()
                return carry

            lax.fori_loop(0, tm, first, 0)

        for slot in range(2):
            @pl.when(jnp.logical_and(f == 0, r % 2 == slot))
            def _():
                tile_wait(slot)
                xb_ref[...] = xbuf_ref[slot].astype(BF16)
                o_ref[...] = jnp.zeros(o_ref.shape, F32)
                tile_start(jnp.minimum(r + 1, pl.num_programs(0) - 1), 1 - slot)

        xb = xb_ref[...]
        a = jax.nn.silu(_dot(xb, wg_ref[...])) * _dot(xb, wu_ref[...])
        o_ref[...] += _dot(a.astype(BF16), wd_ref[...])

        for slot in range(2):
            @pl.when(jnp.logical_and(jnp.logical_and(r == n_used - 1, f == n_f - 1), r % 2 == slot))
            def _():
                tile_wait(1 - slot)

    @pl.when(jnp.logical_and(r >= n_used, f == 0))
    def _():
        o_ref[...] = jnp.zeros(o_ref.shape, F32)


def _expert_ffn(tile_expert, n_used, src, xn, wg, wu, wd, f_tile):
    rows = src.shape[0]
    tm = MOE_TILE
    ffn = wg.shape[2]
    grid_spec = pltpu.PrefetchScalarGridSpec(
        num_scalar_prefetch=3,
        grid=(rows // tm, ffn // f_tile),
        in_specs=[pl.BlockSpec(memory_space=pl.ANY),
                  pl.BlockSpec((None, D_MODEL, f_tile),
                               lambda r, f, te, nu, sr: (te[r], 0, jnp.where(r < nu[0], f, 0))),
                  pl.BlockSpec((None, D_MODEL, f_tile),
                               lambda r, f, te, nu, sr: (te[r], 0, jnp.where(r < nu[0], f, 0))),
                  pl.BlockSpec((None, f_tile, D_MODEL),
                               lambda r, f, te, nu, sr: (te[r], jnp.where(r < nu[0], f, 0), 0))],
        out_specs=pl.BlockSpec((tm, D_MODEL), lambda r, f, te, nu, sr: (r, 0)),
        scratch_shapes=[pltpu.VMEM((2, tm, D_MODEL), F32), pltpu.VMEM((tm, D_MODEL), BF16),
                        pltpu.SemaphoreType.DMA((2,))],
    )
    return pl.pallas_call(
        functools.partial(_expert_ffn_kernel, n_f=ffn // f_tile),
        grid_spec=grid_spec,
        out_shape=jax.ShapeDtypeStruct((rows, D_MODEL), F32),
        compiler_params=_cparams("arbitrary", "arbitrary"),
        name="expert_ffn",
    )(tile_expert, n_used, src, xn, wg, wu, wd)


def _odd_in_kernel(h_ref, g_ref, w_ref, wkpe_ref, cos_ref, sin_ref, pw_ref, ps_ref,
                   qg_ref, wuq_ref, kvg_ref, wk_ref, wv_ref,
                   oc_ref, q_ref, k_ref, v_ref, halo_ref):
    tm = h_ref.shape[0]
    i = pl.program_id(0)
    xn = _rms(h_ref[...], g_ref[...]).astype(BF16)
    cosf = cos_ref[...]
    sins = sin_ref[...]
    lane = lax.broadcasted_iota(jnp.int32, (tm, LANES), 1)
    lo_mask = lane < (D_NOPE + D_ROPE // 2)

    @pl.when(i == 0)
    def _():
        halo_ref[...] = jnp.zeros(halo_ref.shape, F32)

    xc = _dot(xn, w_ref[:, 0:C_WIDTH])
    ext = jnp.concatenate([halo_ref[...], xc], axis=0)
    halo_ref[...] = xc[tm - C_HALO:tm, :]
    pos = i * tm + lax.broadcasted_iota(jnp.int32, (tm, 1), 0)
    gc = C_WIDTH // len(C_WINDOWS)
    for g, win in enumerate(C_WINDOWS):
        cs = slice(g * gc, (g + 1) * gc)
        a = ext[:, cs]
        span = 1
        while span < win:
            n = a.shape[0] - span
            a = a[span:span + n, :] + a[0:n, :]
            span *= 2
        start = C_HALO - (win - 1)
        wsum = a[start:start + tm, :]
        cnt = jnp.minimum(pos + 1, win).astype(F32)
        pooled = (wsum / cnt - xc[:, cs]).astype(BF16)
        oc_ref[:, cs] = (_dot(pooled, pw_ref[g]) * ps_ref[:, cs]).astype(BF16)

    cq = _dot(xn, w_ref[:, C_WIDTH:C_WIDTH + D_Q_RANK])
    q = _dot(_rms(cq, qg_ref[...]).astype(BF16), wuq_ref[...])
    q = _rope_wide(q, cosf, sins, D_ROPE // 2, lo_mask) * ((D_NOPE + D_ROPE) ** -0.5 * LOG2E)
    q_ref[...] = q.astype(BF16)

    ckv = _dot(xn, w_ref[:, C_WIDTH + D_Q_RANK:C_WIDTH + D_Q_RANK + D_KV_RANK])
    ckvn = _rms(ckv, kvg_ref[...]).astype(BF16)
    kpe = _rope_block(_dot(xn, wkpe_ref[...]), cosf, sins, D_ROPE // 2, lo_mask)
    kn = _dot(ckvn, wk_ref[...])
    k_ref[...] = jnp.concatenate(
        [kn[:, b * LANES:(b + 1) * LANES] + kpe for b in range(D_HEADS)], axis=1).astype(BF16)
    _store_vt(v_ref, _dot(ckvn, wv_ref[...]))


def _odd_in(h, g, w, wkpe, cosf, sins, pool_w, pool_scale, qg, wuq, kvg, wk, wv):
    s = h.shape[0]
    tm = min(ROW_TILE, s)
    row_spec = lambda width: pl.BlockSpec((tm, width), lambda i: (i, 0))
    full = lambda shape: pl.BlockSpec(shape, lambda i: (0,) * len(shape))
    return pl.pallas_call(
        _odd_in_kernel,
        grid=(s // tm,),
        in_specs=[row_spec(D_MODEL), full((1, D_MODEL)), full(w.shape), full(wkpe.shape),
                  row_spec(LANES), row_spec(LANES), full(pool_w.shape), full((1, C_WIDTH)),
                  full((1, D_Q_RANK)), full(wuq.shape), full((1, D_KV_RANK)), full(wk.shape), full(wv.shape)],
        out_specs=[row_spec(C_WIDTH), row_spec(D_HEADS * D_HEAD_PAD), row_spec(D_HEADS * D_HEAD_PAD),
                   pl.BlockSpec((D_HEADS // 2 * VT_ROWS, tm), lambda i: (0, i))],
        out_shape=[jax.ShapeDtypeStruct((s, C_WIDTH), BF16),
                   jax.ShapeDtypeStruct((s, D_HEADS * D_HEAD_PAD), BF16),
                   jax.ShapeDtypeStruct((s, D_HEADS * D_HEAD_PAD), BF16),
                   jax.ShapeDtypeStruct((D_HEADS // 2 * VT_ROWS, s), BF16)],
        scratch_shapes=[pltpu.VMEM((C_HALO, C_WIDTH), F32)],
        compiler_params=_cparams("arbitrary"),
        name="odd_in",
    )(h, g, w, wkpe, cosf, sins, pool_w, pool_scale, qg, wuq, kvg, wk, wv)


def _router_kernel(h_ref, a_ref, b_ref, wo_ref, g_ref, rt_ref,
                   h2_ref, xn_ref, eidx_ref, rank_ref, wcol_ref, cnt_ref, run_ref):
    tm = h_ref.shape[0]
    i = pl.program_id(0)

    @pl.when(i == 0)
    def _():
        run_ref[...] = jnp.zeros(run_ref.shape, F32)

    x = _mixer_out(h_ref, a_ref, b_ref, wo_ref)
    h2_ref[...] = x
    xn = _rms(x, g_ref[...])
    xn_ref[...] = xn
    lg = lax.dot_general(rt_ref[...], xn, (((1,), (1,)), ((), ())),
                         preferred_element_type=F32, precision=lax.Precision.HIGHEST)
    eio = lax.broadcasted_iota(jnp.int32, lg.shape, 0)
    m1 = jnp.max(lg, axis=0, keepdims=True)
    i1 = jnp.min(jnp.where(lg == m1, eio, N_EXPERTS), axis=0, keepdims=True)
    lg2 = jnp.where(eio == i1, -jnp.inf, lg)
    m2 = jnp.max(lg2, axis=0, keepdims=True)
    i2 = jnp.min(jnp.where(lg2 == m2, eio, N_EXPERTS), axis=0, keepdims=True)
    e = jnp.exp(m2 - m1)
    w1 = 1.0 / (1.0 + e)
    w2 = e / (1.0 + e)

    sel1 = eio == i1
    sel2 = eio == i2
    onehot = jnp.logical_or(sel1, sel2)
    tr = lax.broadcasted_iota(jnp.int32, (tm, tm), 0)
    tc = lax.broadcasted_iota(jnp.int32, (tm, tm), 1)
    upper = (tr < tc).astype(BF16)
    before = _dot(onehot.astype(BF16), upper) + run_ref[:, 0:1]
    r1 = jnp.sum(jnp.where(sel1, before, 0.0), axis=0, keepdims=True)
    r2 = jnp.sum(jnp.where(sel2, before, 0.0), axis=0, keepdims=True)
    run_new = run_ref[:, 0:1] + jnp.sum(onehot.astype(F32), axis=1, keepdims=True)
    run_ref[...] = jnp.broadcast_to(run_new, run_ref.shape)
    cnt_ref[...] = jnp.broadcast_to(run_new, cnt_ref.shape).astype(jnp.int32)

    eidx_ref[...] = jnp.concatenate([i1, i2], axis=0)
    rank_ref[...] = jnp.concatenate([r1, r2], axis=0).astype(jnp.int32)
    w8 = jnp.concatenate([w1, w2, jnp.zeros((N_EXPERTS - 2, tm), F32)], axis=0)
    eye = (tr == tc).astype(F32)
    wcol_ref[...] = lax.dot_general(eye, w8, (((1,), (1,)), ((), ())),
                                    preferred_element_type=F32, precision=lax.Precision.HIGHEST)


def _router(h, a, b, wo, g, router_t):
    s = h.shape[0]
    tm = min(ROW_TILE, s)
    half = a.shape[1]
    return pl.pallas_call(
        _router_kernel,
        grid=(s // tm,),
        in_specs=[pl.BlockSpec((tm, D_MODEL), lambda i: (i, 0)),
                  pl.BlockSpec((tm, half), lambda i: (i, 0)),
                  pl.BlockSpec((tm, half), lambda i: (i, 0)),
                  pl.BlockSpec((2 * half, D_MODEL), lambda i: (0, 0)),
                  pl.BlockSpec((1, D_MODEL), lambda i: (0, 0)),
                  pl.BlockSpec((N_EXPERTS, D_MODEL), lambda i: (0, 0))],
        out_specs=[pl.BlockSpec((tm, D_MODEL), lambda i: (i, 0)),
                   pl.BlockSpec((tm, D_MODEL), lambda i: (i, 0)),
                   pl.BlockSpec((2, tm), lambda i: (0, i)),
                   pl.BlockSpec((2, tm), lambda i: (0, i)),
                   pl.BlockSpec((tm, N_EXPERTS), lambda i: (i, 0)),
                   pl.BlockSpec((N_EXPERTS, LANES), lambda i: (0, 0))],
        out_shape=[jax.ShapeDtypeStruct((s, D_MODEL), F32),
                   jax.ShapeDtypeStruct((s, D_MODEL), F32),
                   jax.ShapeDtypeStruct((2, s), jnp.int32),
                   jax.ShapeDtypeStruct((2, s), jnp.int32),
                   jax.ShapeDtypeStruct((s, N_EXPERTS), F32),
                   jax.ShapeDtypeStruct((N_EXPERTS, LANES), jnp.int32)],
        scratch_shapes=[pltpu.VMEM((N_EXPERTS, LANES), F32)],
        compiler_params=_cparams("arbitrary"),
        name="router",
    )(h, a, b, wo, g, router_t)


def _invert_kernel(pos_ref, src_ref):
    n_tok = pos_ref.shape[0] // 2

    def clear(i, carry):
        src_ref[i] = 0
        return carry

    lax.fori_loop(0, src_ref.shape[0], clear, 0, unroll=8)

    def put(t, carry):
        src_ref[pos_ref[t]] = t
        src_ref[pos_ref[n_tok + t]] = t
        return carry

    lax.fori_loop(0, n_tok, put, 0, unroll=8)


def _invert(pos_flat, n_slots):
    return pl.pallas_call(
        _invert_kernel,
        in_specs=[pl.BlockSpec(memory_space=pltpu.SMEM)],
        out_specs=pl.BlockSpec(memory_space=pltpu.SMEM),
        out_shape=jax.ShapeDtypeStruct((n_slots,), jnp.int32),
        name="invert",
    )(pos_flat)


def _combine_kernel(pos_ref, ys_hbm, h_ref, wcol_ref, fg_ref, o_ref, buf_ref, sem, *, final_norm):
    tm = h_ref.shape[0]

    for t in range(tm):
        for k in range(2):
            _row_copy(ys_hbm.at[pl.ds(pos_ref[k, t], 1)], buf_ref.at[k, pl.ds(t, 1)], sem).start()
    for k in range(2):
        _row_copy(ys_hbm.at[pl.ds(0, tm)], buf_ref.at[k], sem).wait()
    out = h_ref[...] + wcol_ref[:, 0:1] * buf_ref[0] + wcol_ref[:, 1:2] * buf_ref[1]
    if final_norm:
        out = _rms(out, fg_ref[...])
    o_ref[...] = out


def _combine(pos, ys, h, wcol, final_g, final_norm):
    s = h.shape[0]
    tm = min(ROW_TILE, s)
    return pl.pallas_call(
        functools.partial(_combine_kernel, final_norm=final_norm),
        grid=(s // tm,),
        in_specs=[pl.BlockSpec((2, tm), lambda i: (0, i), memory_space=pltpu.SMEM),
                  pl.BlockSpec(memory_space=pl.ANY),
                  pl.BlockSpec((tm, D_MODEL), lambda i: (i, 0)),
                  pl.BlockSpec((tm, N_EXPERTS), lambda i: (i, 0)),
                  pl.BlockSpec((1, D_MODEL), lambda i: (0, 0))],
        out_specs=pl.BlockSpec((tm, D_MODEL), lambda i: (i, 0)),
        out_shape=jax.ShapeDtypeStruct((s, D_MODEL), F32),
        scratch_shapes=[pltpu.VMEM((2, tm, D_MODEL), F32), pltpu.SemaphoreType.DMA(())],
        compiler_params=_cparams("arbitrary"),
        name="combine",
    )(pos, ys, h, wcol, final_g)


def _final_norm_kernel(h_ref, g_ref, o_ref):
    o_ref[...] = _rms(h_ref[...], g_ref[...])


def _final_norm(h, g):
    s = h.shape[0]
    tm = min(ROW_TILE, s)
    return pl.pallas_call(
        _final_norm_kernel,
        grid=(s // tm,),
        in_specs=[pl.BlockSpec((tm, D_MODEL), lambda i: (i, 0)), pl.BlockSpec((1, D_MODEL), lambda i: (0, 0))],
        out_specs=pl.BlockSpec((tm, D_MODEL), lambda i: (i, 0)),
        out_shape=jax.ShapeDtypeStruct((s, D_MODEL), F32),
        compiler_params=_cparams("parallel"),
        name="final_norm",
    )(h, g)


def _rope_lane_tables(seq, rot_dim, first_lane, period):
    half = rot_dim // 2
    pos = jnp.arange(seq, dtype=F32)
    inv = ROPE_THETA ** (-jnp.arange(0, rot_dim, 2, dtype=F32) / rot_dim)
    ang = pos[:, None] * inv[None, :]
    cos, sin = jnp.cos(ang), jnp.sin(ang)
    lane = jnp.arange(LANES) % period - first_lane
    lo = (lane >= 0) & (lane < half)
    hi = (lane >= half) & (lane < rot_dim)
    idx = jnp.clip(jnp.where(hi, lane - half, lane), 0, half - 1)
    cosf = jnp.where((lo | hi)[None, :], cos[:, idx], 1.0)
    sins = jnp.where(lo[None, :], -sin[:, idx], jnp.where(hi[None, :], sin[:, idx], 0.0))
    return cosf, sins


def _even_layer(h, l, p, i, rope_a):
    w_in = p["w_in_even"][i].astype(BF16)
    q, k, v, ob = _even_in(
        h, p["norm_mix_even"][i][None, :], w_in, rope_a[0], rope_a[1],
        p["sgu_ln_g"][i][None, :], p["sgu_ln_b"][i][None, :], p["sgu_w"][i], p["sgu_b"][i].T)
    lam_init = 0.8 - 0.6 * math.exp(-0.3 * l)
    oa, moe_w = _diff_attn(q, k, v, p["lam_q1"][i][None, :], p["lam_k1"][i][None, :],
                           p["lam_q2"][i][None, :], p["lam_k2"][i][None, :], p["subln_g"][i][:, None], lam_init,
                           (p["moe_wg"], p["moe_wu"], p["moe_wd"]), min(i, p["moe_wg"].shape[0] - 1))
    ffn = p["ffn_wg"].shape[2]
    h = _dense_ffn(h, oa, ob, p["w_out_even"][i].astype(BF16), p["norm_ffn_even"][i][None, :],
                   p["ffn_wg"][i].astype(BF16), p["ffn_wu"][i].astype(BF16), p["ffn_wd"][i].astype(BF16),
                   ffn // 2)
    return h, moe_w


def _odd_layer(h, p, i, moe_w, rope_d, final_g, final_norm):
    s = h.shape[0]
    w_in = p["w_in_odd"][i]
    n_main = C_WIDTH + D_Q_RANK + D_KV_RANK
    w_main = w_in[:, :n_main].astype(BF16)
    wkpe = jnp.zeros((D_MODEL, LANES), F32).at[:, D_NOPE:D_NOPE + D_ROPE].set(w_in[:, n_main:]).astype(BF16)
    wuq = jnp.pad(p["w_uq"][i].reshape(D_Q_RANK, D_HEADS, D_NOPE + D_ROPE),
                  ((0, 0), (0, 0), (0, D_HEAD_PAD - D_NOPE - D_ROPE))).reshape(D_Q_RANK, -1).astype(BF16)
    wukv = p["w_ukv"][i].reshape(D_KV_RANK, D_HEADS, D_NOPE + D_V)
    wk = jnp.pad(wukv[:, :, :D_NOPE], ((0, 0), (0, 0), (0, D_HEAD_PAD - D_NOPE))).reshape(D_KV_RANK, -1).astype(BF16)
    wv = wukv[:, :, D_NOPE:].reshape(D_KV_RANK, -1).astype(BF16)
    oc, q, k, v = _odd_in(
        h, p["norm_mix_odd"][i][None, :], w_main, wkpe, rope_d[0], rope_d[1],
        p["pool_w"][i].astype(BF16), p["pool_scale"][i][None, :],
        p["q_norm_g"][i][None, :], wuq, p["kv_norm_g"][i][None, :], wk, wv)
    od = _mla_attn(q, k, v)

    h, xn, eidx, rank, wcol, cnt = _router(h, oc, od, p["w_out_odd"][i].astype(BF16),
                                           p["norm_ffn_odd"][i][None, :], p["router"][i].T)
    counts = cnt[:, 0]
    padded = (counts + MOE_TILE - 1) // MOE_TILE * MOE_TILE
    ends = jnp.cumsum(padded)
    base = ends - padded
    pos = rank
    for e in range(N_EXPERTS):
        pos = pos + jnp.where(eidx == e, base[e], 0)
    n_tiles = (2 * s) // MOE_TILE + N_EXPERTS
    tile_start = jnp.arange(n_tiles, dtype=jnp.int32) * MOE_TILE
    tile_expert = jnp.minimum(jnp.sum(tile_start[:, None] >= ends[None, :], axis=1), N_EXPERTS - 1).astype(jnp.int32)
    n_used = (ends[-1:] // MOE_TILE).astype(jnp.int32)
    src = _invert(pos.reshape(-1), n_tiles * MOE_TILE)
    ys = _expert_ffn(tile_expert, n_used, src, xn, moe_w[0], moe_w[1], moe_w[2], MOE_F_TILE)
    return _combine(pos, ys, h, wcol, final_g, final_norm)


def kernel(x, norm_mix_even, w_in_even, lam_q1, lam_k1, lam_q2, lam_k2, subln_g, sgu_ln_g, sgu_ln_b, sgu_w, sgu_b, w_out_even, norm_ffn_even, ffn_wg, ffn_wu, ffn_wd, norm_mix_odd, w_in_odd, pool_w, pool_scale, q_norm_g, w_uq, kv_norm_g, w_ukv, w_out_odd, norm_ffn_odd, router, moe_wg, moe_wu, moe_wd, final_norm):
    p = dict(norm_mix_even=norm_mix_even, w_in_even=w_in_even, lam_q1=lam_q1, lam_k1=lam_k1, lam_q2=lam_q2,
             lam_k2=lam_k2, subln_g=subln_g, sgu_ln_g=sgu_ln_g, sgu_ln_b=sgu_ln_b, sgu_w=sgu_w, sgu_b=sgu_b,
             w_out_even=w_out_even, norm_ffn_even=norm_ffn_even, ffn_wg=ffn_wg, ffn_wu=ffn_wu, ffn_wd=ffn_wd,
             norm_mix_odd=norm_mix_odd, w_in_odd=w_in_odd, pool_w=pool_w, pool_scale=pool_scale,
             q_norm_g=q_norm_g, w_uq=w_uq, kv_norm_g=kv_norm_g, w_ukv=w_ukv, w_out_odd=w_out_odd,
             norm_ffn_odd=norm_ffn_odd, router=router, moe_wg=moe_wg, moe_wu=moe_wu, moe_wd=moe_wd)
    b, s, d = x.shape
    depth = norm_mix_even.shape[0] + norm_mix_odd.shape[0]
    rope_a = _rope_lane_tables(s, A_ROT, 0, A_QK_DIM)
    rope_d = _rope_lane_tables(s, D_ROPE, D_NOPE, LANES)
    final_g = final_norm[None, :]
    outs = []
    for bi in range(b):
        h = x[bi]
        for l in range(depth):
            if l % 2 == 0:
                h, moe_w = _even_layer(h, l, p, l // 2, rope_a)
            else:
                h = _odd_layer(h, p, l // 2, moe_w, rope_d, final_g, final_norm=(l == depth - 1))
        if depth % 2 == 1:
            h = _final_norm(h, final_g)
        outs.append(h)
    return jnp.stack(outs)
```

```python
import functools
import math

import jax
import jax.numpy as jnp
from jax import lax
from jax.experimental import pallas as pl
from jax.experimental.pallas import tpu as pltpu

F32 = jnp.float32
BF16 = jnp.bfloat16

ROPE_THETA = 500000.0
NORM_EPS = 1e-6
LN_EPS = 1e-5
LOG2E = 1.4426950408889634

D_MODEL = 1024
A_HEADS = 4
A_QK_DIM = 64
A_ROT = 16
A_WIDTH = 512
B_WIDTH = 512
B_GROUPS = 4
B_CHUNK = 128
C_WIDTH = 512
C_WINDOWS = (2, 4, 8, 16)
C_HALO = 16
D_HEADS = 8
D_NOPE = 64
D_ROPE = 32
D_V = 64
D_Q_RANK = 384
D_KV_RANK = 256
D_HEAD_PAD = 128
N_EXPERTS = 8

LANES = 128
VT_TAIL = 16
VT_ROWS = LANES + VT_TAIL
VT_ROWS_D = 2 * (D_V + VT_TAIL)
ROW_TILE = 512
ATTN_TILE = 512
MOE_TILE = 512
MOE_F_TILE = 1792
VMEM_LIMIT = 56 * 1024 * 1024


def _cparams(*sem):
    return pltpu.CompilerParams(dimension_semantics=sem, vmem_limit_bytes=VMEM_LIMIT)


def _rms(x, g):
    return x * lax.rsqrt(jnp.mean(x * x, axis=-1, keepdims=True) + NORM_EPS) * g


def _dot(a, b):
    return jnp.dot(a, b, preferred_element_type=F32)


def _rope_block(x, cosf, sins, shift, lo_mask):
    up = pltpu.roll(x, LANES - shift, 1)
    dn = pltpu.roll(x, shift, 1)
    return x * cosf + jnp.where(lo_mask, up, dn) * sins


def _rope_wide(x, cosf, sins, shift, lo_mask):
    nb = x.shape[1] // LANES
    return jnp.concatenate(
        [_rope_block(x[:, b * LANES:(b + 1) * LANES], cosf, sins, shift, lo_mask) for b in range(nb)],
        axis=1)


def _store_vt(vt_ref, v, group):
    rows = v.shape[0]
    vt = v.T
    r = lax.broadcasted_iota(jnp.int32, (VT_TAIL, rows), 0)
    tail = jnp.where(r == 0, 1.0, 0.0).astype(BF16)
    step = group + VT_TAIL
    for b in range(v.shape[1] // group):
        vt_ref[b * step:b * step + group, :] = vt[b * group:(b + 1) * group, :].astype(BF16)
        vt_ref[b * step + group:(b + 1) * step, :] = tail


def _even_in_kernel(h_ref, g_ref, w_ref, cos_ref, sin_ref, lng_ref, lnb_ref, sw_ref, sbt_ref,
                    q_ref, k_ref, v_ref, ob_ref):
    tm = h_ref.shape[0]
    xn = _rms(h_ref[...], g_ref[...]).astype(BF16)
    cosf = cos_ref[...]
    sins = sin_ref[...]
    lane = lax.broadcasted_iota(jnp.int32, (tm, LANES), 1)
    lo_mask = (lane % A_QK_DIM) < (A_ROT // 2)

    q = _dot(xn, w_ref[:, 0:A_WIDTH])
    q = _rope_wide(q, cosf, sins, A_ROT // 2, lo_mask) * (A_QK_DIM ** -0.5 * LOG2E)
    q_ref[...] = q.astype(BF16)
    k = _dot(xn, w_ref[:, A_WIDTH:2 * A_WIDTH])
    k_ref[...] = _rope_wide(k, cosf, sins, A_ROT // 2, lo_mask).astype(BF16)
    _store_vt(v_ref, _dot(xn, w_ref[:, 2 * A_WIDTH:3 * A_WIDTH]), LANES)

    ub = jax.nn.gelu(_dot(xn, w_ref[:, 3 * A_WIDTH:3 * A_WIDTH + B_WIDTH]))
    vb = jax.nn.gelu(_dot(xn, w_ref[:, 3 * A_WIDTH + B_WIDTH:3 * A_WIDTH + 2 * B_WIDTH]))
    mu = jnp.mean(vb, axis=-1, keepdims=True)
    var = jnp.mean(jnp.square(vb - mu), axis=-1, keepdims=True)
    vn = ((vb - mu) * lax.rsqrt(var + LN_EPS) * lng_ref[...] + lnb_ref[...]).astype(BF16)

    row = lax.broadcasted_iota(jnp.int32, (B_CHUNK, B_CHUNK), 0)
    col = lax.broadcasted_iota(jnp.int32, (B_CHUNK, B_CHUNK), 1)
    causal = col <= row
    gc = B_WIDTH // B_GROUPS
    for g in range(B_GROUPS):
        wg = jnp.where(causal, sw_ref[g], 0.0).astype(BF16)
        bias = sbt_ref[:, g:g + 1]
        for c in range(tm // B_CHUNK):
            rs = slice(c * B_CHUNK, (c + 1) * B_CHUNK)
            cs = slice(g * gc, (g + 1) * gc)
            mixed = _dot(wg, vn[rs, cs]) + bias
            ob_ref[rs, cs] = (ub[rs, cs] * mixed).astype(BF16)


def _even_in(h, g, w, cosf, sins, lng, lnb, sgu_w, sgu_bt):
    s = h.shape[0]
    tm = min(ROW_TILE, s)
    n_in = w.shape[1]
    row_spec = lambda width: pl.BlockSpec((tm, width), lambda i: (i, 0))
    full = lambda shape: pl.BlockSpec(shape, lambda i: (0,) * len(shape))
    out = jax.ShapeDtypeStruct((s, A_WIDTH), BF16)
    return pl.pallas_call(
        _even_in_kernel,
        grid=(s // tm,),
        in_specs=[row_spec(D_MODEL), full((1, D_MODEL)), full((D_MODEL, n_in)),
                  row_spec(LANES), row_spec(LANES), full((1, B_WIDTH)), full((1, B_WIDTH)),
                  full((B_GROUPS, B_CHUNK, B_CHUNK)), full((B_CHUNK, B_GROUPS))],
        out_specs=[row_spec(A_WIDTH), row_spec(A_WIDTH),
                   pl.BlockSpec((A_HEADS * VT_ROWS, tm), lambda i: (0, i)), row_spec(A_WIDTH)],
        out_shape=[out, out, jax.ShapeDtypeStruct((A_HEADS * VT_ROWS, s), BF16), out],
        compiler_params=_cparams("parallel"),
        name="even_in",
    )(h, g, w, cosf, sins, lng, lnb, sgu_w, sgu_bt)


def _flash_core(qi, t, score_fn, vt_ref, s_a, s_b, m_ref, acc_ref, per_half_values):
    m_ref[...] = jnp.full(m_ref.shape, -jnp.inf, F32)
    acc_ref[...] = jnp.zeros(acc_ref.shape, F32)
    n = acc_ref.shape[0]

    def process(s_ref, j, diagonal):
        s = s_ref[...]
        if diagonal:
            key = lax.broadcasted_iota(jnp.int32, s.shape, 0)
            qry = lax.broadcasted_iota(jnp.int32, s.shape, 1) % t
            s = jnp.where(key <= qry, s, -jnp.inf)
        m_prev = m_ref[...]
        m_new = jnp.maximum(m_prev, jnp.max(s, axis=0, keepdims=True))
        alpha = jnp.exp2(m_prev - m_new)
        p = jnp.exp2(s - m_new).astype(BF16)
        off = pl.multiple_of(j * t, t)
        if per_half_values:
            for half in range(2):
                cols = slice(half * t, (half + 1) * t)
                acc_ref[:, cols] = (acc_ref[:, cols] * alpha[:, cols]
                                    + _dot(vt_ref[half * n:(half + 1) * n, pl.ds(off, t)], p[:, cols]))
        else:
            acc_ref[...] = acc_ref[...] * alpha + _dot(vt_ref[:, pl.ds(off, t)], p)
        m_ref[...] = m_new

    score_fn(0, s_a)

    def pair(j):
        score_fn(j + 1, s_b)
        process(s_a, j, False)
        score_fn(j + 2, s_a)
        process(s_b, j + 1, False)

    def body4(it, carry):
        pair(4 * it)
        pair(4 * it + 2)
        return carry

    def body2(it, carry):
        pair(2 * it)
        return carry

    def body8(it, carry):
        for u in range(4):
            pair(8 * it + 2 * u)
        return carry

    lax.fori_loop(0, qi // 8, body8, 0)
    lax.fori_loop(qi // 8 * 2, qi // 4, body4, 0)
    lax.fori_loop(qi // 4 * 2, qi // 2, body2, 0)

    @pl.when(qi % 2 == 1)
    def _():
        score_fn(qi, s_b)
        process(s_a, qi - 1, False)
        process(s_b, qi, True)

    @pl.when(qi % 2 == 0)
    def _():
        process(s_a, qi, True)


def _attn_scratch(t, acc_rows):
    return [pltpu.VMEM((LANES, 2 * t), BF16), pltpu.VMEM((t, 2 * t), F32), pltpu.VMEM((t, 2 * t), F32),
            pltpu.VMEM((1, 2 * t), F32), pltpu.VMEM((acc_rows, 2 * t), F32)]


def _diff_attn_kernel(q_ref, k_ref, vt_ref, lq1_ref, lk1_ref, lq2_ref, lk2_ref, sg_ref, w0_ref, w1_ref, w2_ref,
                      o_ref, c0_ref, c1_ref, c2_ref, qt_ref, s_a, s_b, m_ref, acc_ref, *, lam_init):
    t = q_ref.shape[0]
    qi = pl.program_id(1)
    c0_ref[...] = w0_ref[...].astype(BF16)
    c1_ref[...] = w1_ref[...].astype(BF16)
    c2_ref[...] = w2_ref[...].astype(BF16)
    qt = q_ref[...].astype(F32).T
    row = lax.broadcasted_iota(jnp.int32, qt.shape, 0)
    qt_ref[:, 0:t] = jnp.where(row < A_QK_DIM, qt, 0.0).astype(BF16)
    qt_ref[:, t:2 * t] = jnp.where(row >= A_QK_DIM, qt, 0.0).astype(BF16)

    def score_fn(j, dst_ref):
        off = pl.multiple_of(j * t, t)
        dst_ref[...] = _dot(k_ref[pl.ds(off, t), :], qt_ref[...])

    _flash_core(qi, t, score_fn, vt_ref, s_a, s_b, m_ref, acc_ref, per_half_values=False)

    lam = (jnp.exp(jnp.sum(lq1_ref[...] * lk1_ref[...], axis=-1, keepdims=True))
           - jnp.exp(jnp.sum(lq2_ref[...] * lk2_ref[...], axis=-1, keepdims=True)) + lam_init)
    o = (acc_ref[0:LANES, 0:t] / acc_ref[LANES:LANES + 1, 0:t]
         - lam * (acc_ref[0:LANES, t:2 * t] / acc_ref[LANES:LANES + 1, t:2 * t]))
    o = o * lax.rsqrt(jnp.mean(o * o, axis=0, keepdims=True) + NORM_EPS) * sg_ref[...] * (1.0 - lam_init)
    o_ref[...] = o.T.astype(BF16)


def _diff_attn(q, k, vt, lq1, lk1, lq2, lk2, subln_g, lam_init, cast_ws, cast_layer):
    s = q.shape[0]
    t = min(ATTN_TILE, s)
    nq = s // t
    hw = 2 * A_QK_DIM
    small = lambda n: pl.BlockSpec((1, n), lambda h, i: (0, 0))
    steps = A_HEADS * nq
    flat = [w.reshape(w.shape[0], w.shape[1] * w.shape[2], w.shape[3]) for w in cast_ws]
    cast_in = [pl.BlockSpec((None, w.shape[1] // steps, w.shape[2]), lambda h, i: (cast_layer, h * nq + i, 0))
               for w in flat]
    cast_out = [pl.BlockSpec((w.shape[1] // steps, w.shape[2]), lambda h, i: (h * nq + i, 0)) for w in flat]
    outs = pl.pallas_call(
        functools.partial(_diff_attn_kernel, lam_init=lam_init),
        grid=(A_HEADS, nq),
        in_specs=[pl.BlockSpec((t, hw), lambda h, i: (i, h)),
                  pl.BlockSpec((s, hw), lambda h, i: (0, h)),
                  pl.BlockSpec((VT_ROWS, s), lambda h, i: (h, 0)),
                  small(A_QK_DIM), small(A_QK_DIM), small(A_QK_DIM), small(A_QK_DIM),
                  pl.BlockSpec((hw, 1), lambda h, i: (0, 0))] + cast_in,
        out_specs=[pl.BlockSpec((t, hw), lambda h, i: (i, h))] + cast_out,
        out_shape=[jax.ShapeDtypeStruct((s, A_WIDTH), BF16)]
        + [jax.ShapeDtypeStruct(w.shape[1:], BF16) for w in flat],
        scratch_shapes=_attn_scratch(t, VT_ROWS),
        compiler_params=_cparams("parallel", "parallel"),
        name="diff_attn",
    )(q, k, vt, lq1, lk1, lq2, lk2, subln_g, *flat)
    return outs[0], [c.reshape(w.shape[1:]) for c, w in zip(outs[1:], cast_ws)]


def _mla_attn_kernel(q_ref, k_ref, vt_ref, o_ref, qt_ref, s_a, s_b, m_ref, acc_ref):
    t = q_ref.shape[0]
    qi = pl.program_id(1)
    qt_ref[:, 0:t] = q_ref[:, 0:D_HEAD_PAD].astype(F32).T.astype(BF16)
    qt_ref[:, t:2 * t] = q_ref[:, D_HEAD_PAD:2 * D_HEAD_PAD].astype(F32).T.astype(BF16)

    def score_fn(j, dst_ref):
        off = pl.multiple_of(j * t, t)
        dst_ref[:, 0:t] = _dot(k_ref[pl.ds(off, t), 0:D_HEAD_PAD], qt_ref[:, 0:t])
        dst_ref[:, t:2 * t] = _dot(k_ref[pl.ds(off, t), D_HEAD_PAD:2 * D_HEAD_PAD], qt_ref[:, t:2 * t])

    _flash_core(qi, t, score_fn, vt_ref, s_a, s_b, m_ref, acc_ref, per_half_values=True)

    oa = acc_ref[0:D_V, 0:t] / acc_ref[D_V:D_V + 1, 0:t]
    ob = acc_ref[0:D_V, t:2 * t] / acc_ref[D_V:D_V + 1, t:2 * t]
    o_ref[...] = jnp.concatenate([oa, ob], axis=0).T.astype(BF16)


def _mla_attn(q, k, vt):
    s = q.shape[0]
    t = min(ATTN_TILE, s)
    return pl.pallas_call(
        _mla_attn_kernel,
        grid=(D_HEADS // 2, s // t),
        in_specs=[pl.BlockSpec((t, 2 * D_HEAD_PAD), lambda h, i: (i, h)),
                  pl.BlockSpec((s, 2 * D_HEAD_PAD), lambda h, i: (0, h)),
                  pl.BlockSpec((VT_ROWS_D, s), lambda h, i: (h, 0))],
        out_specs=pl.BlockSpec((t, 2 * D_V), lambda h, i: (i, h)),
        out_shape=jax.ShapeDtypeStruct((s, D_HEADS * D_V), BF16),
        scratch_shapes=_attn_scratch(t, VT_ROWS_D // 2),
        compiler_params=_cparams("parallel", "parallel"),
        name="mla_attn",
    )(q, k, vt)


def _mixer_out(h_ref, a_ref, b_ref, wo_ref):
    half = a_ref.shape[1]
    return h_ref[...] + _dot(a_ref[...], wo_ref[0:half, :]) + _dot(b_ref[...], wo_ref[half:2 * half, :])


def _dense_ffn_kernel(h_ref, a_ref, b_ref, wo_ref, g_ref, wg_ref, wu_ref, wd_ref, o_ref, xn_ref):
    f = pl.program_id(1)

    @pl.when(f == 0)
    def _():
        x = _mixer_out(h_ref, a_ref, b_ref, wo_ref)
        xn_ref[...] = _rms(x, g_ref[...]).astype(BF16)
        o_ref[...] = x

    xn = xn_ref[...]
    a = jax.nn.silu(_dot(xn, wg_ref[...])) * _dot(xn, wu_ref[...])
    o_ref[...] += _dot(a.astype(BF16), wd_ref[...])


def _dense_ffn(h, a, b, wo, g, wg, wu, wd, f_tile):
    s = h.shape[0]
    tm = min(ROW_TILE, s)
    ffn = wg.shape[1]
    half = a.shape[1]
    return pl.pallas_call(
        _dense_ffn_kernel,
        grid=(s // tm, ffn // f_tile),
        in_specs=[pl.BlockSpec((tm, D_MODEL), lambda i, f: (i, 0)),
                  pl.BlockSpec((tm, half), lambda i, f: (i, 0)),
                  pl.BlockSpec((tm, half), lambda i, f: (i, 0)),
                  pl.BlockSpec((2 * half, D_MODEL), lambda i, f: (0, 0)),
                  pl.BlockSpec((1, D_MODEL), lambda i, f: (0, 0)),
                  pl.BlockSpec((D_MODEL, f_tile), lambda i, f: (0, f)),
                  pl.BlockSpec((D_MODEL, f_tile), lambda i, f: (0, f)),
                  pl.BlockSpec((f_tile, D_MODEL), lambda i, f: (f, 0))],
        out_specs=pl.BlockSpec((tm, D_MODEL), lambda i, f: (i, 0)),
        out_shape=jax.ShapeDtypeStruct((s, D_MODEL), F32),
        scratch_shapes=[pltpu.VMEM((tm, D_MODEL), BF16)],
        compiler_params=_cparams("parallel", "arbitrary"),
        name="dense_ffn",
    )(h, a, b, wo, g, wg, wu, wd)


def _row_copy(src, dst, sem):
    return pltpu.make_async_copy(src, dst, sem)


def _expert_ffn_kernel(te_ref, nu_ref, src_ref, x_hbm, wg_ref, wu_ref, wd_ref, o_ref, xbuf_ref, xb_ref, sem,
                       *, n_f):
    r = pl.program_id(0)
    f = pl.program_id(1)
    tm = o_ref.shape[0]
    n_used = nu_ref[0]

    def tile_start(tile, slot):
        for t in range(tm):
            _row_copy(x_hbm.at[pl.ds(src_ref[tile * tm + t], 1)], xbuf_ref.at[slot, pl.ds(t, 1)],
                      sem.at[slot]).start()

    def tile_wait(slot):
        _row_copy(x_hbm.at[pl.ds(0, tm)], xbuf_ref.at[slot], sem.at[slot]).wait()

    @pl.when(r < n_used)
    def _():
        @pl.when(jnp.logical_and(r == 0, f == 0))
        def _():
            def first(t, carry):
                _row_copy(x_hbm.at[pl.ds(src_ref[t], 1)], xbuf_ref.at[0, pl.ds(t, 1)], sem.at[0]).start()
                return carry

            lax.fori_loop(0, tm, first, 0)

        for slot in range(2):
            @pl.when(jnp.logical_and(f == 0, r % 2 == slot))
            def _():
                tile_wait(slot)
                xb_ref[...] = xbuf_ref[slot].astype(BF16)
                o_ref[...] = jnp.zeros(o_ref.shape, F32)
                tile_start(jnp.minimum(r + 1, pl.num_programs(0) - 1), 1 - slot)

        xb = xb_ref[...]
        a = jax.nn.silu(_dot(xb, wg_ref[...])) * _dot(xb, wu_ref[...])
        o_ref[...] += _dot(a.astype(BF16), wd_ref[...])

        for slot in range(2):
            @pl.when(jnp.logical_and(jnp.logical_and(r == n_used - 1, f == n_f - 1), r % 2 == slot))
            def _():
                tile_wait(1 - slot)

    @pl.when(jnp.logical_and(r >= n_used, f == 0))
    def _():
        o_ref[...] = jnp.zeros(o_ref.shape, F32)


def _expert_ffn(tile_expert, n_used, src, xn, wg, wu, wd, f_tile):
    rows = src.shape[0]
    tm = MOE_TILE
    ffn = wg.shape[2]
    grid_spec = pltpu.PrefetchScalarGridSpec(
        num_scalar_prefetch=3,
        grid=(rows // tm, ffn // f_tile),
        in_specs=[pl.BlockSpec(memory_space=pl.ANY),
                  pl.BlockSpec((None, D_MODEL, f_tile),
                               lambda r, f, te, nu, sr: (te[r], 0, jnp.where(r < nu[0], f, 0))),
                  pl.BlockSpec((None, D_MODEL, f_tile),
                               lambda r, f, te, nu, sr: (te[r], 0, jnp.where(r < nu[0], f, 0))),
                  pl.BlockSpec((None, f_tile, D_MODEL),
                               lambda r, f, te, nu, sr: (te[r], jnp.where(r < nu[0], f, 0), 0))],
        out_specs=pl.BlockSpec((tm, D_MODEL), lambda r, f, te, nu, sr: (r, 0)),
        scratch_shapes=[pltpu.VMEM((2, tm, D_MODEL), F32), pltpu.VMEM((tm, D_MODEL), BF16),
                        pltpu.SemaphoreType.DMA((2,))],
    )
    return pl.pallas_call(
        functools.partial(_expert_ffn_kernel, n_f=ffn // f_tile),
        grid_spec=grid_spec,
        out_shape=jax.ShapeDtypeStruct((rows, D_MODEL), F32),
        compiler_params=_cparams("arbitrary", "arbitrary"),
        name="expert_ffn",
    )(tile_expert, n_used, src, xn, wg, wu, wd)


def _odd_in_kernel(h_ref, g_ref, w_ref, wkpe_ref, cos_ref, sin_ref, pw_ref, ps_ref,
                   qg_ref, wuq_ref, kvg_ref, wk_ref, wv_ref,
                   oc_ref, q_ref, k_ref, v_ref, halo_ref):
    tm = h_ref.shape[0]
    i = pl.program_id(0)
    xn = _rms(h_ref[...], g_ref[...]).astype(BF16)
    cosf = cos_ref[...]
    sins = sin_ref[...]
    lane = lax.broadcasted_iota(jnp.int32, (tm, LANES), 1)
    lo_mask = lane < (D_NOPE + D_ROPE // 2)

    @pl.when(i == 0)
    def _():
        halo_ref[...] = jnp.zeros(halo_ref.shape, F32)

    xc = _dot(xn, w_ref[:, 0:C_WIDTH])
    ext = jnp.concatenate([halo_ref[...], xc], axis=0)
    halo_ref[...] = xc[tm - C_HALO:tm, :]
    pos = i * tm + lax.broadcasted_iota(jnp.int32, (tm, 1), 0)
    gc = C_WIDTH // len(C_WINDOWS)
    for g, win in enumerate(C_WINDOWS):
        cs = slice(g * gc, (g + 1) * gc)
        a = ext[:, cs]
        span = 1
        while span < win:
            n = a.shape[0] - span
            a = a[span:span + n, :] + a[0:n, :]
            span *= 2
        start = C_HALO - (win - 1)
        wsum = a[start:start + tm, :]
        cnt = jnp.minimum(pos + 1, win).astype(F32)
        pooled = (wsum / cnt - xc[:, cs]).astype(BF16)
        oc_ref[:, cs] = (_dot(pooled, pw_ref[g]) * ps_ref[:, cs]).astype(BF16)

    cq = _dot(xn, w_ref[:, C_WIDTH:C_WIDTH + D_Q_RANK])
    q = _dot(_rms(cq, qg_ref[...]).astype(BF16), wuq_ref[...])
    q = _rope_wide(q, cosf, sins, D_ROPE // 2, lo_mask) * ((D_NOPE + D_ROPE) ** -0.5 * LOG2E)
    q_ref[...] = q.astype(BF16)

    ckv = _dot(xn, w_ref[:, C_WIDTH + D_Q_RANK:C_WIDTH + D_Q_RANK + D_KV_RANK])
    ckvn = _rms(ckv, kvg_ref[...]).astype(BF16)
    kpe = _rope_block(_dot(xn, wkpe_ref[...]), cosf, sins, D_ROPE // 2, lo_mask)
    kn = _dot(ckvn, wk_ref[...])
    k_ref[...] = jnp.concatenate(
        [kn[:, b * LANES:(b + 1) * LANES] + kpe for b in range(D_HEADS)], axis=1).astype(BF16)
    _store_vt(v_ref, _dot(ckvn, wv_ref[...]), D_V)


def _odd_in(h, g, w, wkpe, cosf, sins, pool_w, pool_scale, qg, wuq, kvg, wk, wv):
    s = h.shape[0]
    tm = min(ROW_TILE, s)
    row_spec = lambda width: pl.BlockSpec((tm, width), lambda i: (i, 0))
    full = lambda shape: pl.BlockSpec(shape, lambda i: (0,) * len(shape))
    return pl.pallas_call(
        _odd_in_kernel,
        grid=(s // tm,),
        in_specs=[row_spec(D_MODEL), full((1, D_MODEL)), full(w.shape), full(wkpe.shape),
                  row_spec(LANES), row_spec(LANES), full(pool_w.shape), full((1, C_WIDTH)),
                  full((1, D_Q_RANK)), full(wuq.shape), full((1, D_KV_RANK)), full(wk.shape), full(wv.shape)],
        out_specs=[row_spec(C_WIDTH), row_spec(D_HEADS * D_HEAD_PAD), row_spec(D_HEADS * D_HEAD_PAD),
                   pl.BlockSpec((D_HEADS // 2 * VT_ROWS_D, tm), lambda i: (0, i))],
        out_shape=[jax.ShapeDtypeStruct((s, C_WIDTH), BF16),
                   jax.ShapeDtypeStruct((s, D_HEADS * D_HEAD_PAD), BF16),
                   jax.ShapeDtypeStruct((s, D_HEADS * D_HEAD_PAD), BF16),
                   jax.ShapeDtypeStruct((D_HEADS // 2 * VT_ROWS_D, s), BF16)],
        scratch_shapes=[pltpu.VMEM((C_HALO, C_WIDTH), F32)],
        compiler_params=_cparams("arbitrary"),
        name="odd_in",
    )(h, g, w, wkpe, cosf, sins, pool_w, pool_scale, qg, wuq, kvg, wk, wv)


def _router_kernel(h_ref, a_ref, b_ref, wo_ref, g_ref, rt_ref,
                   h2_ref, xn_ref, eidx_ref, rank_ref, wcol_ref, cnt_ref, run_ref):
    tm = h_ref.shape[0]
    i = pl.program_id(0)

    @pl.when(i == 0)
    def _():
        run_ref[...] = jnp.zeros(run_ref.shape, F32)

    x = _mixer_out(h_ref, a_ref, b_ref, wo_ref)
    h2_ref[...] = x
    xn = _rms(x, g_ref[...])
    xn_ref[...] = xn
    lg = lax.dot_general(rt_ref[...], xn, (((1,), (1,)), ((), ())),
                         preferred_element_type=F32, precision=lax.Precision.HIGHEST)
    eio = lax.broadcasted_iota(jnp.int32, lg.shape, 0)
    m1 = jnp.max(lg, axis=0, keepdims=True)
    i1 = jnp.min(jnp.where(lg == m1, eio, N_EXPERTS), axis=0, keepdims=True)
    lg2 = jnp.where(eio == i1, -jnp.inf, lg)
    m2 = jnp.max(lg2, axis=0, keepdims=True)
    i2 = jnp.min(jnp.where(lg2 == m2, eio, N_EXPERTS), axis=0, keepdims=True)
    e = jnp.exp(m2 - m1)
    w1 = 1.0 / (1.0 + e)
    w2 = e / (1.0 + e)

    sel1 = eio == i1
    sel2 = eio == i2
    onehot = jnp.logical_or(sel1, sel2)
    tr = lax.broadcasted_iota(jnp.int32, (tm, tm), 0)
    tc = lax.broadcasted_iota(jnp.int32, (tm, tm), 1)
    upper = (tr < tc).astype(BF16)
    before = _dot(onehot.astype(BF16), upper) + run_ref[:, 0:1]
    r1 = jnp.sum(jnp.where(sel1, before, 0.0), axis=0, keepdims=True)
    r2 = jnp.sum(jnp.where(sel2, before, 0.0), axis=0, keepdims=True)
    run_new = run_ref[:, 0:1] + jnp.sum(onehot.astype(F32), axis=1, keepdims=True)
    run_ref[...] = jnp.broadcast_to(run_new, run_ref.shape)
    cnt_ref[...] = jnp.broadcast_to(run_new, cnt_ref.shape).astype(jnp.int32)

    eidx_ref[...] = jnp.concatenate([i1, i2], axis=0)
    rank_ref[...] = jnp.concatenate([r1, r2], axis=0).astype(jnp.int32)
    w8 = jnp.concatenate([w1, w2, jnp.zeros((N_EXPERTS - 2, tm), F32)], axis=0)
    wcol_ref[...] = w8.T


def _router(h, a, b, wo, g, router_t):
    s = h.shape[0]
    tm = min(ROW_TILE, s)
    half = a.shape[1]
    return pl.pallas_call(
        _router_kernel,
        grid=(s // tm,),
        in_specs=[pl.BlockSpec((tm, D_MODEL), lambda i: (i, 0)),
                  pl.BlockSpec((tm, half), lambda i: (i, 0)),
                  pl.BlockSpec((tm, half), lambda i: (i, 0)),
                  pl.BlockSpec((2 * half, D_MODEL), lambda i: (0, 0)),
                  pl.BlockSpec((1, D_MODEL), lambda i: (0, 0)),
                  pl.BlockSpec((N_EXPERTS, D_MODEL), lambda i: (0, 0))],
        out_specs=[pl.BlockSpec((tm, D_MODEL), lambda i: (i, 0)),
                   pl.BlockSpec((tm, D_MODEL), lambda i: (i, 0)),
                   pl.BlockSpec((2, tm), lambda i: (0, i)),
                   pl.BlockSpec((2, tm), lambda i: (0, i)),
                   pl.BlockSpec((tm, N_EXPERTS), lambda i: (i, 0)),
                   pl.BlockSpec((N_EXPERTS, LANES), lambda i: (0, 0))],
        out_shape=[jax.ShapeDtypeStruct((s, D_MODEL), F32),
                   jax.ShapeDtypeStruct((s, D_MODEL), F32),
                   jax.ShapeDtypeStruct((2, s), jnp.int32),
                   jax.ShapeDtypeStruct((2, s), jnp.int32),
                   jax.ShapeDtypeStruct((s, N_EXPERTS), F32),
                   jax.ShapeDtypeStruct((N_EXPERTS, LANES), jnp.int32)],
        scratch_shapes=[pltpu.VMEM((N_EXPERTS, LANES), F32)],
        compiler_params=_cparams("arbitrary"),
        name="router",
    )(h, a, b, wo, g, router_t)


def _invert_kernel(gap_ref, pos_ref, src_ref):
    n_tok = pos_ref.shape[0] // 2

    def clear(i, carry):
        src_ref[i] = 0
        return carry

    for g in range(gap_ref.shape[0] // 2):
        lax.fori_loop(gap_ref[2 * g], gap_ref[2 * g + 1], clear, 0)

    def put(t, carry):
        src_ref[pos_ref[t]] = t
        src_ref[pos_ref[n_tok + t]] = t
        return carry

    lax.fori_loop(0, n_tok, put, 0, unroll=8)


def _invert(gaps, pos_flat, n_slots):
    return pl.pallas_call(
        _invert_kernel,
        in_specs=[pl.BlockSpec(memory_space=pltpu.SMEM), pl.BlockSpec(memory_space=pltpu.SMEM)],
        out_specs=pl.BlockSpec(memory_space=pltpu.SMEM),
        out_shape=jax.ShapeDtypeStruct((n_slots,), jnp.int32),
        name="invert",
    )(gaps, pos_flat)


def _combine_kernel(pos_ref, ys_hbm, h_ref, wcol_ref, fg_ref, o_ref, buf_ref, sem, *, final_norm):
    tm = h_ref.shape[0]

    for t in range(tm):
        for k in range(2):
            _row_copy(ys_hbm.at[pl.ds(pos_ref[k, t], 1)], buf_ref.at[k, pl.ds(t, 1)], sem).start()
    for k in range(2):
        _row_copy(ys_hbm.at[pl.ds(0, tm)], buf_ref.at[k], sem).wait()
    out = h_ref[...] + wcol_ref[:, 0:1] * buf_ref[0] + wcol_ref[:, 1:2] * buf_ref[1]
    if final_norm:
        out = _rms(out, fg_ref[...])
    o_ref[...] = out


def _combine(pos, ys, h, wcol, final_g, final_norm):
    s = h.shape[0]
    tm = min(ROW_TILE, s)
    return pl.pallas_call(
        functools.partial(_combine_kernel, final_norm=final_norm),
        grid=(s // tm,),
        in_specs=[pl.BlockSpec((2, tm), lambda i: (0, i), memory_space=pltpu.SMEM),
                  pl.BlockSpec(memory_space=pl.ANY),
                  pl.BlockSpec((tm, D_MODEL), lambda i: (i, 0)),
                  pl.BlockSpec((tm, N_EXPERTS), lambda i: (i, 0)),
                  pl.BlockSpec((1, D_MODEL), lambda i: (0, 0))],
        out_specs=pl.BlockSpec((tm, D_MODEL), lambda i: (i, 0)),
        out_shape=jax.ShapeDtypeStruct((s, D_MODEL), F32),
        scratch_shapes=[pltpu.VMEM((2, tm, D_MODEL), F32), pltpu.SemaphoreType.DMA(())],
        compiler_params=_cparams("arbitrary"),
        name="combine",
    )(pos, ys, h, wcol, final_g)


def _final_norm_kernel(h_ref, g_ref, o_ref):
    o_ref[...] = _rms(h_ref[...], g_ref[...])


def _final_norm(h, g):
    s = h.shape[0]
    tm = min(ROW_TILE, s)
    return pl.pallas_call(
        _final_norm_kernel,
        grid=(s // tm,),
        in_specs=[pl.BlockSpec((tm, D_MODEL), lambda i: (i, 0)), pl.BlockSpec((1, D_MODEL), lambda i: (0, 0))],
        out_specs=pl.BlockSpec((tm, D_MODEL), lambda i: (i, 0)),
        out_shape=jax.ShapeDtypeStruct((s, D_MODEL), F32),
        compiler_params=_cparams("parallel"),
        name="final_norm",
    )(h, g)


def _rope_lane_tables(seq, rot_dim, first_lane, period):
    half = rot_dim // 2
    pos = jnp.arange(seq, dtype=F32)
    inv = ROPE_THETA ** (-jnp.arange(0, rot_dim, 2, dtype=F32) / rot_dim)
    ang = pos[:, None] * inv[None, :]
    cos, sin = jnp.cos(ang), jnp.sin(ang)
    lane = jnp.arange(LANES) % period - first_lane
    lo = (lane >= 0) & (lane < half)
    hi = (lane >= half) & (lane < rot_dim)
    idx = jnp.clip(jnp.where(hi, lane - half, lane), 0, half - 1)
    cosf = jnp.where((lo | hi)[None, :], cos[:, idx], 1.0)
    sins = jnp.where(lo[None, :], -sin[:, idx], jnp.where(hi[None, :], sin[:, idx], 0.0))
    return cosf, sins


def _even_layer(h, l, p, i, rope_a):
    w_in = p["w_in_even"][i].astype(BF16)
    q, k, v, ob = _even_in(
        h, p["norm_mix_even"][i][None, :], w_in, rope_a[0], rope_a[1],
        p["sgu_ln_g"][i][None, :], p["sgu_ln_b"][i][None, :], p["sgu_w"][i], p["sgu_b"][i].T)
    lam_init = 0.8 - 0.6 * math.exp(-0.3 * l)
    oa, moe_w = _diff_attn(q, k, v, p["lam_q1"][i][None, :], p["lam_k1"][i][None, :],
                           p["lam_q2"][i][None, :], p["lam_k2"][i][None, :], p["subln_g"][i][:, None], lam_init,
                           (p["moe_wg"], p["moe_wu"], p["moe_wd"]), min(i, p["moe_wg"].shape[0] - 1))
    ffn = p["ffn_wg"].shape[2]
    h = _dense_ffn(h, oa, ob, p["w_out_even"][i].astype(BF16), p["norm_ffn_even"][i][None, :],
                   p["ffn_wg"][i].astype(BF16), p["ffn_wu"][i].astype(BF16), p["ffn_wd"][i].astype(BF16),
                   ffn // 2)
    return h, moe_w


def _odd_layer(h, p, i, moe_w, rope_d, final_g, final_norm):
    s = h.shape[0]
    w_in = p["w_in_odd"][i]
    n_main = C_WIDTH + D_Q_RANK + D_KV_RANK
    w_main = w_in[:, :n_main].astype(BF16)
    wkpe = jnp.zeros((D_MODEL, LANES), F32).at[:, D_NOPE:D_NOPE + D_ROPE].set(w_in[:, n_main:]).astype(BF16)
    wuq = jnp.pad(p["w_uq"][i].reshape(D_Q_RANK, D_HEADS, D_NOPE + D_ROPE),
                  ((0, 0), (0, 0), (0, D_HEAD_PAD - D_NOPE - D_ROPE))).reshape(D_Q_RANK, -1).astype(BF16)
    wukv = p["w_ukv"][i].reshape(D_KV_RANK, D_HEADS, D_NOPE + D_V)
    wk = jnp.pad(wukv[:, :, :D_NOPE], ((0, 0), (0, 0), (0, D_HEAD_PAD - D_NOPE))).reshape(D_KV_RANK, -1).astype(BF16)
    wv = wukv[:, :, D_NOPE:].reshape(D_KV_RANK, -1).astype(BF16)
    oc, q, k, v = _odd_in(
        h, p["norm_mix_odd"][i][None, :], w_main, wkpe, rope_d[0], rope_d[1],
        p["pool_w"][i].astype(BF16), p["pool_scale"][i][None, :],
        p["q_norm_g"][i][None, :], wuq, p["kv_norm_g"][i][None, :], wk, wv)
    od = _mla_attn(q, k, v)

    h, xn, eidx, rank, wcol, cnt = _router(h, oc, od, p["w_out_odd"][i].astype(BF16),
                                           p["norm_ffn_odd"][i][None, :], p["router"][i].T)
    counts = cnt[:, 0]
    padded = (counts + MOE_TILE - 1) // MOE_TILE * MOE_TILE
    ends = jnp.cumsum(padded)
    base = ends - padded
    pos = rank
    for e in range(N_EXPERTS):
        pos = pos + jnp.where(eidx == e, base[e], 0)
    n_tiles = (2 * s) // MOE_TILE + N_EXPERTS
    tile_start = jnp.arange(n_tiles, dtype=jnp.int32) * MOE_TILE
    tile_expert = jnp.minimum(jnp.sum(tile_start[:, None] >= ends[None, :], axis=1), N_EXPERTS - 1).astype(jnp.int32)
    n_used = (ends[-1:] // MOE_TILE).astype(jnp.int32)
    n_slots = n_tiles * MOE_TILE
    gaps = jnp.stack([jnp.append(base + counts, ends[-1]), jnp.append(ends, n_slots)], axis=1)
    src = _invert(gaps.reshape(-1).astype(jnp.int32), pos.reshape(-1), n_slots)
    ys = _expert_ffn(tile_expert, n_used, src, xn, moe_w[0], moe_w[1], moe_w[2], MOE_F_TILE)
    return _combine(pos, ys, h, wcol, final_g, final_norm)


def kernel(x, norm_mix_even, w_in_even, lam_q1, lam_k1, lam_q2, lam_k2, subln_g, sgu_ln_g, sgu_ln_b, sgu_w, sgu_b, w_out_even, norm_ffn_even, ffn_wg, ffn_wu, ffn_wd, norm_mix_odd, w_in_odd, pool_w, pool_scale, q_norm_g, w_uq, kv_norm_g, w_ukv, w_out_odd, norm_ffn_odd, router, moe_wg, moe_wu, moe_wd, final_norm):
    p = dict(norm_mix_even=norm_mix_even, w_in_even=w_in_even, lam_q1=lam_q1, lam_k1=lam_k1, lam_q2=lam_q2,
             lam_k2=lam_k2, subln_g=subln_g, sgu_ln_g=sgu_ln_g, sgu_ln_b=sgu_ln_b, sgu_w=sgu_w, sgu_b=sgu_b,
             w_out_even=w_out_even, norm_ffn_even=norm_ffn_even, ffn_wg=ffn_wg, ffn_wu=ffn_wu, ffn_wd=ffn_wd,
             norm_mix_odd=norm_mix_odd, w_in_odd=w_in_odd, pool_w=pool_w, pool_scale=pool_scale,
             q_norm_g=q_norm_g, w_uq=w_uq, kv_norm_g=kv_norm_g, w_ukv=w_ukv, w_out_odd=w_out_odd,
             norm_ffn_odd=norm_ffn_odd, router=router, moe_wg=moe_wg, moe_wu=moe_wu, moe_wd=moe_wd)
    b, s, d = x.shape
    depth = norm_mix_even.shape[0] + norm_mix_odd.shape[0]
    rope_a = _rope_lane_tables(s, A_ROT, 0, A_QK_DIM)
    rope_d = _rope_lane_tables(s, D_ROPE, D_NOPE, LANES)
    final_g = final_norm[None, :]
    outs = []
    for bi in range(b):
        h = x[bi]
        for l in range(depth):
            if l % 2 == 0:
                h, moe_w = _even_layer(h, l, p, l // 2, rope_a)
            else:
                h = _odd_layer(h, p, l // 2, moe_w, rope_d, final_g, final_norm=(l == depth - 1))
        if depth % 2 == 1:
            h = _final_norm(h, final_g)
        outs.append(h)
    return jnp.stack(outs)
```

```python
import functools
import math

import jax
import jax.numpy as jnp
from jax import lax
from jax.experimental import pallas as pl
from jax.experimental.pallas import tpu as pltpu

F32 = jnp.float32
BF16 = jnp.bfloat16

ROPE_THETA = 500000.0
NORM_EPS = 1e-6
LN_EPS = 1e-5
LOG2E = 1.4426950408889634

D_MODEL = 1024
A_HEADS = 4
A_QK_DIM = 64
A_ROT = 16
A_WIDTH = 512
B_WIDTH = 512
B_GROUPS = 4
B_CHUNK = 128
C_WIDTH = 512
C_WINDOWS = (2, 4, 8, 16)
C_HALO = 16
D_HEADS = 8
D_NOPE = 64
D_ROPE = 32
D_V = 64
D_Q_RANK = 384
D_KV_RANK = 256
D_HEAD_PAD = 128
N_EXPERTS = 8

LANES = 128
VT_TAIL = 16
VT_ROWS = LANES + VT_TAIL
VT_ROWS_D = 2 * (D_V + VT_TAIL)
ROW_TILE = 512
ATTN_TILE = 512
MOE_TILE = 512
MOE_F_TILE = 1792
VMEM_LIMIT = 56 * 1024 * 1024


def _cparams(*sem):
    return pltpu.CompilerParams(dimension_semantics=sem, vmem_limit_bytes=VMEM_LIMIT)


def _rms(x, g):
    return x * lax.rsqrt(jnp.mean(x * x, axis=-1, keepdims=True) + NORM_EPS) * g


def _dot(a, b):
    return jnp.dot(a, b, preferred_element_type=F32)


def _rope_block(x, cosf, sins, shift, lo_mask):
    up = pltpu.roll(x, LANES - shift, 1)
    dn = pltpu.roll(x, shift, 1)
    return x * cosf + jnp.where(lo_mask, up, dn) * sins


def _rope_wide(x, cosf, sins, shift, lo_mask):
    nb = x.shape[1] // LANES
    return jnp.concatenate(
        [_rope_block(x[:, b * LANES:(b + 1) * LANES], cosf, sins, shift, lo_mask) for b in range(nb)],
        axis=1)


def _store_vt(vt_ref, v, group):
    rows = v.shape[0]
    vt = v.T
    r = lax.broadcasted_iota(jnp.int32, (VT_TAIL, rows), 0)
    tail = jnp.where(r == 0, 1.0, 0.0).astype(BF16)
    step = group + VT_TAIL
    for b in range(v.shape[1] // group):
        vt_ref[b * step:b * step + group, :] = vt[b * group:(b + 1) * group, :].astype(BF16)
        vt_ref[b * step + group:(b + 1) * step, :] = tail


def _even_in_kernel(h_ref, g_ref, w_ref, cos_ref, sin_ref, lng_ref, lnb_ref, sw_ref, sbt_ref,
                    q_ref, k_ref, v_ref, ob_ref):
    tm = h_ref.shape[0]
    xn = _rms(h_ref[...], g_ref[...]).astype(BF16)
    cosf = cos_ref[...]
    sins = sin_ref[...]
    lane = lax.broadcasted_iota(jnp.int32, (tm, LANES), 1)
    lo_mask = (lane % A_QK_DIM) < (A_ROT // 2)

    q = _dot(xn, w_ref[:, 0:A_WIDTH])
    q = _rope_wide(q, cosf, sins, A_ROT // 2, lo_mask) * (A_QK_DIM ** -0.5 * LOG2E)
    q_ref[...] = q.astype(BF16)
    k = _dot(xn, w_ref[:, A_WIDTH:2 * A_WIDTH])
    k_ref[...] = _rope_wide(k, cosf, sins, A_ROT // 2, lo_mask).astype(BF16)
    _store_vt(v_ref, _dot(xn, w_ref[:, 2 * A_WIDTH:3 * A_WIDTH]), LANES)

    ub = jax.nn.gelu(_dot(xn, w_ref[:, 3 * A_WIDTH:3 * A_WIDTH + B_WIDTH]))
    vb = jax.nn.gelu(_dot(xn, w_ref[:, 3 * A_WIDTH + B_WIDTH:3 * A_WIDTH + 2 * B_WIDTH]))
    mu = jnp.mean(vb, axis=-1, keepdims=True)
    var = jnp.mean(jnp.square(vb - mu), axis=-1, keepdims=True)
    vn = ((vb - mu) * lax.rsqrt(var + LN_EPS) * lng_ref[...] + lnb_ref[...]).astype(BF16)

    row = lax.broadcasted_iota(jnp.int32, (B_CHUNK, B_CHUNK), 0)
    col = lax.broadcasted_iota(jnp.int32, (B_CHUNK, B_CHUNK), 1)
    causal = col <= row
    gc = B_WIDTH // B_GROUPS
    for g in range(B_GROUPS):
        wg = jnp.where(causal, sw_ref[g], 0.0).astype(BF16)
        bias = sbt_ref[:, g:g + 1]
        for c in range(tm // B_CHUNK):
            rs = slice(c * B_CHUNK, (c + 1) * B_CHUNK)
            cs = slice(g * gc, (g + 1) * gc)
            mixed = _dot(wg, vn[rs, cs]) + bias
            ob_ref[rs, cs] = (ub[rs, cs] * mixed).astype(BF16)


def _even_in(h, g, w, cosf, sins, lng, lnb, sgu_w, sgu_bt):
    s = h.shape[0]
    tm = min(ROW_TILE, s)
    n_in = w.shape[1]
    row_spec = lambda width: pl.BlockSpec((tm, width), lambda i: (i, 0))
    full = lambda shape: pl.BlockSpec(shape, lambda i: (0,) * len(shape))
    out = jax.ShapeDtypeStruct((s, A_WIDTH), BF16)
    return pl.pallas_call(
        _even_in_kernel,
        grid=(s // tm,),
        in_specs=[row_spec(D_MODEL), full((1, D_MODEL)), full((D_MODEL, n_in)),
                  row_spec(LANES), row_spec(LANES), full((1, B_WIDTH)), full((1, B_WIDTH)),
                  full((B_GROUPS, B_CHUNK, B_CHUNK)), full((B_CHUNK, B_GROUPS))],
        out_specs=[row_spec(A_WIDTH), row_spec(A_WIDTH),
                   pl.BlockSpec((A_HEADS * VT_ROWS, tm), lambda i: (0, i)), row_spec(A_WIDTH)],
        out_shape=[out, out, jax.ShapeDtypeStruct((A_HEADS * VT_ROWS, s), BF16), out],
        compiler_params=_cparams("parallel"),
        name="even_in",
    )(h, g, w, cosf, sins, lng, lnb, sgu_w, sgu_bt)


def _flash_core(qi, t, score_fn, vt_ref, bufs, m_ref, acc_ref, per_half_values):
    m_ref[...] = jnp.full(m_ref.shape, -jnp.inf, F32)
    acc_ref[...] = jnp.zeros(acc_ref.shape, F32)
    n = acc_ref.shape[0]
    s_a, s_b = bufs[0], bufs[1]

    def process(buf, j, diagonal):
        s = buf[0][...]
        if diagonal:
            key = lax.broadcasted_iota(jnp.int32, s.shape, 0)
            qry = lax.broadcasted_iota(jnp.int32, s.shape, 1) % t
            s = jnp.where(key <= qry, s, -jnp.inf)
            tile_max = jnp.max(s, axis=0, keepdims=True)
        else:
            tile_max = buf[1][...]
        m_prev = m_ref[...]
        m_new = jnp.maximum(m_prev, tile_max)
        alpha = jnp.exp2(m_prev - m_new)
        p = jnp.exp2(s - m_new).astype(BF16)
        off = pl.multiple_of(j * t, t)
        if per_half_values:
            for half in range(2):
                cols = slice(half * t, (half + 1) * t)
                acc_ref[:, cols] = (acc_ref[:, cols] * alpha[:, cols]
                                    + _dot(vt_ref[half * n:(half + 1) * n, pl.ds(off, t)], p[:, cols]))
        else:
            acc_ref[...] = acc_ref[...] * alpha + _dot(vt_ref[:, pl.ds(off, t)], p)
        m_ref[...] = m_new

    def score(j, buf):
        score_fn(j, buf[0], buf[1])

    score(0, s_a)

    def pair(j):
        score(j + 1, s_b)
        process(s_a, j, False)
        score(j + 2, s_a)
        process(s_b, j + 1, False)

    def body4(it, carry):
        pair(4 * it)
        pair(4 * it + 2)
        return carry

    def body2(it, carry):
        pair(2 * it)
        return carry

    def body8(it, carry):
        for u in range(4):
            pair(8 * it + 2 * u)
        return carry

    lax.fori_loop(0, qi // 8, body8, 0)
    lax.fori_loop(qi // 8 * 2, qi // 4, body4, 0)
    lax.fori_loop(qi // 4 * 2, qi // 2, body2, 0)

    @pl.when(qi % 2 == 1)
    def _():
        score(qi, s_b)
        process(s_a, qi - 1, False)
        process(s_b, qi, True)

    @pl.when(qi % 2 == 0)
    def _():
        process(s_a, qi, True)


def _attn_scratch(t, acc_rows):
    return [pltpu.VMEM((LANES, 2 * t), BF16), pltpu.VMEM((t, 2 * t), F32), pltpu.VMEM((t, 2 * t), F32),
            pltpu.VMEM((1, 2 * t), F32), pltpu.VMEM((1, 2 * t), F32),
            pltpu.VMEM((1, 2 * t), F32), pltpu.VMEM((acc_rows, 2 * t), F32)]


def _diff_attn_kernel(q_ref, k_ref, vt_ref, lq1_ref, lk1_ref, lq2_ref, lk2_ref, sg_ref, w0_ref, w1_ref, w2_ref,
                      o_ref, c0_ref, c1_ref, c2_ref, qt_ref, s_a, s_b, mx_a, mx_b, m_ref, acc_ref, *, lam_init):
    t = q_ref.shape[0]
    qi = pl.program_id(1)
    c0_ref[...] = w0_ref[...].astype(BF16)
    c1_ref[...] = w1_ref[...].astype(BF16)
    c2_ref[...] = w2_ref[...].astype(BF16)
    qt = q_ref[...].astype(F32).T
    row = lax.broadcasted_iota(jnp.int32, qt.shape, 0)
    qt_ref[:, 0:t] = jnp.where(row < A_QK_DIM, qt, 0.0).astype(BF16)
    qt_ref[:, t:2 * t] = jnp.where(row >= A_QK_DIM, qt, 0.0).astype(BF16)

    def score_fn(j, dst_ref, max_ref):
        off = pl.multiple_of(j * t, t)
        sc = _dot(k_ref[pl.ds(off, t), :], qt_ref[...])
        dst_ref[...] = sc
        max_ref[...] = jnp.max(sc, axis=0, keepdims=True)

    _flash_core(qi, t, score_fn, vt_ref, ((s_a, mx_a), (s_b, mx_b)), m_ref, acc_ref, per_half_values=False)

    lam = (jnp.exp(jnp.sum(lq1_ref[...] * lk1_ref[...], axis=-1, keepdims=True))
           - jnp.exp(jnp.sum(lq2_ref[...] * lk2_ref[...], axis=-1, keepdims=True)) + lam_init)
    o = (acc_ref[0:LANES, 0:t] / acc_ref[LANES:LANES + 1, 0:t]
         - lam * (acc_ref[0:LANES, t:2 * t] / acc_ref[LANES:LANES + 1, t:2 * t]))
    o = o * lax.rsqrt(jnp.mean(o * o, axis=0, keepdims=True) + NORM_EPS) * sg_ref[...] * (1.0 - lam_init)
    o_ref[...] = o.T.astype(BF16)


def _diff_attn(q, k, vt, lq1, lk1, lq2, lk2, subln_g, lam_init, cast_ws, cast_layer):
    s = q.shape[0]
    t = min(ATTN_TILE, s)
    nq = s // t
    hw = 2 * A_QK_DIM
    small = lambda n: pl.BlockSpec((1, n), lambda h, i: (0, 0))
    steps = A_HEADS * nq
    flat = [w.reshape(w.shape[0], w.shape[1] * w.shape[2], w.shape[3]) for w in cast_ws]
    cast_in = [pl.BlockSpec((None, w.shape[1] // steps, w.shape[2]), lambda h, i: (cast_layer, h * nq + i, 0))
               for w in flat]
    cast_out = [pl.BlockSpec((w.shape[1] // steps, w.shape[2]), lambda h, i: (h * nq + i, 0)) for w in flat]
    outs = pl.pallas_call(
        functools.partial(_diff_attn_kernel, lam_init=lam_init),
        grid=(A_HEADS, nq),
        in_specs=[pl.BlockSpec((t, hw), lambda h, i: (i, h)),
                  pl.BlockSpec((s, hw), lambda h, i: (0, h)),
                  pl.BlockSpec((VT_ROWS, s), lambda h, i: (h, 0)),
                  small(A_QK_DIM), small(A_QK_DIM), small(A_QK_DIM), small(A_QK_DIM),
                  pl.BlockSpec((hw, 1), lambda h, i: (0, 0))] + cast_in,
        out_specs=[pl.BlockSpec((t, hw), lambda h, i: (i, h))] + cast_out,
        out_shape=[jax.ShapeDtypeStruct((s, A_WIDTH), BF16)]
        + [jax.ShapeDtypeStruct(w.shape[1:], BF16) for w in flat],
        scratch_shapes=_attn_scratch(t, VT_ROWS),
        compiler_params=_cparams("parallel", "parallel"),
        name="diff_attn",
    )(q, k, vt, lq1, lk1, lq2, lk2, subln_g, *flat)
    return outs[0], [c.reshape(w.shape[1:]) for c, w in zip(outs[1:], cast_ws)]


def _mla_attn_kernel(q_ref, k_ref, vt_ref, o_ref, qt_ref, s_a, s_b, mx_a, mx_b, m_ref, acc_ref):
    t = q_ref.shape[0]
    qi = pl.program_id(1)
    qt_ref[:, 0:t] = q_ref[:, 0:D_HEAD_PAD].astype(F32).T.astype(BF16)
    qt_ref[:, t:2 * t] = q_ref[:, D_HEAD_PAD:2 * D_HEAD_PAD].astype(F32).T.astype(BF16)

    def score_fn(j, dst_ref, max_ref):
        off = pl.multiple_of(j * t, t)
        for half in range(2):
            cols = slice(half * t, (half + 1) * t)
            sc = _dot(k_ref[pl.ds(off, t), half * D_HEAD_PAD:(half + 1) * D_HEAD_PAD], qt_ref[:, cols])
            dst_ref[:, cols] = sc
            max_ref[:, cols] = jnp.max(sc, axis=0, keepdims=True)

    _flash_core(qi, t, score_fn, vt_ref, ((s_a, mx_a), (s_b, mx_b)), m_ref, acc_ref, per_half_values=True)

    oa = acc_ref[0:D_V, 0:t] / acc_ref[D_V:D_V + 1, 0:t]
    ob = acc_ref[0:D_V, t:2 * t] / acc_ref[D_V:D_V + 1, t:2 * t]
    o_ref[...] = jnp.concatenate([oa, ob], axis=0).T.astype(BF16)


def _mla_attn(q, k, vt):
    s = q.shape[0]
    t = min(ATTN_TILE, s)
    return pl.pallas_call(
        _mla_attn_kernel,
        grid=(D_HEADS // 2, s // t),
        in_specs=[pl.BlockSpec((t, 2 * D_HEAD_PAD), lambda h, i: (i, h)),
                  pl.BlockSpec((s, 2 * D_HEAD_PAD), lambda h, i: (0, h)),
                  pl.BlockSpec((VT_ROWS_D, s), lambda h, i: (h, 0))],
        out_specs=pl.BlockSpec((t, 2 * D_V), lambda h, i: (i, h)),
        out_shape=jax.ShapeDtypeStruct((s, D_HEADS * D_V), BF16),
        scratch_shapes=_attn_scratch(t, VT_ROWS_D // 2),
        compiler_params=_cparams("parallel", "parallel"),
        name="mla_attn",
    )(q, k, vt)


def _mixer_out(h_ref, a_ref, b_ref, wo_ref):
    half = a_ref.shape[1]
    return h_ref[...] + _dot(a_ref[...], wo_ref[0:half, :]) + _dot(b_ref[...], wo_ref[half:2 * half, :])


def _dense_ffn_kernel(h_ref, a_ref, b_ref, wo_ref, g_ref, wg_ref, wu_ref, wd_ref, o_ref, xn_ref):
    f = pl.program_id(1)

    @pl.when(f == 0)
    def _():
        x = _mixer_out(h_ref, a_ref, b_ref, wo_ref)
        xn_ref[...] = _rms(x, g_ref[...]).astype(BF16)
        o_ref[...] = x

    xn = xn_ref[...]
    a = jax.nn.silu(_dot(xn, wg_ref[...])) * _dot(xn, wu_ref[...])
    o_ref[...] += _dot(a.astype(BF16), wd_ref[...])


def _dense_ffn(h, a, b, wo, g, wg, wu, wd, f_tile):
    s = h.shape[0]
    tm = min(ROW_TILE, s)
    ffn = wg.shape[1]
    half = a.shape[1]
    return pl.pallas_call(
        _dense_ffn_kernel,
        grid=(s // tm, ffn // f_tile),
        in_specs=[pl.BlockSpec((tm, D_MODEL), lambda i, f: (i, 0)),
                  pl.BlockSpec((tm, half), lambda i, f: (i, 0)),
                  pl.BlockSpec((tm, half), lambda i, f: (i, 0)),
                  pl.BlockSpec((2 * half, D_MODEL), lambda i, f: (0, 0)),
                  pl.BlockSpec((1, D_MODEL), lambda i, f: (0, 0)),
                  pl.BlockSpec((D_MODEL, f_tile), lambda i, f: (0, f)),
                  pl.BlockSpec((D_MODEL, f_tile), lambda i, f: (0, f)),
                  pl.BlockSpec((f_tile, D_MODEL), lambda i, f: (f, 0))],
        out_specs=pl.BlockSpec((tm, D_MODEL), lambda i, f: (i, 0)),
        out_shape=jax.ShapeDtypeStruct((s, D_MODEL), F32),
        scratch_shapes=[pltpu.VMEM((tm, D_MODEL), BF16)],
        compiler_params=_cparams("parallel", "arbitrary"),
        name="dense_ffn",
    )(h, a, b, wo, g, wg, wu, wd)


def _row_copy(src, dst, sem):
    return pltpu.make_async_copy(src, dst, sem)


def _expert_ffn_kernel(te_ref, nu_ref, src_ref, x_hbm, wg_ref, wu_ref, wd_ref, o_ref, xbuf_ref, xb_ref, sem,
                       *, n_f):
    r = pl.program_id(0)
    f = pl.program_id(1)
    tm = o_ref.shape[0]
    n_used = nu_ref[0]

    def tile_start(tile, slot):
        for t in range(tm):
            _row_copy(x_hbm.at[pl.ds(src_ref[tile * tm + t], 1)], xbuf_ref.at[slot, pl.ds(t, 1)],
                      sem.at[slot]).start()

    def tile_wait(slot):
        _row_copy(x_hbm.at[pl.ds(0, tm)], xbuf_ref.at[slot], sem.at[slot]).wait()

    @pl.when(r < n_used)
    def _():
        @pl.when(jnp.logical_and(r == 0, f == 0))
        def _():
            def first(t, carry):
                _row_copy(x_hbm.at[pl.ds(src_ref[t], 1)], xbuf_ref.at[0, pl.ds(t, 1)], sem.at[0]).start()
                return carry

            lax.fori_loop(0, tm, first, 0)

        for slot in range(2):
            @pl.when(jnp.logical_and(f == 0, r % 2 == slot))
            def _():
                tile_wait(slot)
                xb_ref[...] = xbuf_ref[slot].astype(BF16)
                o_ref[...] = jnp.zeros(o_ref.shape, F32)
                tile_start(jnp.minimum(r + 1, pl.num_programs(0) - 1), 1 - slot)

        xb = xb_ref[...]
        a = jax.nn.silu(_dot(xb, wg_ref[...])) * _dot(xb, wu_ref[...])
        o_ref[...] += _dot(a.astype(BF16), wd_ref[...])

        for slot in range(2):
            @pl.when(jnp.logical_and(jnp.logical_and(r == n_used - 1, f == n_f - 1), r % 2 == slot))
            def _():
                tile_wait(1 - slot)

    @pl.when(jnp.logical_and(r >= n_used, f == 0))
    def _():
        o_ref[...] = jnp.zeros(o_ref.shape, F32)


def _expert_ffn(tile_expert, n_used, src, xn, wg, wu, wd, f_tile):
    rows = src.shape[0]
    tm = MOE_TILE
    ffn = wg.shape[2]
    grid_spec = pltpu.PrefetchScalarGridSpec(
        num_scalar_prefetch=3,
        grid=(rows // tm, ffn // f_tile),
        in_specs=[pl.BlockSpec(memory_space=pl.ANY),
                  pl.BlockSpec((None, D_MODEL, f_tile),
                               lambda r, f, te, nu, sr: (te[r], 0, jnp.where(r < nu[0], f, 0))),
                  pl.BlockSpec((None, D_MODEL, f_tile),
                               lambda r, f, te, nu, sr: (te[r], 0, jnp.where(r < nu[0], f, 0))),
                  pl.BlockSpec((None, f_tile, D_MODEL),
                               lambda r, f, te, nu, sr: (te[r], jnp.where(r < nu[0], f, 0), 0))],
        out_specs=pl.BlockSpec((tm, D_MODEL), lambda r, f, te, nu, sr: (r, 0)),
        scratch_shapes=[pltpu.VMEM((2, tm, D_MODEL), F32), pltpu.VMEM((tm, D_MODEL), BF16),
                        pltpu.SemaphoreType.DMA((2,))],
    )
    return pl.pallas_call(
        functools.partial(_expert_ffn_kernel, n_f=ffn // f_tile),
        grid_spec=grid_spec,
        out_shape=jax.ShapeDtypeStruct((rows, D_MODEL), F32),
        compiler_params=_cparams("arbitrary", "arbitrary"),
        name="expert_ffn",
    )(tile_expert, n_used, src, xn, wg, wu, wd)


def _odd_in_kernel(h_ref, g_ref, w_ref, wkpe_ref, cos_ref, sin_ref, pw_ref, ps_ref,
                   qg_ref, wuq_ref, kvg_ref, wk_ref, wv_ref,
                   oc_ref, q_ref, k_ref, v_ref, halo_ref):
    tm = h_ref.shape[0]
    i = pl.program_id(0)
    xn = _rms(h_ref[...], g_ref[...]).astype(BF16)
    cosf = cos_ref[...]
    sins = sin_ref[...]
    lane = lax.broadcasted_iota(jnp.int32, (tm, LANES), 1)
    lo_mask = lane < (D_NOPE + D_ROPE // 2)

    @pl.when(i == 0)
    def _():
        halo_ref[...] = jnp.zeros(halo_ref.shape, F32)

    xc = _dot(xn, w_ref[:, 0:C_WIDTH])
    ext = jnp.concatenate([halo_ref[...], xc], axis=0)
    halo_ref[...] = xc[tm - C_HALO:tm, :]
    pos = i * tm + lax.broadcasted_iota(jnp.int32, (tm, 1), 0)
    gc = C_WIDTH // len(C_WINDOWS)
    for g, win in enumerate(C_WINDOWS):
        cs = slice(g * gc, (g + 1) * gc)
        a = ext[:, cs]
        span = 1
        while span < win:
            n = a.shape[0] - span
            a = a[span:span + n, :] + a[0:n, :]
            span *= 2
        start = C_HALO - (win - 1)
        wsum = a[start:start + tm, :]
        cnt = jnp.minimum(pos + 1, win).astype(F32)
        pooled = (wsum / cnt - xc[:, cs]).astype(BF16)
        oc_ref[:, cs] = (_dot(pooled, pw_ref[g]) * ps_ref[:, cs]).astype(BF16)

    cq = _dot(xn, w_ref[:, C_WIDTH:C_WIDTH + D_Q_RANK])
    q = _dot(_rms(cq, qg_ref[...]).astype(BF16), wuq_ref[...])
    q = _rope_wide(q, cosf, sins, D_ROPE // 2, lo_mask) * ((D_NOPE + D_ROPE) ** -0.5 * LOG2E)
    q_ref[...] = q.astype(BF16)

    ckv = _dot(xn, w_ref[:, C_WIDTH + D_Q_RANK:C_WIDTH + D_Q_RANK + D_KV_RANK])
    ckvn = _rms(ckv, kvg_ref[...]).astype(BF16)
    kpe = _rope_block(_dot(xn, wkpe_ref[...]), cosf, sins, D_ROPE // 2, lo_mask)
    kn = _dot(ckvn, wk_ref[...])
    k_ref[...] = jnp.concatenate(
        [kn[:, b * LANES:(b + 1) * LANES] + kpe for b in range(D_HEADS)], axis=1).astype(BF16)
    _store_vt(v_ref, _dot(ckvn, wv_ref[...]), D_V)


def _odd_in(h, g, w, wkpe, cosf, sins, pool_w, pool_scale, qg, wuq, kvg, wk, wv):
    s = h.shape[0]
    tm = min(ROW_TILE, s)
    row_spec = lambda width: pl.BlockSpec((tm, width), lambda i: (i, 0))
    full = lambda shape: pl.BlockSpec(shape, lambda i: (0,) * len(shape))
    return pl.pallas_call(
        _odd_in_kernel,
        grid=(s // tm,),
        in_specs=[row_spec(D_MODEL), full((1, D_MODEL)), full(w.shape), full(wkpe.shape),
                  row_spec(LANES), row_spec(LANES), full(pool_w.shape), full((1, C_WIDTH)),
                  full((1, D_Q_RANK)), full(wuq.shape), full((1, D_KV_RANK)), full(wk.shape), full(wv.shape)],
        out_specs=[row_spec(C_WIDTH), row_spec(D_HEADS * D_HEAD_PAD), row_spec(D_HEADS * D_HEAD_PAD),
                   pl.BlockSpec((D_HEADS // 2 * VT_ROWS_D, tm), lambda i: (0, i))],
        out_shape=[jax.ShapeDtypeStruct((s, C_WIDTH), BF16),
                   jax.ShapeDtypeStruct((s, D_HEADS * D_HEAD_PAD), BF16),
                   jax.ShapeDtypeStruct((s, D_HEADS * D_HEAD_PAD), BF16),
                   jax.ShapeDtypeStruct((D_HEADS // 2 * VT_ROWS_D, s), BF16)],
        scratch_shapes=[pltpu.VMEM((C_HALO, C_WIDTH), F32)],
        compiler_params=_cparams("arbitrary"),
        name="odd_in",
    )(h, g, w, wkpe, cosf, sins, pool_w, pool_scale, qg, wuq, kvg, wk, wv)


def _router_kernel(h_ref, a_ref, b_ref, wo_ref, g_ref, rt_ref,
                   h2_ref, xn_ref, eidx_ref, rank_ref, wcol_ref, cnt_ref, run_ref):
    tm = h_ref.shape[0]
    i = pl.program_id(0)

    @pl.when(i == 0)
    def _():
        run_ref[...] = jnp.zeros(run_ref.shape, F32)

    x = _mixer_out(h_ref, a_ref, b_ref, wo_ref)
    h2_ref[...] = x
    xn = _rms(x, g_ref[...])
    xn_ref[...] = xn
    lg = lax.dot_general(rt_ref[...], xn, (((1,), (1,)), ((), ())),
                         preferred_element_type=F32, precision=lax.Precision.HIGHEST)
    eio = lax.broadcasted_iota(jnp.int32, lg.shape, 0)
    m1 = jnp.max(lg, axis=0, keepdims=True)
    i1 = jnp.min(jnp.where(lg == m1, eio, N_EXPERTS), axis=0, keepdims=True)
    lg2 = jnp.where(eio == i1, -jnp.inf, lg)
    m2 = jnp.max(lg2, axis=0, keepdims=True)
    i2 = jnp.min(jnp.where(lg2 == m2, eio, N_EXPERTS), axis=0, keepdims=True)
    e = jnp.exp(m2 - m1)
    w1 = 1.0 / (1.0 + e)
    w2 = e / (1.0 + e)

    sel1 = eio == i1
    sel2 = eio == i2
    onehot = jnp.logical_or(sel1, sel2)
    tr = lax.broadcasted_iota(jnp.int32, (tm, tm), 0)
    tc = lax.broadcasted_iota(jnp.int32, (tm, tm), 1)
    upper = (tr < tc).astype(BF16)
    before = _dot(onehot.astype(BF16), upper) + run_ref[:, 0:1]
    r1 = jnp.sum(jnp.where(sel1, before, 0.0), axis=0, keepdims=True)
    r2 = jnp.sum(jnp.where(sel2, before, 0.0), axis=0, keepdims=True)
    run_new = run_ref[:, 0:1] + jnp.sum(onehot.astype(F32), axis=1, keepdims=True)
    run_ref[...] = jnp.broadcast_to(run_new, run_ref.shape)
    cnt_ref[...] = jnp.broadcast_to(run_new, cnt_ref.shape).astype(jnp.int32)

    eidx_ref[...] = jnp.concatenate([i1, i2], axis=0)
    rank_ref[...] = jnp.concatenate([r1, r2], axis=0).astype(jnp.int32)
    w8 = jnp.concatenate([w1, w2, jnp.zeros((N_EXPERTS - 2, tm), F32)], axis=0)
    wcol_ref[...] = w8.T


def _router(h, a, b, wo, g, router_t):
    s = h.shape[0]
    tm = min(ROW_TILE, s)
    half = a.shape[1]
    return pl.pallas_call(
        _router_kernel,
        grid=(s // tm,),
        in_specs=[pl.BlockSpec((tm, D_MODEL), lambda i: (i, 0)),
                  pl.BlockSpec((tm, half), lambda i: (i, 0)),
                  pl.BlockSpec((tm, half), lambda i: (i, 0)),
                  pl.BlockSpec((2 * half, D_MODEL), lambda i: (0, 0)),
                  pl.BlockSpec((1, D_MODEL), lambda i: (0, 0)),
                  pl.BlockSpec((N_EXPERTS, D_MODEL), lambda i: (0, 0))],
        out_specs=[pl.BlockSpec((tm, D_MODEL), lambda i: (i, 0)),
                   pl.BlockSpec((tm, D_MODEL), lambda i: (i, 0)),
                   pl.BlockSpec((2, tm), lambda i: (0, i)),
                   pl.BlockSpec((2, tm), lambda i: (0, i)),
                   pl.BlockSpec((tm, N_EXPERTS), lambda i: (i, 0)),
                   pl.BlockSpec((N_EXPERTS, LANES), lambda i: (0, 0))],
        out_shape=[jax.ShapeDtypeStruct((s, D_MODEL), F32),
                   jax.ShapeDtypeStruct((s, D_MODEL), F32),
                   jax.ShapeDtypeStruct((2, s), jnp.int32),
                   jax.ShapeDtypeStruct((2, s), jnp.int32),
                   jax.ShapeDtypeStruct((s, N_EXPERTS), F32),
                   jax.ShapeDtypeStruct((N_EXPERTS, LANES), jnp.int32)],
        scratch_shapes=[pltpu.VMEM((N_EXPERTS, LANES), F32)],
        compiler_params=_cparams("arbitrary"),
        name="router",
    )(h, a, b, wo, g, router_t)


def _invert_kernel(gap_ref, pos_ref, src_ref):
    n_tok = pos_ref.shape[0] // 2

    def clear(i, carry):
        src_ref[i] = 0
        return carry

    for g in range(gap_ref.shape[0] // 2):
        lax.fori_loop(gap_ref[2 * g], gap_ref[2 * g + 1], clear, 0)

    def put(t, carry):
        src_ref[pos_ref[t]] = t
        src_ref[pos_ref[n_tok + t]] = t
        return carry

    lax.fori_loop(0, n_tok, put, 0, unroll=8)


def _invert(gaps, pos_flat, n_slots):
    return pl.pallas_call(
        _invert_kernel,
        in_specs=[pl.BlockSpec(memory_space=pltpu.SMEM), pl.BlockSpec(memory_space=pltpu.SMEM)],
        out_specs=pl.BlockSpec(memory_space=pltpu.SMEM),
        out_shape=jax.ShapeDtypeStruct((n_slots,), jnp.int32),
        name="invert",
    )(gaps, pos_flat)


def _combine_kernel(pos_ref, ys_hbm, h_ref, wcol_ref, fg_ref, o_ref, buf_ref, sem, *, final_norm):
    tm = h_ref.shape[0]

    for t in range(tm):
        for k in range(2):
            _row_copy(ys_hbm.at[pl.ds(pos_ref[k, t], 1)], buf_ref.at[k, pl.ds(t, 1)], sem).start()
    for k in range(2):
        _row_copy(ys_hbm.at[pl.ds(0, tm)], buf_ref.at[k], sem).wait()
    out = h_ref[...] + wcol_ref[:, 0:1] * buf_ref[0] + wcol_ref[:, 1:2] * buf_ref[1]
    if final_norm:
        out = _rms(out, fg_ref[...])
    o_ref[...] = out


def _combine(pos, ys, h, wcol, final_g, final_norm):
    s = h.shape[0]
    tm = min(ROW_TILE, s)
    return pl.pallas_call(
        functools.partial(_combine_kernel, final_norm=final_norm),
        grid=(s // tm,),
        in_specs=[pl.BlockSpec((2, tm), lambda i: (0, i), memory_space=pltpu.SMEM),
                  pl.BlockSpec(memory_space=pl.ANY),
                  pl.BlockSpec((tm, D_MODEL), lambda i: (i, 0)),
                  pl.BlockSpec((tm, N_EXPERTS), lambda i: (i, 0)),
                  pl.BlockSpec((1, D_MODEL), lambda i: (0, 0))],
        out_specs=pl.BlockSpec((tm, D_MODEL), lambda i: (i, 0)),
        out_shape=jax.ShapeDtypeStruct((s, D_MODEL), F32),
        scratch_shapes=[pltpu.VMEM((2, tm, D_MODEL), F32), pltpu.SemaphoreType.DMA(())],
        compiler_params=_cparams("arbitrary"),
        name="combine",
    )(pos, ys, h, wcol, final_g)


def _final_norm_kernel(h_ref, g_ref, o_ref):
    o_ref[...] = _rms(h_ref[...], g_ref[...])


def _final_norm(h, g):
    s = h.shape[0]
    tm = min(ROW_TILE, s)
    return pl.pallas_call(
        _final_norm_kernel,
        grid=(s // tm,),
        in_specs=[pl.BlockSpec((tm, D_MODEL), lambda i: (i, 0)), pl.BlockSpec((1, D_MODEL), lambda i: (0, 0))],
        out_specs=pl.BlockSpec((tm, D_MODEL), lambda i: (i, 0)),
        out_shape=jax.ShapeDtypeStruct((s, D_MODEL), F32),
        compiler_params=_cparams("parallel"),
        name="final_norm",
    )(h, g)


def _rope_lane_tables(seq, rot_dim, first_lane, period):
    half = rot_dim // 2
    pos = jnp.arange(seq, dtype=F32)
    inv = ROPE_THETA ** (-jnp.arange(0, rot_dim, 2, dtype=F32) / rot_dim)
    ang = pos[:, None] * inv[None, :]
    cos, sin = jnp.cos(ang), jnp.sin(ang)
    lane = jnp.arange(LANES) % period - first_lane
    lo = (lane >= 0) & (lane < half)
    hi = (lane >= half) & (lane < rot_dim)
    idx = jnp.clip(jnp.where(hi, lane - half, lane), 0, half - 1)
    cosf = jnp.where((lo | hi)[None, :], cos[:, idx], 1.0)
    sins = jnp.where(lo[None, :], -sin[:, idx], jnp.where(hi[None, :], sin[:, idx], 0.0))
    return cosf, sins


def _even_layer(h, l, p, i, rope_a):
    w_in = p["w_in_even"][i].astype(BF16)
    q, k, v, ob = _even_in(
        h, p["norm_mix_even"][i][None, :], w_in, rope_a[0], rope_a[1],
        p["sgu_ln_g"][i][None, :], p["sgu_ln_b"][i][None, :], p["sgu_w"][i], p["sgu_b"][i].T)
    lam_init = 0.8 - 0.6 * math.exp(-0.3 * l)
    oa, moe_w = _diff_attn(q, k, v, p["lam_q1"][i][None, :], p["lam_k1"][i][None, :],
                           p["lam_q2"][i][None, :], p["lam_k2"][i][None, :], p["subln_g"][i][:, None], lam_init,
                           (p["moe_wg"], p["moe_wu"], p["moe_wd"]), min(i, p["moe_wg"].shape[0] - 1))
    ffn = p["ffn_wg"].shape[2]
    h = _dense_ffn(h, oa, ob, p["w_out_even"][i].astype(BF16), p["norm_ffn_even"][i][None, :],
                   p["ffn_wg"][i].astype(BF16), p["ffn_wu"][i].astype(BF16), p["ffn_wd"][i].astype(BF16),
                   ffn // 2)
    return h, moe_w


def _odd_layer(h, p, i, moe_w, rope_d, final_g, final_norm):
    s = h.shape[0]
    w_in = p["w_in_odd"][i]
    n_main = C_WIDTH + D_Q_RANK + D_KV_RANK
    w_main = w_in[:, :n_main].astype(BF16)
    wkpe = jnp.zeros((D_MODEL, LANES), F32).at[:, D_NOPE:D_NOPE + D_ROPE].set(w_in[:, n_main:]).astype(BF16)
    wuq = jnp.pad(p["w_uq"][i].reshape(D_Q_RANK, D_HEADS, D_NOPE + D_ROPE),
                  ((0, 0), (0, 0), (0, D_HEAD_PAD - D_NOPE - D_ROPE))).reshape(D_Q_RANK, -1).astype(BF16)
    wukv = p["w_ukv"][i].reshape(D_KV_RANK, D_HEADS, D_NOPE + D_V)
    wk = jnp.pad(wukv[:, :, :D_NOPE], ((0, 0), (0, 0), (0, D_HEAD_PAD - D_NOPE))).reshape(D_KV_RANK, -1).astype(BF16)
    wv = wukv[:, :, D_NOPE:].reshape(D_KV_RANK, -1).astype(BF16)
    oc, q, k, v = _odd_in(
        h, p["norm_mix_odd"][i][None, :], w_main, wkpe, rope_d[0], rope_d[1],
        p["pool_w"][i].astype(BF16), p["pool_scale"][i][None, :],
        p["q_norm_g"][i][None, :], wuq, p["kv_norm_g"][i][None, :], wk, wv)
    od = _mla_attn(q, k, v)

    h, xn, eidx, rank, wcol, cnt = _router(h, oc, od, p["w_out_odd"][i].astype(BF16),
                                           p["norm_ffn_odd"][i][None, :], p["router"][i].T)
    counts = cnt[:, 0]
    padded = (counts + MOE_TILE - 1) // MOE_TILE * MOE_TILE
    ends = jnp.cumsum(padded)
    base = ends - padded
    pos = rank
    for e in range(N_EXPERTS):
        pos = pos + jnp.where(eidx == e, base[e], 0)
    n_tiles = (2 * s) // MOE_TILE + N_EXPERTS
    tile_start = jnp.arange(n_tiles, dtype=jnp.int32) * MOE_TILE
    tile_expert = jnp.minimum(jnp.sum(tile_start[:, None] >= ends[None, :], axis=1), N_EXPERTS - 1).astype(jnp.int32)
    n_used = (ends[-1:] // MOE_TILE).astype(jnp.int32)
    n_slots = n_tiles * MOE_TILE
    gaps = jnp.stack([jnp.append(base + counts, ends[-1]), jnp.append(ends, n_slots)], axis=1)
    src = _invert(gaps.reshape(-1).astype(jnp.int32), pos.reshape(-1), n_slots)
    ys = _expert_ffn(tile_expert, n_used, src, xn, moe_w[0], moe_w[1], moe_w[2], MOE_F_TILE)
    return _combine(pos, ys, h, wcol, final_g, final_norm)


def kernel(x, norm_mix_even, w_in_even, lam_q1, lam_k1, lam_q2, lam_k2, subln_g, sgu_ln_g, sgu_ln_b, sgu_w, sgu_b, w_out_even, norm_ffn_even, ffn_wg, ffn_wu, ffn_wd, norm_mix_odd, w_in_odd, pool_w, pool_scale, q_norm_g, w_uq, kv_norm_g, w_ukv, w_out_odd, norm_ffn_odd, router, moe_wg, moe_wu, moe_wd, final_norm):
    p = dict(norm_mix_even=norm_mix_even, w_in_even=w_in_even, lam_q1=lam_q1, lam_k1=lam_k1, lam_q2=lam_q2,
             lam_k2=lam_k2, subln_g=subln_g, sgu_ln_g=sgu_ln_g, sgu_ln_b=sgu_ln_b, sgu_w=sgu_w, sgu_b=sgu_b,
             w_out_even=w_out_even, norm_ffn_even=norm_ffn_even, ffn_wg=ffn_wg, ffn_wu=ffn_wu, ffn_wd=ffn_wd,
             norm_mix_odd=norm_mix_odd, w_in_odd=w_in_odd, pool_w=pool_w, pool_scale=pool_scale,
             q_norm_g=q_norm_g, w_uq=w_uq, kv_norm_g=kv_norm_g, w_ukv=w_ukv, w_out_odd=w_out_odd,
             norm_ffn_odd=norm_ffn_odd, router=router, moe_wg=moe_wg, moe_wu=moe_wu, moe_wd=moe_wd)
    b, s, d = x.shape
    depth = norm_mix_even.shape[0] + norm_mix_odd.shape[0]
    rope_a = _rope_lane_tables(s, A_ROT, 0, A_QK_DIM)
    rope_d = _rope_lane_tables(s, D_ROPE, D_NOPE, LANES)
    final_g = final_norm[None, :]
    outs = []
    for bi in range(b):
        h = x[bi]
        for l in range(depth):
            if l % 2 == 0:
                h, moe_w = _even_layer(h, l, p, l // 2, rope_a)
            else:
                h = _odd_layer(h, p, l // 2, moe_w, rope_d, final_g, final_norm=(l == depth - 1))
        if depth % 2 == 1:
            h = _final_norm(h, final_g)
        outs.append(h)
    return jnp.stack(outs)
```

```python
import functools
import math

import jax
import jax.numpy as jnp
from jax import lax
from jax.experimental import pallas as pl
from jax.experimental.pallas import tpu as pltpu

F32 = jnp.float32
BF16 = jnp.bfloat16

ROPE_THETA = 500000.0
NORM_EPS = 1e-6
LN_EPS = 1e-5
LOG2E = 1.4426950408889634

D_MODEL = 1024
A_HEADS = 4
A_QK_DIM = 64
A_ROT = 16
A_WIDTH = 512
B_WIDTH = 512
B_GROUPS = 4
B_CHUNK = 128
C_WIDTH = 512
C_WINDOWS = (2, 4, 8, 16)
C_HALO = 16
D_HEADS = 8
D_NOPE = 64
D_ROPE = 32
D_V = 64
D_Q_RANK = 384
D_KV_RANK = 256
D_HEAD_PAD = 128
N_EXPERTS = 8

LANES = 128
VT_TAIL = 16
VT_ROWS = LANES + VT_TAIL
VT_ROWS_D = 2 * (D_V + VT_TAIL)
ROW_TILE = 512
ATTN_TILE = 512
MOE_TILE = 512
MOE_F_TILE = 1792
VMEM_LIMIT = 56 * 1024 * 1024


def _cparams(*sem):
    return pltpu.CompilerParams(dimension_semantics=sem, vmem_limit_bytes=VMEM_LIMIT)


def _rms(x, g):
    return x * lax.rsqrt(jnp.mean(x * x, axis=-1, keepdims=True) + NORM_EPS) * g


def _dot(a, b):
    return jnp.dot(a, b, preferred_element_type=F32)


def _rope_block(x, cosf, sins, shift, lo_mask):
    up = pltpu.roll(x, LANES - shift, 1)
    dn = pltpu.roll(x, shift, 1)
    return x * cosf + jnp.where(lo_mask, up, dn) * sins


def _rope_wide(x, cosf, sins, shift, lo_mask):
    nb = x.shape[1] // LANES
    return jnp.concatenate(
        [_rope_block(x[:, b * LANES:(b + 1) * LANES], cosf, sins, shift, lo_mask) for b in range(nb)],
        axis=1)


def _store_vt(vt_ref, v, group):
    rows = v.shape[0]
    vt = v.T
    r = lax.broadcasted_iota(jnp.int32, (VT_TAIL, rows), 0)
    tail = jnp.where(r == 0, 1.0, 0.0).astype(BF16)
    step = group + VT_TAIL
    for b in range(v.shape[1] // group):
        vt_ref[b * step:b * step + group, :] = vt[b * group:(b + 1) * group, :].astype(BF16)
        vt_ref[b * step + group:(b + 1) * step, :] = tail


def _even_in_kernel(h_ref, g_ref, w_ref, cos_ref, sin_ref, lng_ref, lnb_ref, sw_ref, sbt_ref,
                    q_ref, k_ref, v_ref, ob_ref):
    tm = h_ref.shape[0]
    xn = _rms(h_ref[...], g_ref[...]).astype(BF16)
    cosf = cos_ref[...]
    sins = sin_ref[...]
    lane = lax.broadcasted_iota(jnp.int32, (tm, LANES), 1)
    lo_mask = (lane % A_QK_DIM) < (A_ROT // 2)

    q = _dot(xn, w_ref[:, 0:A_WIDTH])
    q = _rope_wide(q, cosf, sins, A_ROT // 2, lo_mask) * (A_QK_DIM ** -0.5 * LOG2E)
    q_ref[...] = q.astype(BF16)
    k = _dot(xn, w_ref[:, A_WIDTH:2 * A_WIDTH])
    k_ref[...] = _rope_wide(k, cosf, sins, A_ROT // 2, lo_mask).astype(BF16)
    _store_vt(v_ref, _dot(xn, w_ref[:, 2 * A_WIDTH:3 * A_WIDTH]), LANES)

    ub = jax.nn.gelu(_dot(xn, w_ref[:, 3 * A_WIDTH:3 * A_WIDTH + B_WIDTH]))
    vb = jax.nn.gelu(_dot(xn, w_ref[:, 3 * A_WIDTH + B_WIDTH:3 * A_WIDTH + 2 * B_WIDTH]))
    mu = jnp.mean(vb, axis=-1, keepdims=True)
    var = jnp.mean(jnp.square(vb - mu), axis=-1, keepdims=True)
    vn = ((vb - mu) * lax.rsqrt(var + LN_EPS) * lng_ref[...] + lnb_ref[...]).astype(BF16)

    row = lax.broadcasted_iota(jnp.int32, (B_CHUNK, B_CHUNK), 0)
    col = lax.broadcasted_iota(jnp.int32, (B_CHUNK, B_CHUNK), 1)
    causal = col <= row
    gc = B_WIDTH // B_GROUPS
    for g in range(B_GROUPS):
        wg = jnp.where(causal, sw_ref[g], 0.0).astype(BF16)
        bias = sbt_ref[:, g:g + 1]
        for c in range(tm // B_CHUNK):
            rs = slice(c * B_CHUNK, (c + 1) * B_CHUNK)
            cs = slice(g * gc, (g + 1) * gc)
            mixed = _dot(wg, vn[rs, cs]) + bias
            ob_ref[rs, cs] = (ub[rs, cs] * mixed).astype(BF16)


def _even_in(h, g, w, cosf, sins, lng, lnb, sgu_w, sgu_bt):
    s = h.shape[0]
    tm = min(ROW_TILE, s)
    n_in = w.shape[1]
    row_spec = lambda width: pl.BlockSpec((tm, width), lambda i: (i, 0))
    full = lambda shape: pl.BlockSpec(shape, lambda i: (0,) * len(shape))
    out = jax.ShapeDtypeStruct((s, A_WIDTH), BF16)
    return pl.pallas_call(
        _even_in_kernel,
        grid=(s // tm,),
        in_specs=[row_spec(D_MODEL), full((1, D_MODEL)), full((D_MODEL, n_in)),
                  row_spec(LANES), row_spec(LANES), full((1, B_WIDTH)), full((1, B_WIDTH)),
                  full((B_GROUPS, B_CHUNK, B_CHUNK)), full((B_CHUNK, B_GROUPS))],
        out_specs=[row_spec(A_WIDTH), row_spec(A_WIDTH),
                   pl.BlockSpec((A_HEADS * VT_ROWS, tm), lambda i: (0, i)), row_spec(A_WIDTH)],
        out_shape=[out, out, jax.ShapeDtypeStruct((A_HEADS * VT_ROWS, s), BF16), out],
        compiler_params=_cparams("parallel"),
        name="even_in",
    )(h, g, w, cosf, sins, lng, lnb, sgu_w, sgu_bt)


def _flash_core(qi, t, score_fn, vt_ref, bufs, m_ref, acc_ref, per_half_values):
    m_ref[...] = jnp.full(m_ref.shape, -jnp.inf, F32)
    acc_ref[...] = jnp.zeros(acc_ref.shape, F32)
    n = acc_ref.shape[0]
    s_a, s_b = bufs[0], bufs[1]

    def process(buf, j, diagonal):
        s = buf[0][...]
        if diagonal:
            key = lax.broadcasted_iota(jnp.int32, s.shape, 0)
            qry = lax.broadcasted_iota(jnp.int32, s.shape, 1) % t
            s = jnp.where(key <= qry, s, -jnp.inf)
            tile_max = jnp.max(s, axis=0, keepdims=True)
        else:
            tile_max = buf[1][...]
        m_prev = m_ref[...]
        m_new = jnp.maximum(m_prev, tile_max)
        alpha = jnp.exp2(m_prev - m_new)
        p = jnp.exp2(s - m_new).astype(BF16)
        off = pl.multiple_of(j * t, t)
        if per_half_values:
            for half in range(2):
                cols = slice(half * t, (half + 1) * t)
                acc_ref[:, cols] = (acc_ref[:, cols] * alpha[:, cols]
                                    + _dot(vt_ref[half * n:(half + 1) * n, pl.ds(off, t)], p[:, cols]))
        else:
            acc_ref[...] = acc_ref[...] * alpha + _dot(vt_ref[:, pl.ds(off, t)], p)
        m_ref[...] = m_new

    def score(j, buf):
        score_fn(j, buf[0], buf[1])

    score(0, s_a)

    def pair(j):
        score(j + 1, s_b)
        process(s_a, j, False)
        score(j + 2, s_a)
        process(s_b, j + 1, False)

    def body4(it, carry):
        pair(4 * it)
        pair(4 * it + 2)
        return carry

    def body2(it, carry):
        pair(2 * it)
        return carry

    def body8(it, carry):
        for u in range(4):
            pair(8 * it + 2 * u)
        return carry

    lax.fori_loop(0, qi // 8, body8, 0)
    lax.fori_loop(qi // 8 * 2, qi // 4, body4, 0)
    lax.fori_loop(qi // 4 * 2, qi // 2, body2, 0)

    @pl.when(qi % 2 == 1)
    def _():
        score(qi, s_b)
        process(s_a, qi - 1, False)
        process(s_b, qi, True)

    @pl.when(qi % 2 == 0)
    def _():
        process(s_a, qi, True)


def _attn_scratch(t, acc_rows):
    return [pltpu.VMEM((LANES, 2 * t), BF16), pltpu.VMEM((t, 2 * t), F32), pltpu.VMEM((t, 2 * t), F32),
            pltpu.VMEM((1, 2 * t), F32), pltpu.VMEM((1, 2 * t), F32),
            pltpu.VMEM((1, 2 * t), F32), pltpu.VMEM((acc_rows, 2 * t), F32)]


def _diff_attn_kernel(q_ref, k_ref, vt_ref, lq1_ref, lk1_ref, lq2_ref, lk2_ref, sg_ref, w0_ref, w1_ref, w2_ref,
                      o_ref, c0_ref, c1_ref, c2_ref, qt_ref, s_a, s_b, mx_a, mx_b, m_ref, acc_ref, *, lam_init):
    t = q_ref.shape[0]
    qi = pl.program_id(1)
    c0_ref[...] = w0_ref[...].astype(BF16)
    c1_ref[...] = w1_ref[...].astype(BF16)
    c2_ref[...] = w2_ref[...].astype(BF16)
    qt = q_ref[...].astype(F32).T
    row = lax.broadcasted_iota(jnp.int32, qt.shape, 0)
    qt_ref[:, 0:t] = jnp.where(row < A_QK_DIM, qt, 0.0).astype(BF16)
    qt_ref[:, t:2 * t] = jnp.where(row >= A_QK_DIM, qt, 0.0).astype(BF16)

    def score_fn(j, dst_ref, max_ref):
        off = pl.multiple_of(j * t, t)
        sc = _dot(k_ref[pl.ds(off, t), :], qt_ref[...])
        dst_ref[...] = sc
        max_ref[...] = jnp.max(sc, axis=0, keepdims=True)

    _flash_core(qi, t, score_fn, vt_ref, ((s_a, mx_a), (s_b, mx_b)), m_ref, acc_ref, per_half_values=False)

    lam = (jnp.exp(jnp.sum(lq1_ref[...] * lk1_ref[...], axis=-1, keepdims=True))
           - jnp.exp(jnp.sum(lq2_ref[...] * lk2_ref[...], axis=-1, keepdims=True)) + lam_init)
    o = (acc_ref[0:LANES, 0:t] / acc_ref[LANES:LANES + 1, 0:t]
         - lam * (acc_ref[0:LANES, t:2 * t] / acc_ref[LANES:LANES + 1, t:2 * t]))
    o = o * lax.rsqrt(jnp.mean(o * o, axis=0, keepdims=True) + NORM_EPS) * sg_ref[...] * (1.0 - lam_init)
    o_ref[...] = o.T.astype(BF16)


def _diff_attn(q, k, vt, lq1, lk1, lq2, lk2, subln_g, lam_init, cast_ws, cast_layer):
    s = q.shape[0]
    t = min(ATTN_TILE, s)
    nq = s // t
    hw = 2 * A_QK_DIM
    small = lambda n: pl.BlockSpec((1, n), lambda h, i: (0, 0))
    steps = A_HEADS * nq
    flat = [w.reshape(w.shape[0], w.shape[1] * w.shape[2], w.shape[3]) for w in cast_ws]
    cast_in = [pl.BlockSpec((None, w.shape[1] // steps, w.shape[2]), lambda h, i: (cast_layer, h * nq + i, 0))
               for w in flat]
    cast_out = [pl.BlockSpec((w.shape[1] // steps, w.shape[2]), lambda h, i: (h * nq + i, 0)) for w in flat]
    outs = pl.pallas_call(
        functools.partial(_diff_attn_kernel, lam_init=lam_init),
        grid=(A_HEADS, nq),
        in_specs=[pl.BlockSpec((t, hw), lambda h, i: (i, h)),
                  pl.BlockSpec((s, hw), lambda h, i: (0, h)),
                  pl.BlockSpec((VT_ROWS, s), lambda h, i: (h, 0)),
                  small(A_QK_DIM), small(A_QK_DIM), small(A_QK_DIM), small(A_QK_DIM),
                  pl.BlockSpec((hw, 1), lambda h, i: (0, 0))] + cast_in,
        out_specs=[pl.BlockSpec((t, hw), lambda h, i: (i, h))] + cast_out,
        out_shape=[jax.ShapeDtypeStruct((s, A_WIDTH), BF16)]
        + [jax.ShapeDtypeStruct(w.shape[1:], BF16) for w in flat],
        scratch_shapes=_attn_scratch(t, VT_ROWS),
        compiler_params=_cparams("parallel", "parallel"),
        name="diff_attn",
    )(q, k, vt, lq1, lk1, lq2, lk2, subln_g, *flat)
    return outs[0], [c.reshape(w.shape[1:]) for c, w in zip(outs[1:], cast_ws)]


def _mla_attn_kernel(q_ref, k_ref, vt_ref, o_ref, qt_ref, s_a, s_b, mx_a, mx_b, m_ref, acc_ref):
    t = q_ref.shape[0]
    qi = pl.program_id(1)
    qt_ref[:, 0:t] = q_ref[:, 0:D_HEAD_PAD].astype(F32).T.astype(BF16)
    qt_ref[:, t:2 * t] = q_ref[:, D_HEAD_PAD:2 * D_HEAD_PAD].astype(F32).T.astype(BF16)

    def score_fn(j, dst_ref, max_ref):
        off = pl.multiple_of(j * t, t)
        for half in range(2):
            cols = slice(half * t, (half + 1) * t)
            sc = _dot(k_ref[pl.ds(off, t), half * D_HEAD_PAD:(half + 1) * D_HEAD_PAD], qt_ref[:, cols])
            dst_ref[:, cols] = sc
            max_ref[:, cols] = jnp.max(sc, axis=0, keepdims=True)

    _flash_core(qi, t, score_fn, vt_ref, ((s_a, mx_a), (s_b, mx_b)), m_ref, acc_ref, per_half_values=True)

    oa = acc_ref[0:D_V, 0:t] / acc_ref[D_V:D_V + 1, 0:t]
    ob = acc_ref[0:D_V, t:2 * t] / acc_ref[D_V:D_V + 1, t:2 * t]
    o_ref[...] = jnp.concatenate([oa, ob], axis=0).T.astype(BF16)


def _mla_attn(q, k, vt):
    s = q.shape[0]
    t = min(ATTN_TILE, s)
    return pl.pallas_call(
        _mla_attn_kernel,
        grid=(D_HEADS // 2, s // t),
        in_specs=[pl.BlockSpec((t, 2 * D_HEAD_PAD), lambda h, i: (i, h)),
                  pl.BlockSpec((s, 2 * D_HEAD_PAD), lambda h, i: (0, h)),
                  pl.BlockSpec((VT_ROWS_D, s), lambda h, i: (h, 0))],
        out_specs=pl.BlockSpec((t, 2 * D_V), lambda h, i: (i, h)),
        out_shape=jax.ShapeDtypeStruct((s, D_HEADS * D_V), BF16),
        scratch_shapes=_attn_scratch(t, VT_ROWS_D // 2),
        compiler_params=_cparams("parallel", "parallel"),
        name="mla_attn",
    )(q, k, vt)


def _mixer_out(h_ref, a_ref, b_ref, wo_ref):
    half = a_ref.shape[1]
    return h_ref[...] + _dot(a_ref[...], wo_ref[0:half, :]) + _dot(b_ref[...], wo_ref[half:2 * half, :])


def _dense_ffn_kernel(h_ref, a_ref, b_ref, wo_ref, g_ref, wg_ref, wu_ref, wd_ref, o_ref, xn_ref):
    f = pl.program_id(1)

    @pl.when(f == 0)
    def _():
        x = _mixer_out(h_ref, a_ref, b_ref, wo_ref)
        xn_ref[...] = _rms(x, g_ref[...]).astype(BF16)
        o_ref[...] = x

    xn = xn_ref[...]
    a = jax.nn.silu(_dot(xn, wg_ref[...])) * _dot(xn, wu_ref[...])
    o_ref[...] += _dot(a.astype(BF16), wd_ref[...])


def _dense_ffn(h, a, b, wo, g, wg, wu, wd, f_tile):
    s = h.shape[0]
    tm = min(ROW_TILE, s)
    ffn = wg.shape[1]
    half = a.shape[1]
    once = pl.Buffered(1) if f_tile == ffn else None
    return pl.pallas_call(
        _dense_ffn_kernel,
        grid=(s // tm, ffn // f_tile),
        in_specs=[pl.BlockSpec((tm, D_MODEL), lambda i, f: (i, 0)),
                  pl.BlockSpec((tm, half), lambda i, f: (i, 0)),
                  pl.BlockSpec((tm, half), lambda i, f: (i, 0)),
                  pl.BlockSpec((2 * half, D_MODEL), lambda i, f: (0, 0), pipeline_mode=once),
                  pl.BlockSpec((1, D_MODEL), lambda i, f: (0, 0)),
                  pl.BlockSpec((D_MODEL, f_tile), lambda i, f: (0, f), pipeline_mode=once),
                  pl.BlockSpec((D_MODEL, f_tile), lambda i, f: (0, f), pipeline_mode=once),
                  pl.BlockSpec((f_tile, D_MODEL), lambda i, f: (f, 0), pipeline_mode=once)],
        out_specs=pl.BlockSpec((tm, D_MODEL), lambda i, f: (i, 0)),
        out_shape=jax.ShapeDtypeStruct((s, D_MODEL), F32),
        scratch_shapes=[pltpu.VMEM((tm, D_MODEL), BF16)],
        compiler_params=_cparams("parallel", "arbitrary"),
        name="dense_ffn",
    )(h, a, b, wo, g, wg, wu, wd)


def _row_copy(src, dst, sem):
    return pltpu.make_async_copy(src, dst, sem)


def _expert_ffn_kernel(te_ref, nu_ref, src_ref, x_hbm, wg_ref, wu_ref, wd_ref, o_ref, xbuf_ref, xb_ref, sem,
                       *, n_f):
    r = pl.program_id(0)
    f = pl.program_id(1)
    tm = o_ref.shape[0]
    n_used = nu_ref[0]

    def tile_start(tile, slot):
        for t in range(tm):
            _row_copy(x_hbm.at[pl.ds(src_ref[tile * tm + t], 1)], xbuf_ref.at[slot, pl.ds(t, 1)],
                      sem.at[slot]).start()

    def tile_wait(slot):
        _row_copy(x_hbm.at[pl.ds(0, tm)], xbuf_ref.at[slot], sem.at[slot]).wait()

    @pl.when(r < n_used)
    def _():
        @pl.when(jnp.logical_and(r == 0, f == 0))
        def _():
            def first(t, carry):
                _row_copy(x_hbm.at[pl.ds(src_ref[t], 1)], xbuf_ref.at[0, pl.ds(t, 1)], sem.at[0]).start()
                return carry

            lax.fori_loop(0, tm, first, 0)

        for slot in range(2):
            @pl.when(jnp.logical_and(f == 0, r % 2 == slot))
            def _():
                tile_wait(slot)
                xb_ref[...] = xbuf_ref[slot].astype(BF16)
                o_ref[...] = jnp.zeros(o_ref.shape, F32)
                tile_start(jnp.minimum(r + 1, pl.num_programs(0) - 1), 1 - slot)

        xb = xb_ref[...]
        a = jax.nn.silu(_dot(xb, wg_ref[...])) * _dot(xb, wu_ref[...])
        o_ref[...] += _dot(a.astype(BF16), wd_ref[...])

        for slot in range(2):
            @pl.when(jnp.logical_and(jnp.logical_and(r == n_used - 1, f == n_f - 1), r % 2 == slot))
            def _():
                tile_wait(1 - slot)

    @pl.when(jnp.logical_and(r >= n_used, f == 0))
    def _():
        o_ref[...] = jnp.zeros(o_ref.shape, F32)


def _expert_ffn(tile_expert, n_used, src, xn, wg, wu, wd, f_tile):
    rows = src.shape[0]
    tm = MOE_TILE
    ffn = wg.shape[2]
    grid_spec = pltpu.PrefetchScalarGridSpec(
        num_scalar_prefetch=3,
        grid=(rows // tm, ffn // f_tile),
        in_specs=[pl.BlockSpec(memory_space=pl.ANY),
                  pl.BlockSpec((None, D_MODEL, f_tile),
                               lambda r, f, te, nu, sr: (te[r], 0, jnp.where(r < nu[0], f, 0))),
                  pl.BlockSpec((None, D_MODEL, f_tile),
                               lambda r, f, te, nu, sr: (te[r], 0, jnp.where(r < nu[0], f, 0))),
                  pl.BlockSpec((None, f_tile, D_MODEL),
                               lambda r, f, te, nu, sr: (te[r], jnp.where(r < nu[0], f, 0), 0))],
        out_specs=pl.BlockSpec((tm, D_MODEL), lambda r, f, te, nu, sr: (r, 0)),
        scratch_shapes=[pltpu.VMEM((2, tm, D_MODEL), F32), pltpu.VMEM((tm, D_MODEL), BF16),
                        pltpu.SemaphoreType.DMA((2,))],
    )
    return pl.pallas_call(
        functools.partial(_expert_ffn_kernel, n_f=ffn // f_tile),
        grid_spec=grid_spec,
        out_shape=jax.ShapeDtypeStruct((rows, D_MODEL), F32),
        compiler_params=_cparams("arbitrary", "arbitrary"),
        name="expert_ffn",
    )(tile_expert, n_used, src, xn, wg, wu, wd)


def _odd_in_kernel(h_ref, g_ref, w_ref, wkpe_ref, cos_ref, sin_ref, pw_ref, ps_ref,
                   qg_ref, wuq_ref, kvg_ref, wk_ref, wv_ref,
                   oc_ref, q_ref, k_ref, v_ref, halo_ref):
    tm = h_ref.shape[0]
    i = pl.program_id(0)
    xn = _rms(h_ref[...], g_ref[...]).astype(BF16)
    cosf = cos_ref[...]
    sins = sin_ref[...]
    lane = lax.broadcasted_iota(jnp.int32, (tm, LANES), 1)
    lo_mask = lane < (D_NOPE + D_ROPE // 2)

    @pl.when(i == 0)
    def _():
        halo_ref[...] = jnp.zeros(halo_ref.shape, F32)

    xc = _dot(xn, w_ref[:, 0:C_WIDTH])
    ext = jnp.concatenate([halo_ref[...], xc], axis=0)
    halo_ref[...] = xc[tm - C_HALO:tm, :]
    pos = i * tm + lax.broadcasted_iota(jnp.int32, (tm, 1), 0)
    gc = C_WIDTH // len(C_WINDOWS)
    for g, win in enumerate(C_WINDOWS):
        cs = slice(g * gc, (g + 1) * gc)
        a = ext[:, cs]
        span = 1
        while span < win:
            n = a.shape[0] - span
            a = a[span:span + n, :] + a[0:n, :]
            span *= 2
        start = C_HALO - (win - 1)
        wsum = a[start:start + tm, :]
        cnt = jnp.minimum(pos + 1, win).astype(F32)
        pooled = (wsum / cnt - xc[:, cs]).astype(BF16)
        oc_ref[:, cs] = (_dot(pooled, pw_ref[g]) * ps_ref[:, cs]).astype(BF16)

    cq = _dot(xn, w_ref[:, C_WIDTH:C_WIDTH + D_Q_RANK])
    q = _dot(_rms(cq, qg_ref[...]).astype(BF16), wuq_ref[...])
    q = _rope_wide(q, cosf, sins, D_ROPE // 2, lo_mask) * ((D_NOPE + D_ROPE) ** -0.5 * LOG2E)
    q_ref[...] = q.astype(BF16)

    ckv = _dot(xn, w_ref[:, C_WIDTH + D_Q_RANK:C_WIDTH + D_Q_RANK + D_KV_RANK])
    ckvn = _rms(ckv, kvg_ref[...]).astype(BF16)
    kpe = _rope_block(_dot(xn, wkpe_ref[...]), cosf, sins, D_ROPE // 2, lo_mask)
    kn = _dot(ckvn, wk_ref[...])
    k_ref[...] = jnp.concatenate(
        [kn[:, b * LANES:(b + 1) * LANES] + kpe for b in range(D_HEADS)], axis=1).astype(BF16)
    _store_vt(v_ref, _dot(ckvn, wv_ref[...]), D_V)


def _odd_in(h, g, w, wkpe, cosf, sins, pool_w, pool_scale, qg, wuq, kvg, wk, wv):
    s = h.shape[0]
    tm = min(ROW_TILE, s)
    row_spec = lambda width: pl.BlockSpec((tm, width), lambda i: (i, 0))
    full = lambda shape: pl.BlockSpec(shape, lambda i: (0,) * len(shape))
    return pl.pallas_call(
        _odd_in_kernel,
        grid=(s // tm,),
        in_specs=[row_spec(D_MODEL), full((1, D_MODEL)), full(w.shape), full(wkpe.shape),
                  row_spec(LANES), row_spec(LANES), full(pool_w.shape), full((1, C_WIDTH)),
                  full((1, D_Q_RANK)), full(wuq.shape), full((1, D_KV_RANK)), full(wk.shape), full(wv.shape)],
        out_specs=[row_spec(C_WIDTH), row_spec(D_HEADS * D_HEAD_PAD), row_spec(D_HEADS * D_HEAD_PAD),
                   pl.BlockSpec((D_HEADS // 2 * VT_ROWS_D, tm), lambda i: (0, i))],
        out_shape=[jax.ShapeDtypeStruct((s, C_WIDTH), BF16),
                   jax.ShapeDtypeStruct((s, D_HEADS * D_HEAD_PAD), BF16),
                   jax.ShapeDtypeStruct((s, D_HEADS * D_HEAD_PAD), BF16),
                   jax.ShapeDtypeStruct((D_HEADS // 2 * VT_ROWS_D, s), BF16)],
        scratch_shapes=[pltpu.VMEM((C_HALO, C_WIDTH), F32)],
        compiler_params=_cparams("arbitrary"),
        name="odd_in",
    )(h, g, w, wkpe, cosf, sins, pool_w, pool_scale, qg, wuq, kvg, wk, wv)


def _router_kernel(h_ref, a_ref, b_ref, wo_ref, g_ref, rt_ref,
                   h2_ref, xn_ref, eidx_ref, rank_ref, wcol_ref, cnt_ref, run_ref):
    tm = h_ref.shape[0]
    i = pl.program_id(0)

    @pl.when(i == 0)
    def _():
        run_ref[...] = jnp.zeros(run_ref.shape, F32)

    x = _mixer_out(h_ref, a_ref, b_ref, wo_ref)
    h2_ref[...] = x
    xn = _rms(x, g_ref[...])
    xn_ref[...] = xn
    lg = lax.dot_general(rt_ref[...], xn, (((1,), (1,)), ((), ())),
                         preferred_element_type=F32, precision=lax.Precision.HIGHEST)
    eio = lax.broadcasted_iota(jnp.int32, lg.shape, 0)
    m1 = jnp.max(lg, axis=0, keepdims=True)
    i1 = jnp.min(jnp.where(lg == m1, eio, N_EXPERTS), axis=0, keepdims=True)
    lg2 = jnp.where(eio == i1, -jnp.inf, lg)
    m2 = jnp.max(lg2, axis=0, keepdims=True)
    i2 = jnp.min(jnp.where(lg2 == m2, eio, N_EXPERTS), axis=0, keepdims=True)
    e = jnp.exp(m2 - m1)
    w1 = 1.0 / (1.0 + e)
    w2 = e / (1.0 + e)

    sel1 = eio == i1
    sel2 = eio == i2
    onehot = jnp.logical_or(sel1, sel2)
    tr = lax.broadcasted_iota(jnp.int32, (tm, tm), 0)
    tc = lax.broadcasted_iota(jnp.int32, (tm, tm), 1)
    upper = (tr < tc).astype(BF16)
    before = _dot(onehot.astype(BF16), upper) + run_ref[:, 0:1]
    r1 = jnp.sum(jnp.where(sel1, before, 0.0), axis=0, keepdims=True)
    r2 = jnp.sum(jnp.where(sel2, before, 0.0), axis=0, keepdims=True)
    run_new = run_ref[:, 0:1] + jnp.sum(onehot.astype(F32), axis=1, keepdims=True)
    run_ref[...] = jnp.broadcast_to(run_new, run_ref.shape)
    cnt_ref[...] = jnp.broadcast_to(run_new, cnt_ref.shape).astype(jnp.int32)

    eidx_ref[...] = jnp.concatenate([i1, i2], axis=0)
    rank_ref[...] = jnp.concatenate([r1, r2], axis=0).astype(jnp.int32)
    w8 = jnp.concatenate([w1, w2, jnp.zeros((N_EXPERTS - 2, tm), F32)], axis=0)
    wcol_ref[...] = w8.T


def _router(h, a, b, wo, g, router_t):
    s = h.shape[0]
    tm = min(ROW_TILE, s)
    half = a.shape[1]
    return pl.pallas_call(
        _router_kernel,
        grid=(s // tm,),
        in_specs=[pl.BlockSpec((tm, D_MODEL), lambda i: (i, 0)),
                  pl.BlockSpec((tm, half), lambda i: (i, 0)),
                  pl.BlockSpec((tm, half), lambda i: (i, 0)),
                  pl.BlockSpec((2 * half, D_MODEL), lambda i: (0, 0)),
                  pl.BlockSpec((1, D_MODEL), lambda i: (0, 0)),
                  pl.BlockSpec((N_EXPERTS, D_MODEL), lambda i: (0, 0))],
        out_specs=[pl.BlockSpec((tm, D_MODEL), lambda i: (i, 0)),
                   pl.BlockSpec((tm, D_MODEL), lambda i: (i, 0)),
                   pl.BlockSpec((2, tm), lambda i: (0, i)),
                   pl.BlockSpec((2, tm), lambda i: (0, i)),
                   pl.BlockSpec((tm, N_EXPERTS), lambda i: (i, 0)),
                   pl.BlockSpec((N_EXPERTS, LANES), lambda i: (0, 0))],
        out_shape=[jax.ShapeDtypeStruct((s, D_MODEL), F32),
                   jax.ShapeDtypeStruct((s, D_MODEL), F32),
                   jax.ShapeDtypeStruct((2, s), jnp.int32),
                   jax.ShapeDtypeStruct((2, s), jnp.int32),
                   jax.ShapeDtypeStruct((s, N_EXPERTS), F32),
                   jax.ShapeDtypeStruct((N_EXPERTS, LANES), jnp.int32)],
        scratch_shapes=[pltpu.VMEM((N_EXPERTS, LANES), F32)],
        compiler_params=_cparams("arbitrary"),
        name="router",
    )(h, a, b, wo, g, router_t)


def _invert_kernel(gap_ref, pos_ref, src_ref):
    n_tok = pos_ref.shape[0] // 2

    def clear(i, carry):
        src_ref[i] = 0
        return carry

    for g in range(gap_ref.shape[0] // 2):
        lax.fori_loop(gap_ref[2 * g], gap_ref[2 * g + 1], clear, 0)

    def put(t, carry):
        src_ref[pos_ref[t]] = t
        src_ref[pos_ref[n_tok + t]] = t
        return carry

    lax.fori_loop(0, n_tok, put, 0, unroll=8)


def _invert(gaps, pos_flat, n_slots):
    return pl.pallas_call(
        _invert_kernel,
        in_specs=[pl.BlockSpec(memory_space=pltpu.SMEM), pl.BlockSpec(memory_space=pltpu.SMEM)],
        out_specs=pl.BlockSpec(memory_space=pltpu.SMEM),
        out_shape=jax.ShapeDtypeStruct((n_slots,), jnp.int32),
        name="invert",
    )(gaps, pos_flat)


def _combine_kernel(pos_ref, nxt_ref, ys_hbm, h_ref, wcol_ref, fg_ref, o_ref, buf_ref, sem, *, final_norm):
    i = pl.program_id(0)
    tm = h_ref.shape[0]

    def tile_start(idx_ref, slot):
        for t in range(tm):
            for k in range(2):
                _row_copy(ys_hbm.at[pl.ds(idx_ref[k, t], 1)], buf_ref.at[slot, k, pl.ds(t, 1)],
                          sem.at[slot]).start()

    @pl.when(i == 0)
    def _():
        tile_start(pos_ref, 0)

    for slot in range(2):
        @pl.when(jnp.logical_and(i % 2 == slot, i + 1 < pl.num_programs(0)))
        def _():
            tile_start(nxt_ref, 1 - slot)

    for slot in range(2):
        @pl.when(i % 2 == slot)
        def _():
            for k in range(2):
                _row_copy(ys_hbm.at[pl.ds(0, tm)], buf_ref.at[slot, k], sem.at[slot]).wait()
            out = h_ref[...] + wcol_ref[:, 0:1] * buf_ref[slot, 0] + wcol_ref[:, 1:2] * buf_ref[slot, 1]
            if final_norm:
                out = _rms(out, fg_ref[...])
            o_ref[...] = out


def _combine(pos, ys, h, wcol, final_g, final_norm):
    s = h.shape[0]
    tm = min(ROW_TILE, s)
    last = s // tm - 1
    return pl.pallas_call(
        functools.partial(_combine_kernel, final_norm=final_norm),
        grid=(s // tm,),
        in_specs=[pl.BlockSpec((2, tm), lambda i: (0, i), memory_space=pltpu.SMEM),
                  pl.BlockSpec((2, tm), lambda i: (0, jnp.minimum(i + 1, last)), memory_space=pltpu.SMEM),
                  pl.BlockSpec(memory_space=pl.ANY),
                  pl.BlockSpec((tm, D_MODEL), lambda i: (i, 0)),
                  pl.BlockSpec((tm, N_EXPERTS), lambda i: (i, 0)),
                  pl.BlockSpec((1, D_MODEL), lambda i: (0, 0))],
        out_specs=pl.BlockSpec((tm, D_MODEL), lambda i: (i, 0)),
        out_shape=jax.ShapeDtypeStruct((s, D_MODEL), F32),
        scratch_shapes=[pltpu.VMEM((2, 2, tm, D_MODEL), F32), pltpu.SemaphoreType.DMA((2,))],
        compiler_params=_cparams("arbitrary"),
        name="combine",
    )(pos, pos, ys, h, wcol, final_g)


def _final_norm_kernel(h_ref, g_ref, o_ref):
    o_ref[...] = _rms(h_ref[...], g_ref[...])


def _final_norm(h, g):
    s = h.shape[0]
    tm = min(ROW_TILE, s)
    return pl.pallas_call(
        _final_norm_kernel,
        grid=(s // tm,),
        in_specs=[pl.BlockSpec((tm, D_MODEL), lambda i: (i, 0)), pl.BlockSpec((1, D_MODEL), lambda i: (0, 0))],
        out_specs=pl.BlockSpec((tm, D_MODEL), lambda i: (i, 0)),
        out_shape=jax.ShapeDtypeStruct((s, D_MODEL), F32),
        compiler_params=_cparams("parallel"),
        name="final_norm",
    )(h, g)


def _rope_lane_tables(seq, rot_dim, first_lane, period):
    half = rot_dim // 2
    pos = jnp.arange(seq, dtype=F32)
    inv = ROPE_THETA ** (-jnp.arange(0, rot_dim, 2, dtype=F32) / rot_dim)
    ang = pos[:, None] * inv[None, :]
    cos, sin = jnp.cos(ang), jnp.sin(ang)
    lane = jnp.arange(LANES) % period - first_lane
    lo = (lane >= 0) & (lane < half)
    hi = (lane >= half) & (lane < rot_dim)
    idx = jnp.clip(jnp.where(hi, lane - half, lane), 0, half - 1)
    cosf = jnp.where((lo | hi)[None, :], cos[:, idx], 1.0)
    sins = jnp.where(lo[None, :], -sin[:, idx], jnp.where(hi[None, :], sin[:, idx], 0.0))
    return cosf, sins


def _even_layer(h, l, p, i, rope_a):
    w_in = p["w_in_even"][i].astype(BF16)
    q, k, v, ob = _even_in(
        h, p["norm_mix_even"][i][None, :], w_in, rope_a[0], rope_a[1],
        p["sgu_ln_g"][i][None, :], p["sgu_ln_b"][i][None, :], p["sgu_w"][i], p["sgu_b"][i].T)
    lam_init = 0.8 - 0.6 * math.exp(-0.3 * l)
    oa, moe_w = _diff_attn(q, k, v, p["lam_q1"][i][None, :], p["lam_k1"][i][None, :],
                           p["lam_q2"][i][None, :], p["lam_k2"][i][None, :], p["subln_g"][i][:, None], lam_init,
                           (p["moe_wg"], p["moe_wu"], p["moe_wd"]), min(i, p["moe_wg"].shape[0] - 1))
    ffn = p["ffn_wg"].shape[2]
    h = _dense_ffn(h, oa, ob, p["w_out_even"][i].astype(BF16), p["norm_ffn_even"][i][None, :],
                   p["ffn_wg"][i].astype(BF16), p["ffn_wu"][i].astype(BF16), p["ffn_wd"][i].astype(BF16),
                   ffn)
    return h, moe_w


def _odd_layer(h, p, i, moe_w, rope_d, final_g, final_norm):
    s = h.shape[0]
    w_in = p["w_in_odd"][i]
    n_main = C_WIDTH + D_Q_RANK + D_KV_RANK
    w_main = w_in[:, :n_main].astype(BF16)
    wkpe = jnp.zeros((D_MODEL, LANES), F32).at[:, D_NOPE:D_NOPE + D_ROPE].set(w_in[:, n_main:]).astype(BF16)
    wuq = jnp.pad(p["w_uq"][i].reshape(D_Q_RANK, D_HEADS, D_NOPE + D_ROPE),
                  ((0, 0), (0, 0), (0, D_HEAD_PAD - D_NOPE - D_ROPE))).reshape(D_Q_RANK, -1).astype(BF16)
    wukv = p["w_ukv"][i].reshape(D_KV_RANK, D_HEADS, D_NOPE + D_V)
    wk = jnp.pad(wukv[:, :, :D_NOPE], ((0, 0), (0, 0), (0, D_HEAD_PAD - D_NOPE))).reshape(D_KV_RANK, -1).astype(BF16)
    wv = wukv[:, :, D_NOPE:].reshape(D_KV_RANK, -1).astype(BF16)
    oc, q, k, v = _odd_in(
        h, p["norm_mix_odd"][i][None, :], w_main, wkpe, rope_d[0], rope_d[1],
        p["pool_w"][i].astype(BF16), p["pool_scale"][i][None, :],
        p["q_norm_g"][i][None, :], wuq, p["kv_norm_g"][i][None, :], wk, wv)
    od = _mla_attn(q, k, v)

    h, xn, eidx, rank, wcol, cnt = _router(h, oc, od, p["w_out_odd"][i].astype(BF16),
                                           p["norm_ffn_odd"][i][None, :], p["router"][i].T)
    counts = cnt[:, 0]
    padded = (counts + MOE_TILE - 1) // MOE_TILE * MOE_TILE
    ends = jnp.cumsum(padded)
    base = ends - padded
    pos = rank
    for e in range(N_EXPERTS):
        pos = pos + jnp.where(eidx == e, base[e], 0)
    n_tiles = (2 * s) // MOE_TILE + N_EXPERTS
    tile_start = jnp.arange(n_tiles, dtype=jnp.int32) * MOE_TILE
    tile_expert = jnp.minimum(jnp.sum(tile_start[:, None] >= ends[None, :], axis=1), N_EXPERTS - 1).astype(jnp.int32)
    n_used = (ends[-1:] // MOE_TILE).astype(jnp.int32)
    n_slots = n_tiles * MOE_TILE
    gaps = jnp.stack([jnp.append(base + counts, ends[-1]), jnp.append(ends, n_slots)], axis=1)
    src = _invert(gaps.reshape(-1).astype(jnp.int32), pos.reshape(-1), n_slots)
    ys = _expert_ffn(tile_expert, n_used, src, xn, moe_w[0], moe_w[1], moe_w[2], MOE_F_TILE)
    return _combine(pos, ys, h, wcol, final_g, final_norm)


def kernel(x, norm_mix_even, w_in_even, lam_q1, lam_k1, lam_q2, lam_k2, subln_g, sgu_ln_g, sgu_ln_b, sgu_w, sgu_b, w_out_even, norm_ffn_even, ffn_wg, ffn_wu, ffn_wd, norm_mix_odd, w_in_odd, pool_w, pool_scale, q_norm_g, w_uq, kv_norm_g, w_ukv, w_out_odd, norm_ffn_odd, router, moe_wg, moe_wu, moe_wd, final_norm):
    p = dict(norm_mix_even=norm_mix_even, w_in_even=w_in_even, lam_q1=lam_q1, lam_k1=lam_k1, lam_q2=lam_q2,
             lam_k2=lam_k2, subln_g=subln_g, sgu_ln_g=sgu_ln_g, sgu_ln_b=sgu_ln_b, sgu_w=sgu_w, sgu_b=sgu_b,
             w_out_even=w_out_even, norm_ffn_even=norm_ffn_even, ffn_wg=ffn_wg, ffn_wu=ffn_wu, ffn_wd=ffn_wd,
             norm_mix_odd=norm_mix_odd, w_in_odd=w_in_odd, pool_w=pool_w, pool_scale=pool_scale,
             q_norm_g=q_norm_g, w_uq=w_uq, kv_norm_g=kv_norm_g, w_ukv=w_ukv, w_out_odd=w_out_odd,
             norm_ffn_odd=norm_ffn_odd, router=router, moe_wg=moe_wg, moe_wu=moe_wu, moe_wd=moe_wd)
    b, s, d = x.shape
    depth = norm_mix_even.shape[0] + norm_mix_odd.shape[0]
    rope_a = _rope_lane_tables(s, A_ROT, 0, A_QK_DIM)
    rope_d = _rope_lane_tables(s, D_ROPE, D_NOPE, LANES)
    final_g = final_norm[None, :]
    outs = []
    for bi in range(b):
        h = x[bi]
        for l in range(depth):
            if l % 2 == 0:
                h, moe_w = _even_layer(h, l, p, l // 2, rope_a)
            else:
                h = _odd_layer(h, p, l // 2, moe_w, rope_d, final_g, final_norm=(l == depth - 1))
        if depth % 2 == 1:
            h = _final_norm(h, final_g)
        outs.append(h)
    return jnp.stack(outs)
```

```python
import functools
import math

import jax
import jax.numpy as jnp
from jax import lax
from jax.experimental import pallas as pl
from jax.experimental.pallas import tpu as pltpu

F32 = jnp.float32
BF16 = jnp.bfloat16

ROPE_THETA = 500000.0
NORM_EPS = 1e-6
LN_EPS = 1e-5
LOG2E = 1.4426950408889634

D_MODEL = 1024
A_HEADS = 4
A_QK_DIM = 64
A_ROT = 16
A_WIDTH = 512
B_WIDTH = 512
B_GROUPS = 4
B_CHUNK = 128
C_WIDTH = 512
C_WINDOWS = (2, 4, 8, 16)
C_HALO = 16
D_HEADS = 8
D_NOPE = 64
D_ROPE = 32
D_V = 64
D_Q_RANK = 384
D_KV_RANK = 256
D_HEAD_PAD = 128
N_EXPERTS = 8

LANES = 128
VT_TAIL = 16
VT_ROWS = LANES + VT_TAIL
VT_ROWS_D = 2 * (D_V + VT_TAIL)
ROW_TILE = 512
ATTN_TILE = 512
MOE_TILE = 512
MOE_F_TILE = 1792
VMEM_LIMIT = 56 * 1024 * 1024


def _cparams(*sem):
    return pltpu.CompilerParams(dimension_semantics=sem, vmem_limit_bytes=VMEM_LIMIT)


def _rms(x, g):
    return x * lax.rsqrt(jnp.mean(x * x, axis=-1, keepdims=True) + NORM_EPS) * g


def _dot(a, b):
    return jnp.dot(a, b, preferred_element_type=F32)


def _rope_block(x, cosf, sins, shift, lo_mask):
    up = pltpu.roll(x, LANES - shift, 1)
    dn = pltpu.roll(x, shift, 1)
    return x * cosf + jnp.where(lo_mask, up, dn) * sins


def _rope_wide(x, cosf, sins, shift, lo_mask):
    nb = x.shape[1] // LANES
    return jnp.concatenate(
        [_rope_block(x[:, b * LANES:(b + 1) * LANES], cosf, sins, shift, lo_mask) for b in range(nb)],
        axis=1)


def _store_vt(vt_ref, v, group):
    rows = v.shape[0]
    vt = v.T
    r = lax.broadcasted_iota(jnp.int32, (VT_TAIL, rows), 0)
    tail = jnp.where(r == 0, 1.0, 0.0).astype(BF16)
    step = group + VT_TAIL
    for b in range(v.shape[1] // group):
        vt_ref[b * step:b * step + group, :] = vt[b * group:(b + 1) * group, :].astype(BF16)
        vt_ref[b * step + group:(b + 1) * step, :] = tail


def _even_in_kernel(h_ref, g_ref, w_ref, cos_ref, sin_ref, lng_ref, lnb_ref, sw_ref, sbt_ref,
                    w0_ref, w1_ref, w2_ref, q_ref, k_ref, v_ref, ob_ref, c0_ref, c1_ref, c2_ref):
    tm = h_ref.shape[0]
    c0_ref[...] = w0_ref[...].astype(BF16)
    c1_ref[...] = w1_ref[...].astype(BF16)
    c2_ref[...] = w2_ref[...].astype(BF16)
    xn = _rms(h_ref[...], g_ref[...]).astype(BF16)
    cosf = cos_ref[...]
    sins = sin_ref[...]
    lane = lax.broadcasted_iota(jnp.int32, (tm, LANES), 1)
    lo_mask = (lane % A_QK_DIM) < (A_ROT // 2)

    q = _dot(xn, w_ref[:, 0:A_WIDTH])
    q = _rope_wide(q, cosf, sins, A_ROT // 2, lo_mask) * (A_QK_DIM ** -0.5 * LOG2E)
    q_ref[...] = q.astype(BF16)
    k = _dot(xn, w_ref[:, A_WIDTH:2 * A_WIDTH])
    k_ref[...] = _rope_wide(k, cosf, sins, A_ROT // 2, lo_mask).astype(BF16)
    _store_vt(v_ref, _dot(xn, w_ref[:, 2 * A_WIDTH:3 * A_WIDTH]), LANES)

    ub = jax.nn.gelu(_dot(xn, w_ref[:, 3 * A_WIDTH:3 * A_WIDTH + B_WIDTH]))
    vb = jax.nn.gelu(_dot(xn, w_ref[:, 3 * A_WIDTH + B_WIDTH:3 * A_WIDTH + 2 * B_WIDTH]))
    mu = jnp.mean(vb, axis=-1, keepdims=True)
    var = jnp.mean(jnp.square(vb - mu), axis=-1, keepdims=True)
    vn = ((vb - mu) * lax.rsqrt(var + LN_EPS) * lng_ref[...] + lnb_ref[...]).astype(BF16)

    row = lax.broadcasted_iota(jnp.int32, (B_CHUNK, B_CHUNK), 0)
    col = lax.broadcasted_iota(jnp.int32, (B_CHUNK, B_CHUNK), 1)
    causal = col <= row
    gc = B_WIDTH // B_GROUPS
    for g in range(B_GROUPS):
        wg = jnp.where(causal, sw_ref[g], 0.0).astype(BF16)
        bias = sbt_ref[:, g:g + 1]
        for c in range(tm // B_CHUNK):
            rs = slice(c * B_CHUNK, (c + 1) * B_CHUNK)
            cs = slice(g * gc, (g + 1) * gc)
            mixed = _dot(wg, vn[rs, cs]) + bias
            ob_ref[rs, cs] = (ub[rs, cs] * mixed).astype(BF16)


def _even_in(h, g, w, cosf, sins, lng, lnb, sgu_w, sgu_bt, cast_ws, cast_layer):
    s = h.shape[0]
    tm = min(ROW_TILE, s)
    steps = s // tm
    n_in = w.shape[1]
    row_spec = lambda width: pl.BlockSpec((tm, width), lambda i: (i, 0))
    full = lambda shape: pl.BlockSpec(shape, lambda i: (0,) * len(shape))
    out = jax.ShapeDtypeStruct((s, A_WIDTH), BF16)
    cast_in = [pl.BlockSpec((None, cw.shape[1] // steps, cw.shape[2]), lambda i: (cast_layer, i, 0))
               for cw in cast_ws]
    cast_out = [pl.BlockSpec((cw.shape[1] // steps, cw.shape[2]), lambda i: (i, 0)) for cw in cast_ws]
    outs = pl.pallas_call(
        _even_in_kernel,
        grid=(steps,),
        in_specs=[row_spec(D_MODEL), full((1, D_MODEL)), full((D_MODEL, n_in)),
                  row_spec(LANES), row_spec(LANES), full((1, B_WIDTH)), full((1, B_WIDTH)),
                  full((B_GROUPS, B_CHUNK, B_CHUNK)), full((B_CHUNK, B_GROUPS))] + cast_in,
        out_specs=[row_spec(A_WIDTH), row_spec(A_WIDTH),
                   pl.BlockSpec((A_HEADS * VT_ROWS, tm), lambda i: (0, i)), row_spec(A_WIDTH)] + cast_out,
        out_shape=[out, out, jax.ShapeDtypeStruct((A_HEADS * VT_ROWS, s), BF16), out]
        + [jax.ShapeDtypeStruct(cw.shape[1:], BF16) for cw in cast_ws],
        compiler_params=_cparams("parallel"),
        name="even_in",
    )(h, g, w, cosf, sins, lng, lnb, sgu_w, sgu_bt, *cast_ws)
    return outs[0], outs[1], outs[2], outs[3], outs[4:]


def _flash_core(qi, t, score_fn, vt_ref, bufs, m_ref, acc_ref, per_half_values):
    m_ref[...] = jnp.full(m_ref.shape, -jnp.inf, F32)
    acc_ref[...] = jnp.zeros(acc_ref.shape, F32)
    n = acc_ref.shape[0]
    s_a, s_b = bufs[0], bufs[1]

    def process(buf, j, diagonal):
        s = buf[0][...]
        if diagonal:
            key = lax.broadcasted_iota(jnp.int32, s.shape, 0)
            qry = lax.broadcasted_iota(jnp.int32, s.shape, 1) % t
            s = jnp.where(key <= qry, s, -jnp.inf)
            tile_max = jnp.max(s, axis=0, keepdims=True)
        else:
            tile_max = buf[1][...]
        m_prev = m_ref[...]
        m_new = jnp.maximum(m_prev, tile_max)
        alpha = jnp.exp2(m_prev - m_new)
        p = jnp.exp2(s - m_new).astype(BF16)
        off = pl.multiple_of(j * t, t)
        if per_half_values:
            for half in range(2):
                cols = slice(half * t, (half + 1) * t)
                acc_ref[:, cols] = (acc_ref[:, cols] * alpha[:, cols]
                                    + _dot(vt_ref[half * n:(half + 1) * n, pl.ds(off, t)], p[:, cols]))
        else:
            acc_ref[...] = acc_ref[...] * alpha + _dot(vt_ref[:, pl.ds(off, t)], p)
        m_ref[...] = m_new

    def score(j, buf):
        score_fn(j, buf[0], buf[1])

    score(0, s_a)

    def pair(j):
        score(j + 1, s_b)
        process(s_a, j, False)
        score(j + 2, s_a)
        process(s_b, j + 1, False)

    def body4(it, carry):
        pair(4 * it)
        pair(4 * it + 2)
        return carry

    def body2(it, carry):
        pair(2 * it)
        return carry

    def body8(it, carry):
        for u in range(4):
            pair(8 * it + 2 * u)
        return carry

    lax.fori_loop(0, qi // 8, body8, 0)
    lax.fori_loop(qi // 8 * 2, qi // 4, body4, 0)
    lax.fori_loop(qi // 4 * 2, qi // 2, body2, 0)

    @pl.when(qi % 2 == 1)
    def _():
        score(qi, s_b)
        process(s_a, qi - 1, False)
        process(s_b, qi, True)

    @pl.when(qi % 2 == 0)
    def _():
        process(s_a, qi, True)


def _attn_scratch(t, acc_rows):
    return [pltpu.VMEM((LANES, 2 * t), BF16), pltpu.VMEM((t, 2 * t), F32), pltpu.VMEM((t, 2 * t), F32),
            pltpu.VMEM((1, 2 * t), F32), pltpu.VMEM((1, 2 * t), F32),
            pltpu.VMEM((1, 2 * t), F32), pltpu.VMEM((acc_rows, 2 * t), F32)]


def _diff_attn_kernel(q_ref, k_ref, vt_ref, lq1_ref, lk1_ref, lq2_ref, lk2_ref, sg_ref, w0_ref, w1_ref, w2_ref,
                      o_ref, c0_ref, c1_ref, c2_ref, qt_ref, s_a, s_b, mx_a, mx_b, m_ref, acc_ref, *, lam_init):
    t = q_ref.shape[0]
    qi = pl.program_id(1)
    c0_ref[...] = w0_ref[...].astype(BF16)
    c1_ref[...] = w1_ref[...].astype(BF16)
    c2_ref[...] = w2_ref[...].astype(BF16)
    qt = q_ref[...].astype(F32).T
    row = lax.broadcasted_iota(jnp.int32, qt.shape, 0)
    qt_ref[:, 0:t] = jnp.where(row < A_QK_DIM, qt, 0.0).astype(BF16)
    qt_ref[:, t:2 * t] = jnp.where(row >= A_QK_DIM, qt, 0.0).astype(BF16)

    def score_fn(j, dst_ref, max_ref):
        off = pl.multiple_of(j * t, t)
        sc = _dot(k_ref[pl.ds(off, t), :], qt_ref[...])
        dst_ref[...] = sc
        max_ref[...] = jnp.max(sc, axis=0, keepdims=True)

    _flash_core(qi, t, score_fn, vt_ref, ((s_a, mx_a), (s_b, mx_b)), m_ref, acc_ref, per_half_values=False)

    lam = (jnp.exp(jnp.sum(lq1_ref[...] * lk1_ref[...], axis=-1, keepdims=True))
           - jnp.exp(jnp.sum(lq2_ref[...] * lk2_ref[...], axis=-1, keepdims=True)) + lam_init)
    o = (acc_ref[0:LANES, 0:t] / acc_ref[LANES:LANES + 1, 0:t]
         - lam * (acc_ref[0:LANES, t:2 * t] / acc_ref[LANES:LANES + 1, t:2 * t]))
    o = o * lax.rsqrt(jnp.mean(o * o, axis=0, keepdims=True) + NORM_EPS) * sg_ref[...] * (1.0 - lam_init)
    o_ref[...] = o.T.astype(BF16)


def _diff_attn(q, k, vt, lq1, lk1, lq2, lk2, subln_g, lam_init, cast_ws, cast_layer):
    s = q.shape[0]
    t = min(ATTN_TILE, s)
    nq = s // t
    hw = 2 * A_QK_DIM
    small = lambda n: pl.BlockSpec((1, n), lambda h, i: (0, 0))
    steps = A_HEADS * nq
    flat = [w.reshape(w.shape[0], w.shape[1] * w.shape[2], w.shape[3]) for w in cast_ws]
    cast_in = [pl.BlockSpec((None, w.shape[1] // steps, w.shape[2]), lambda h, i: (cast_layer, h * nq + i, 0))
               for w in flat]
    cast_out = [pl.BlockSpec((w.shape[1] // steps, w.shape[2]), lambda h, i: (h * nq + i, 0)) for w in flat]
    outs = pl.pallas_call(
        functools.partial(_diff_attn_kernel, lam_init=lam_init),
        grid=(A_HEADS, nq),
        in_specs=[pl.BlockSpec((t, hw), lambda h, i: (i, h)),
                  pl.BlockSpec((s, hw), lambda h, i: (0, h)),
                  pl.BlockSpec((VT_ROWS, s), lambda h, i: (h, 0)),
                  small(A_QK_DIM), small(A_QK_DIM), small(A_QK_DIM), small(A_QK_DIM),
                  pl.BlockSpec((hw, 1), lambda h, i: (0, 0))] + cast_in,
        out_specs=[pl.BlockSpec((t, hw), lambda h, i: (i, h))] + cast_out,
        out_shape=[jax.ShapeDtypeStruct((s, A_WIDTH), BF16)]
        + [jax.ShapeDtypeStruct(w.shape[1:], BF16) for w in flat],
        scratch_shapes=_attn_scratch(t, VT_ROWS),
        compiler_params=_cparams("parallel", "parallel"),
        name="diff_attn",
    )(q, k, vt, lq1, lk1, lq2, lk2, subln_g, *flat)
    return outs[0], [c.reshape(w.shape[1:]) for c, w in zip(outs[1:], cast_ws)]


def _mla_attn_kernel(q_ref, k_ref, vt_ref, o_ref, qt_ref, s_a, s_b, mx_a, mx_b, m_ref, acc_ref):
    t = q_ref.shape[0]
    qi = pl.program_id(1)
    qt_ref[:, 0:t] = q_ref[:, 0:D_HEAD_PAD].astype(F32).T.astype(BF16)
    qt_ref[:, t:2 * t] = q_ref[:, D_HEAD_PAD:2 * D_HEAD_PAD].astype(F32).T.astype(BF16)

    def score_fn(j, dst_ref, max_ref):
        off = pl.multiple_of(j * t, t)
        for half in range(2):
            cols = slice(half * t, (half + 1) * t)
            sc = _dot(k_ref[pl.ds(off, t), half * D_HEAD_PAD:(half + 1) * D_HEAD_PAD], qt_ref[:, cols])
            dst_ref[:, cols] = sc
            max_ref[:, cols] = jnp.max(sc, axis=0, keepdims=True)

    _flash_core(qi, t, score_fn, vt_ref, ((s_a, mx_a), (s_b, mx_b)), m_ref, acc_ref, per_half_values=True)

    oa = acc_ref[0:D_V, 0:t] / acc_ref[D_V:D_V + 1, 0:t]
    ob = acc_ref[0:D_V, t:2 * t] / acc_ref[D_V:D_V + 1, t:2 * t]
    o_ref[...] = jnp.concatenate([oa, ob], axis=0).T.astype(BF16)


def _mla_attn(q, k, vt):
    s = q.shape[0]
    t = min(ATTN_TILE, s)
    return pl.pallas_call(
        _mla_attn_kernel,
        grid=(D_HEADS // 2, s // t),
        in_specs=[pl.BlockSpec((t, 2 * D_HEAD_PAD), lambda h, i: (i, h)),
                  pl.BlockSpec((s, 2 * D_HEAD_PAD), lambda h, i: (0, h)),
                  pl.BlockSpec((VT_ROWS_D, s), lambda h, i: (h, 0))],
        out_specs=pl.BlockSpec((t, 2 * D_V), lambda h, i: (i, h)),
        out_shape=jax.ShapeDtypeStruct((s, D_HEADS * D_V), BF16),
        scratch_shapes=_attn_scratch(t, VT_ROWS_D // 2),
        compiler_params=_cparams("parallel", "parallel"),
        name="mla_attn",
    )(q, k, vt)


def _mixer_out(h_ref, a_ref, b_ref, wo_ref):
    half = a_ref.shape[1]
    return h_ref[...] + _dot(a_ref[...], wo_ref[0:half, :]) + _dot(b_ref[...], wo_ref[half:2 * half, :])


def _dense_ffn_kernel(h_ref, a_ref, b_ref, wo_ref, g_ref, wg_ref, wu_ref, wd_ref, o_ref, xn_ref):
    f = pl.program_id(1)

    @pl.when(f == 0)
    def _():
        x = _mixer_out(h_ref, a_ref, b_ref, wo_ref)
        xn_ref[...] = _rms(x, g_ref[...]).astype(BF16)
        o_ref[...] = x

    xn = xn_ref[...]
    a = jax.nn.silu(_dot(xn, wg_ref[...])) * _dot(xn, wu_ref[...])
    o_ref[...] += _dot(a.astype(BF16), wd_ref[...])


def _dense_ffn(h, a, b, wo, g, wg, wu, wd, f_tile):
    s = h.shape[0]
    tm = min(ROW_TILE, s)
    ffn = wg.shape[1]
    half = a.shape[1]
    once = pl.Buffered(1) if f_tile == ffn else None
    return pl.pallas_call(
        _dense_ffn_kernel,
        grid=(s // tm, ffn // f_tile),
        in_specs=[pl.BlockSpec((tm, D_MODEL), lambda i, f: (i, 0)),
                  pl.BlockSpec((tm, half), lambda i, f: (i, 0)),
                  pl.BlockSpec((tm, half), lambda i, f: (i, 0)),
                  pl.BlockSpec((2 * half, D_MODEL), lambda i, f: (0, 0), pipeline_mode=once),
                  pl.BlockSpec((1, D_MODEL), lambda i, f: (0, 0)),
                  pl.BlockSpec((D_MODEL, f_tile), lambda i, f: (0, f), pipeline_mode=once),
                  pl.BlockSpec((D_MODEL, f_tile), lambda i, f: (0, f), pipeline_mode=once),
                  pl.BlockSpec((f_tile, D_MODEL), lambda i, f: (f, 0), pipeline_mode=once)],
        out_specs=pl.BlockSpec((tm, D_MODEL), lambda i, f: (i, 0)),
        out_shape=jax.ShapeDtypeStruct((s, D_MODEL), F32),
        scratch_shapes=[pltpu.VMEM((tm, D_MODEL), BF16)],
        compiler_params=_cparams("parallel", "arbitrary"),
        name="dense_ffn",
    )(h, a, b, wo, g, wg, wu, wd)


def _row_copy(src, dst, sem):
    return pltpu.make_async_copy(src, dst, sem)


def _expert_ffn_kernel(te_ref, nu_ref, src_ref, x_hbm, wg_ref, wu_ref, wd_ref, o_ref, xbuf_ref, xb_ref, sem,
                       *, n_f):
    r = pl.program_id(0)
    f = pl.program_id(1)
    tm = o_ref.shape[0]
    n_used = nu_ref[0]

    def tile_start(tile, slot):
        for t in range(tm):
            _row_copy(x_hbm.at[pl.ds(src_ref[tile * tm + t], 1)], xbuf_ref.at[slot, pl.ds(t, 1)],
                      sem.at[slot]).start()

    def tile_wait(slot):
        _row_copy(x_hbm.at[pl.ds(0, tm)], xbuf_ref.at[slot], sem.at[slot]).wait()

    @pl.when(r < n_used)
    def _():
        @pl.when(jnp.logical_and(r == 0, f == 0))
        def _():
            def first(t, carry):
                _row_copy(x_hbm.at[pl.ds(src_ref[t], 1)], xbuf_ref.at[0, pl.ds(t, 1)], sem.at[0]).start()
                return carry

            lax.fori_loop(0, tm, first, 0)

        for slot in range(2):
            @pl.when(jnp.logical_and(f == 0, r % 2 == slot))
            def _():
                tile_wait(slot)
                xb_ref[...] = xbuf_ref[slot].astype(BF16)
                o_ref[...] = jnp.zeros(o_ref.shape, F32)
                tile_start(jnp.minimum(r + 1, pl.num_programs(0) - 1), 1 - slot)

        xb = xb_ref[...]
        a = jax.nn.silu(_dot(xb, wg_ref[...])) * _dot(xb, wu_ref[...])
        o_ref[...] += _dot(a.astype(BF16), wd_ref[...])

        for slot in range(2):
            @pl.when(jnp.logical_and(jnp.logical_and(r == n_used - 1, f == n_f - 1), r % 2 == slot))
            def _():
                tile_wait(1 - slot)

    @pl.when(jnp.logical_and(r >= n_used, f == 0))
    def _():
        o_ref[...] = jnp.zeros(o_ref.shape, F32)


def _expert_ffn(tile_expert, n_used, src, xn, wg, wu, wd, f_tile):
    rows = src.shape[0]
    tm = MOE_TILE
    ffn = wg.shape[2]
    grid_spec = pltpu.PrefetchScalarGridSpec(
        num_scalar_prefetch=3,
        grid=(rows // tm, ffn // f_tile),
        in_specs=[pl.BlockSpec(memory_space=pl.ANY),
                  pl.BlockSpec((None, D_MODEL, f_tile),
                               lambda r, f, te, nu, sr: (te[r], 0, jnp.where(r < nu[0], f, 0))),
                  pl.BlockSpec((None, D_MODEL, f_tile),
                               lambda r, f, te, nu, sr: (te[r], 0, jnp.where(r < nu[0], f, 0))),
                  pl.BlockSpec((None, f_tile, D_MODEL),
                               lambda r, f, te, nu, sr: (te[r], jnp.where(r < nu[0], f, 0), 0))],
        out_specs=pl.BlockSpec((tm, D_MODEL), lambda r, f, te, nu, sr: (r, 0)),
        scratch_shapes=[pltpu.VMEM((2, tm, D_MODEL), F32), pltpu.VMEM((tm, D_MODEL), BF16),
                        pltpu.SemaphoreType.DMA((2,))],
    )
    return pl.pallas_call(
        functools.partial(_expert_ffn_kernel, n_f=ffn // f_tile),
        grid_spec=grid_spec,
        out_shape=jax.ShapeDtypeStruct((rows, D_MODEL), F32),
        compiler_params=_cparams("arbitrary", "arbitrary"),
        name="expert_ffn",
    )(tile_expert, n_used, src, xn, wg, wu, wd)


def _odd_in_kernel(h_ref, g_ref, w_ref, wkpe_ref, cos_ref, sin_ref, pw_ref, ps_ref,
                   qg_ref, wuq_ref, kvg_ref, wk_ref, wv_ref,
                   oc_ref, q_ref, k_ref, v_ref, halo_ref):
    tm = h_ref.shape[0]
    i = pl.program_id(0)
    xn = _rms(h_ref[...], g_ref[...]).astype(BF16)
    cosf = cos_ref[...]
    sins = sin_ref[...]
    lane = lax.broadcasted_iota(jnp.int32, (tm, LANES), 1)
    lo_mask = lane < (D_NOPE + D_ROPE // 2)

    @pl.when(i == 0)
    def _():
        halo_ref[...] = jnp.zeros(halo_ref.shape, F32)

    xc = _dot(xn, w_ref[:, 0:C_WIDTH])
    ext = jnp.concatenate([halo_ref[...], xc], axis=0)
    halo_ref[...] = xc[tm - C_HALO:tm, :]
    pos = i * tm + lax.broadcasted_iota(jnp.int32, (tm, 1), 0)
    gc = C_WIDTH // len(C_WINDOWS)
    for g, win in enumerate(C_WINDOWS):
        cs = slice(g * gc, (g + 1) * gc)
        a = ext[:, cs]
        span = 1
        while span < win:
            n = a.shape[0] - span
            a = a[span:span + n, :] + a[0:n, :]
            span *= 2
        start = C_HALO - (win - 1)
        wsum = a[start:start + tm, :]
        cnt = jnp.minimum(pos + 1, win).astype(F32)
        pooled = (wsum / cnt - xc[:, cs]).astype(BF16)
        oc_ref[:, cs] = (_dot(pooled, pw_ref[g]) * ps_ref[:, cs]).astype(BF16)

    cq = _dot(xn, w_ref[:, C_WIDTH:C_WIDTH + D_Q_RANK])
    q = _dot(_rms(cq, qg_ref[...]).astype(BF16), wuq_ref[...])
    q = _rope_wide(q, cosf, sins, D_ROPE // 2, lo_mask) * ((D_NOPE + D_ROPE) ** -0.5 * LOG2E)
    q_ref[...] = q.astype(BF16)

    ckv = _dot(xn, w_ref[:, C_WIDTH + D_Q_RANK:C_WIDTH + D_Q_RANK + D_KV_RANK])
    ckvn = _rms(ckv, kvg_ref[...]).astype(BF16)
    kpe = _rope_block(_dot(xn, wkpe_ref[...]), cosf, sins, D_ROPE // 2, lo_mask)
    kn = _dot(ckvn, wk_ref[...])
    k_ref[...] = jnp.concatenate(
        [kn[:, b * LANES:(b + 1) * LANES] + kpe for b in range(D_HEADS)], axis=1).astype(BF16)
    _store_vt(v_ref, _dot(ckvn, wv_ref[...]), D_V)


def _odd_in(h, g, w, wkpe, cosf, sins, pool_w, pool_scale, qg, wuq, kvg, wk, wv):
    s = h.shape[0]
    tm = min(ROW_TILE, s)
    row_spec = lambda width: pl.BlockSpec((tm, width), lambda i: (i, 0))
    full = lambda shape: pl.BlockSpec(shape, lambda i: (0,) * len(shape))
    return pl.pallas_call(
        _odd_in_kernel,
        grid=(s // tm,),
        in_specs=[row_spec(D_MODEL), full((1, D_MODEL)), full(w.shape), full(wkpe.shape),
                  row_spec(LANES), row_spec(LANES), full(pool_w.shape), full((1, C_WIDTH)),
                  full((1, D_Q_RANK)), full(wuq.shape), full((1, D_KV_RANK)), full(wk.shape), full(wv.shape)],
        out_specs=[row_spec(C_WIDTH), row_spec(D_HEADS * D_HEAD_PAD), row_spec(D_HEADS * D_HEAD_PAD),
                   pl.BlockSpec((D_HEADS // 2 * VT_ROWS_D, tm), lambda i: (0, i))],
        out_shape=[jax.ShapeDtypeStruct((s, C_WIDTH), BF16),
                   jax.ShapeDtypeStruct((s, D_HEADS * D_HEAD_PAD), BF16),
                   jax.ShapeDtypeStruct((s, D_HEADS * D_HEAD_PAD), BF16),
                   jax.ShapeDtypeStruct((D_HEADS // 2 * VT_ROWS_D, s), BF16)],
        scratch_shapes=[pltpu.VMEM((C_HALO, C_WIDTH), F32)],
        compiler_params=_cparams("arbitrary"),
        name="odd_in",
    )(h, g, w, wkpe, cosf, sins, pool_w, pool_scale, qg, wuq, kvg, wk, wv)


def _router_kernel(h_ref, a_ref, b_ref, wo_ref, g_ref, rt_ref,
                   h2_ref, xn_ref, eidx_ref, rank_ref, wcol_ref, cnt_ref, run_ref):
    tm = h_ref.shape[0]
    i = pl.program_id(0)

    @pl.when(i == 0)
    def _():
        run_ref[...] = jnp.zeros(run_ref.shape, F32)

    x = _mixer_out(h_ref, a_ref, b_ref, wo_ref)
    h2_ref[...] = x
    xn = _rms(x, g_ref[...])
    xn_ref[...] = xn
    lg = lax.dot_general(rt_ref[...], xn, (((1,), (1,)), ((), ())),
                         preferred_element_type=F32, precision=lax.Precision.HIGHEST)
    eio = lax.broadcasted_iota(jnp.int32, lg.shape, 0)
    m1 = jnp.max(lg, axis=0, keepdims=True)
    i1 = jnp.min(jnp.where(lg == m1, eio, N_EXPERTS), axis=0, keepdims=True)
    lg2 = jnp.where(eio == i1, -jnp.inf, lg)
    m2 = jnp.max(lg2, axis=0, keepdims=True)
    i2 = jnp.min(jnp.where(lg2 == m2, eio, N_EXPERTS), axis=0, keepdims=True)
    e = jnp.exp(m2 - m1)
    w1 = 1.0 / (1.0 + e)
    w2 = e / (1.0 + e)

    sel1 = eio == i1
    sel2 = eio == i2
    onehot = jnp.logical_or(sel1, sel2)
    tr = lax.broadcasted_iota(jnp.int32, (tm, tm), 0)
    tc = lax.broadcasted_iota(jnp.int32, (tm, tm), 1)
    upper = (tr < tc).astype(BF16)
    before = _dot(onehot.astype(BF16), upper) + run_ref[:, 0:1]
    r1 = jnp.sum(jnp.where(sel1, before, 0.0), axis=0, keepdims=True)
    r2 = jnp.sum(jnp.where(sel2, before, 0.0), axis=0, keepdims=True)
    run_new = run_ref[:, 0:1] + jnp.sum(onehot.astype(F32), axis=1, keepdims=True)
    run_ref[...] = jnp.broadcast_to(run_new, run_ref.shape)
    cnt_ref[...] = jnp.broadcast_to(run_new, cnt_ref.shape).astype(jnp.int32)

    eidx_ref[...] = jnp.concatenate([i1, i2], axis=0)
    rank_ref[...] = jnp.concatenate([r1, r2], axis=0).astype(jnp.int32)
    w8 = jnp.concatenate([w1, w2, jnp.zeros((N_EXPERTS - 2, tm), F32)], axis=0)
    wcol_ref[...] = w8.T


def _router(h, a, b, wo, g, router_t):
    s = h.shape[0]
    tm = min(ROW_TILE, s)
    half = a.shape[1]
    return pl.pallas_call(
        _router_kernel,
        grid=(s // tm,),
        in_specs=[pl.BlockSpec((tm, D_MODEL), lambda i: (i, 0)),
                  pl.BlockSpec((tm, half), lambda i: (i, 0)),
                  pl.BlockSpec((tm, half), lambda i: (i, 0)),
                  pl.BlockSpec((2 * half, D_MODEL), lambda i: (0, 0)),
                  pl.BlockSpec((1, D_MODEL), lambda i: (0, 0)),
                  pl.BlockSpec((N_EXPERTS, D_MODEL), lambda i: (0, 0))],
        out_specs=[pl.BlockSpec((tm, D_MODEL), lambda i: (i, 0)),
                   pl.BlockSpec((tm, D_MODEL), lambda i: (i, 0)),
                   pl.BlockSpec((2, tm), lambda i: (0, i)),
                   pl.BlockSpec((2, tm), lambda i: (0, i)),
                   pl.BlockSpec((tm, N_EXPERTS), lambda i: (i, 0)),
                   pl.BlockSpec((N_EXPERTS, LANES), lambda i: (0, 0))],
        out_shape=[jax.ShapeDtypeStruct((s, D_MODEL), F32),
                   jax.ShapeDtypeStruct((s, D_MODEL), F32),
                   jax.ShapeDtypeStruct((2, s), jnp.int32),
                   jax.ShapeDtypeStruct((2, s), jnp.int32),
                   jax.ShapeDtypeStruct((s, N_EXPERTS), F32),
                   jax.ShapeDtypeStruct((N_EXPERTS, LANES), jnp.int32)],
        scratch_shapes=[pltpu.VMEM((N_EXPERTS, LANES), F32)],
        compiler_params=_cparams("arbitrary"),
        name="router",
    )(h, a, b, wo, g, router_t)


def _invert_kernel(gap_ref, pos_ref, src_ref):
    n_tok = pos_ref.shape[0] // 2

    def clear(i, carry):
        src_ref[i] = 0
        return carry

    for g in range(gap_ref.shape[0] // 2):
        lax.fori_loop(gap_ref[2 * g], gap_ref[2 * g + 1], clear, 0)

    def put(t, carry):
        src_ref[pos_ref[t]] = t
        src_ref[pos_ref[n_tok + t]] = t
        return carry

    lax.fori_loop(0, n_tok, put, 0, unroll=8)


def _invert(gaps, pos_flat, n_slots):
    return pl.pallas_call(
        _invert_kernel,
        in_specs=[pl.BlockSpec(memory_space=pltpu.SMEM), pl.BlockSpec(memory_space=pltpu.SMEM)],
        out_specs=pl.BlockSpec(memory_space=pltpu.SMEM),
        out_shape=jax.ShapeDtypeStruct((n_slots,), jnp.int32),
        name="invert",
    )(gaps, pos_flat)


def _combine_kernel(pos_ref, nxt_ref, ys_hbm, h_ref, wcol_ref, fg_ref, o_ref, buf_ref, sem, *, final_norm):
    i = pl.program_id(0)
    tm = h_ref.shape[0]

    def tile_start(idx_ref, slot):
        for t in range(tm):
            for k in range(2):
                _row_copy(ys_hbm.at[pl.ds(idx_ref[k, t], 1)], buf_ref.at[slot, k, pl.ds(t, 1)],
                          sem.at[slot]).start()

    @pl.when(i == 0)
    def _():
        tile_start(pos_ref, 0)

    for slot in range(2):
        @pl.when(jnp.logical_and(i % 2 == slot, i + 1 < pl.num_programs(0)))
        def _():
            tile_start(nxt_ref, 1 - slot)

    for slot in range(2):
        @pl.when(i % 2 == slot)
        def _():
            for k in range(2):
                _row_copy(ys_hbm.at[pl.ds(0, tm)], buf_ref.at[slot, k], sem.at[slot]).wait()
            out = h_ref[...] + wcol_ref[:, 0:1] * buf_ref[slot, 0] + wcol_ref[:, 1:2] * buf_ref[slot, 1]
            if final_norm:
                out = _rms(out, fg_ref[...])
            o_ref[...] = out


def _combine(pos, ys, h, wcol, final_g, final_norm):
    s = h.shape[0]
    tm = min(ROW_TILE, s)
    last = s // tm - 1
    return pl.pallas_call(
        functools.partial(_combine_kernel, final_norm=final_norm),
        grid=(s // tm,),
        in_specs=[pl.BlockSpec((2, tm), lambda i: (0, i), memory_space=pltpu.SMEM),
                  pl.BlockSpec((2, tm), lambda i: (0, jnp.minimum(i + 1, last)), memory_space=pltpu.SMEM),
                  pl.BlockSpec(memory_space=pl.ANY),
                  pl.BlockSpec((tm, D_MODEL), lambda i: (i, 0)),
                  pl.BlockSpec((tm, N_EXPERTS), lambda i: (i, 0)),
                  pl.BlockSpec((1, D_MODEL), lambda i: (0, 0))],
        out_specs=pl.BlockSpec((tm, D_MODEL), lambda i: (i, 0)),
        out_shape=jax.ShapeDtypeStruct((s, D_MODEL), F32),
        scratch_shapes=[pltpu.VMEM((2, 2, tm, D_MODEL), F32), pltpu.SemaphoreType.DMA((2,))],
        compiler_params=_cparams("arbitrary"),
        name="combine",
    )(pos, pos, ys, h, wcol, final_g)


def _final_norm_kernel(h_ref, g_ref, o_ref):
    o_ref[...] = _rms(h_ref[...], g_ref[...])


def _final_norm(h, g):
    s = h.shape[0]
    tm = min(ROW_TILE, s)
    return pl.pallas_call(
        _final_norm_kernel,
        grid=(s // tm,),
        in_specs=[pl.BlockSpec((tm, D_MODEL), lambda i: (i, 0)), pl.BlockSpec((1, D_MODEL), lambda i: (0, 0))],
        out_specs=pl.BlockSpec((tm, D_MODEL), lambda i: (i, 0)),
        out_shape=jax.ShapeDtypeStruct((s, D_MODEL), F32),
        compiler_params=_cparams("parallel"),
        name="final_norm",
    )(h, g)


def _rope_lane_tables(seq, rot_dim, first_lane, period):
    half = rot_dim // 2
    pos = jnp.arange(seq, dtype=F32)
    inv = ROPE_THETA ** (-jnp.arange(0, rot_dim, 2, dtype=F32) / rot_dim)
    ang = pos[:, None] * inv[None, :]
    cos, sin = jnp.cos(ang), jnp.sin(ang)
    lane = jnp.arange(LANES) % period - first_lane
    lo = (lane >= 0) & (lane < half)
    hi = (lane >= half) & (lane < rot_dim)
    idx = jnp.clip(jnp.where(hi, lane - half, lane), 0, half - 1)
    cosf = jnp.where((lo | hi)[None, :], cos[:, idx], 1.0)
    sins = jnp.where(lo[None, :], -sin[:, idx], jnp.where(hi[None, :], sin[:, idx], 0.0))
    return cosf, sins


def _even_layer(h, l, p, i, rope_a):
    w_in = p["w_in_even"][i].astype(BF16)
    q, k, v, ob, ffn_w = _even_in(
        h, p["norm_mix_even"][i][None, :], w_in, rope_a[0], rope_a[1],
        p["sgu_ln_g"][i][None, :], p["sgu_ln_b"][i][None, :], p["sgu_w"][i], p["sgu_b"][i].T,
        (p["ffn_wg"], p["ffn_wu"], p["ffn_wd"]), i)
    lam_init = 0.8 - 0.6 * math.exp(-0.3 * l)
    oa, moe_w = _diff_attn(q, k, v, p["lam_q1"][i][None, :], p["lam_k1"][i][None, :],
                           p["lam_q2"][i][None, :], p["lam_k2"][i][None, :], p["subln_g"][i][:, None], lam_init,
                           (p["moe_wg"], p["moe_wu"], p["moe_wd"]), min(i, p["moe_wg"].shape[0] - 1))
    ffn = p["ffn_wg"].shape[2]
    h = _dense_ffn(h, oa, ob, p["w_out_even"][i].astype(BF16), p["norm_ffn_even"][i][None, :],
                   ffn_w[0], ffn_w[1], ffn_w[2], ffn)
    return h, moe_w


def _odd_layer(h, p, i, moe_w, rope_d, final_g, final_norm):
    s = h.shape[0]
    w_in = p["w_in_odd"][i]
    n_main = C_WIDTH + D_Q_RANK + D_KV_RANK
    w_main = w_in[:, :n_main].astype(BF16)
    wkpe = jnp.zeros((D_MODEL, LANES), F32).at[:, D_NOPE:D_NOPE + D_ROPE].set(w_in[:, n_main:]).astype(BF16)
    wuq = jnp.pad(p["w_uq"][i].reshape(D_Q_RANK, D_HEADS, D_NOPE + D_ROPE),
                  ((0, 0), (0, 0), (0, D_HEAD_PAD - D_NOPE - D_ROPE))).reshape(D_Q_RANK, -1).astype(BF16)
    wukv = p["w_ukv"][i].reshape(D_KV_RANK, D_HEADS, D_NOPE + D_V)
    wk = jnp.pad(wukv[:, :, :D_NOPE], ((0, 0), (0, 0), (0, D_HEAD_PAD - D_NOPE))).reshape(D_KV_RANK, -1).astype(BF16)
    wv = wukv[:, :, D_NOPE:].reshape(D_KV_RANK, -1).astype(BF16)
    oc, q, k, v = _odd_in(
        h, p["norm_mix_odd"][i][None, :], w_main, wkpe, rope_d[0], rope_d[1],
        p["pool_w"][i].astype(BF16), p["pool_scale"][i][None, :],
        p["q_norm_g"][i][None, :], wuq, p["kv_norm_g"][i][None, :], wk, wv)
    od = _mla_attn(q, k, v)

    h, xn, eidx, rank, wcol, cnt = _router(h, oc, od, p["w_out_odd"][i].astype(BF16),
                                           p["norm_ffn_odd"][i][None, :], p["router"][i].T)
    counts = cnt[:, 0]
    padded = (counts + MOE_TILE - 1) // MOE_TILE * MOE_TILE
    ends = jnp.cumsum(padded)
    base = ends - padded
    pos = rank
    for e in range(N_EXPERTS):
        pos = pos + jnp.where(eidx == e, base[e], 0)
    n_tiles = (2 * s) // MOE_TILE + N_EXPERTS
    tile_start = jnp.arange(n_tiles, dtype=jnp.int32) * MOE_TILE
    tile_expert = jnp.minimum(jnp.sum(tile_start[:, None] >= ends[None, :], axis=1), N_EXPERTS - 1).astype(jnp.int32)
    n_used = (ends[-1:] // MOE_TILE).astype(jnp.int32)
    n_slots = n_tiles * MOE_TILE
    gaps = jnp.stack([jnp.append(base + counts, ends[-1]), jnp.append(ends, n_slots)], axis=1)
    src = _invert(gaps.reshape(-1).astype(jnp.int32), pos.reshape(-1), n_slots)
    ys = _expert_ffn(tile_expert, n_used, src, xn, moe_w[0], moe_w[1], moe_w[2], MOE_F_TILE)
    return _combine(pos, ys, h, wcol, final_g, final_norm)


def kernel(x, norm_mix_even, w_in_even, lam_q1, lam_k1, lam_q2, lam_k2, subln_g, sgu_ln_g, sgu_ln_b, sgu_w, sgu_b, w_out_even, norm_ffn_even, ffn_wg, ffn_wu, ffn_wd, norm_mix_odd, w_in_odd, pool_w, pool_scale, q_norm_g, w_uq, kv_norm_g, w_ukv, w_out_odd, norm_ffn_odd, router, moe_wg, moe_wu, moe_wd, final_norm):
    p = dict(norm_mix_even=norm_mix_even, w_in_even=w_in_even, lam_q1=lam_q1, lam_k1=lam_k1, lam_q2=lam_q2,
             lam_k2=lam_k2, subln_g=subln_g, sgu_ln_g=sgu_ln_g, sgu_ln_b=sgu_ln_b, sgu_w=sgu_w, sgu_b=sgu_b,
             w_out_even=w_out_even, norm_ffn_even=norm_ffn_even, ffn_wg=ffn_wg, ffn_wu=ffn_wu, ffn_wd=ffn_wd,
             norm_mix_odd=norm_mix_odd, w_in_odd=w_in_odd, pool_w=pool_w, pool_scale=pool_scale,
             q_norm_g=q_norm_g, w_uq=w_uq, kv_norm_g=kv_norm_g, w_ukv=w_ukv, w_out_odd=w_out_odd,
             norm_ffn_odd=norm_ffn_odd, router=router, moe_wg=moe_wg, moe_wu=moe_wu, moe_wd=moe_wd)
    b, s, d = x.shape
    depth = norm_mix_even.shape[0] + norm_mix_odd.shape[0]
    rope_a = _rope_lane_tables(s, A_ROT, 0, A_QK_DIM)
    rope_d = _rope_lane_tables(s, D_ROPE, D_NOPE, LANES)
    final_g = final_norm[None, :]
    outs = []
    for bi in range(b):
        h = x[bi]
        for l in range(depth):
            if l % 2 == 0:
                h, moe_w = _even_layer(h, l, p, l // 2, rope_a)
            else:
                h = _odd_layer(h, p, l // 2, moe_w, rope_d, final_g, final_norm=(l == depth - 1))
        if depth % 2 == 1:
            h = _final_norm(h, final_g)
        outs.append(h)
    return jnp.stack(outs)
```

```python
import functools
import math

import jax
import jax.numpy as jnp
from jax import lax
from jax.experimental import pallas as pl
from jax.experimental.pallas import tpu as pltpu

F32 = jnp.float32
BF16 = jnp.bfloat16

ROPE_THETA = 500000.0
NORM_EPS = 1e-6
LN_EPS = 1e-5
LOG2E = 1.4426950408889634

D_MODEL = 1024
A_HEADS = 4
A_QK_DIM = 64
A_ROT = 16
A_WIDTH = 512
B_WIDTH = 512
B_GROUPS = 4
B_CHUNK = 128
C_WIDTH = 512
C_WINDOWS = (2, 4, 8, 16)
C_HALO = 16
D_HEADS = 8
D_NOPE = 64
D_ROPE = 32
D_V = 64
D_Q_RANK = 384
D_KV_RANK = 256
D_HEAD_PAD = 128
N_EXPERTS = 8

LANES = 128
VT_TAIL = 16
VT_ROWS = LANES + VT_TAIL
VT_ROWS_D = 2 * (D_V + VT_TAIL)
ROW_TILE = 512
ATTN_TILE = 512
MOE_TILE = 512
MOE_F_TILE = 1792
FFN_COLS = 256
VMEM_LIMIT = 56 * 1024 * 1024


def _cparams(*sem):
    return pltpu.CompilerParams(dimension_semantics=sem, vmem_limit_bytes=VMEM_LIMIT)


def _rms(x, g):
    return x * lax.rsqrt(jnp.mean(x * x, axis=-1, keepdims=True) + NORM_EPS) * g


def _dot(a, b):
    return jnp.dot(a, b, preferred_element_type=F32)


def _rope_block(x, cosf, sins, shift, lo_mask):
    up = pltpu.roll(x, LANES - shift, 1)
    dn = pltpu.roll(x, shift, 1)
    return x * cosf + jnp.where(lo_mask, up, dn) * sins


def _rope_wide(x, cosf, sins, shift, lo_mask):
    nb = x.shape[1] // LANES
    return jnp.concatenate(
        [_rope_block(x[:, b * LANES:(b + 1) * LANES], cosf, sins, shift, lo_mask) for b in range(nb)],
        axis=1)


def _store_vt(vt_ref, v, group):
    rows = v.shape[0]
    vt = v.T
    r = lax.broadcasted_iota(jnp.int32, (VT_TAIL, rows), 0)
    tail = jnp.where(r == 0, 1.0, 0.0).astype(BF16)
    step = group + VT_TAIL
    for b in range(v.shape[1] // group):
        vt_ref[b * step:b * step + group, :] = vt[b * group:(b + 1) * group, :].astype(BF16)
        vt_ref[b * step + group:(b + 1) * step, :] = tail


def _even_in_kernel(h_ref, g_ref, w_ref, cos_ref, sin_ref, lng_ref, lnb_ref, sw_ref, sbt_ref,
                    w0_ref, w1_ref, w2_ref, q_ref, k_ref, v_ref, ob_ref, c0_ref, c1_ref, c2_ref):
    tm = h_ref.shape[0]
    c0_ref[...] = w0_ref[...].astype(BF16)
    c1_ref[...] = w1_ref[...].astype(BF16)
    c2_ref[...] = w2_ref[...].astype(BF16)
    xn = _rms(h_ref[...], g_ref[...]).astype(BF16)
    cosf = cos_ref[...]
    sins = sin_ref[...]
    lane = lax.broadcasted_iota(jnp.int32, (tm, LANES), 1)
    lo_mask = (lane % A_QK_DIM) < (A_ROT // 2)

    q = _dot(xn, w_ref[:, 0:A_WIDTH])
    q = _rope_wide(q, cosf, sins, A_ROT // 2, lo_mask) * (A_QK_DIM ** -0.5 * LOG2E)
    q_ref[...] = q.astype(BF16)
    k = _dot(xn, w_ref[:, A_WIDTH:2 * A_WIDTH])
    k_ref[...] = _rope_wide(k, cosf, sins, A_ROT // 2, lo_mask).astype(BF16)
    _store_vt(v_ref, _dot(xn, w_ref[:, 2 * A_WIDTH:3 * A_WIDTH]), LANES)

    ub = jax.nn.gelu(_dot(xn, w_ref[:, 3 * A_WIDTH:3 * A_WIDTH + B_WIDTH]))
    vb = jax.nn.gelu(_dot(xn, w_ref[:, 3 * A_WIDTH + B_WIDTH:3 * A_WIDTH + 2 * B_WIDTH]))
    mu = jnp.mean(vb, axis=-1, keepdims=True)
    var = jnp.mean(jnp.square(vb - mu), axis=-1, keepdims=True)
    vn = ((vb - mu) * lax.rsqrt(var + LN_EPS) * lng_ref[...] + lnb_ref[...]).astype(BF16)

    row = lax.broadcasted_iota(jnp.int32, (B_CHUNK, B_CHUNK), 0)
    col = lax.broadcasted_iota(jnp.int32, (B_CHUNK, B_CHUNK), 1)
    causal = col <= row
    gc = B_WIDTH // B_GROUPS
    for g in range(B_GROUPS):
        wg = jnp.where(causal, sw_ref[g], 0.0).astype(BF16)
        bias = sbt_ref[:, g:g + 1]
        for c in range(tm // B_CHUNK):
            rs = slice(c * B_CHUNK, (c + 1) * B_CHUNK)
            cs = slice(g * gc, (g + 1) * gc)
            mixed = _dot(wg, vn[rs, cs]) + bias
            ob_ref[rs, cs] = (ub[rs, cs] * mixed).astype(BF16)


def _even_in(h, g, w, cosf, sins, lng, lnb, sgu_w, sgu_bt, cast_ws, cast_layer):
    s = h.shape[0]
    tm = min(ROW_TILE, s)
    steps = s // tm
    n_in = w.shape[1]
    row_spec = lambda width: pl.BlockSpec((tm, width), lambda i: (i, 0))
    full = lambda shape: pl.BlockSpec(shape, lambda i: (0,) * len(shape))
    out = jax.ShapeDtypeStruct((s, A_WIDTH), BF16)
    cast_in = [pl.BlockSpec((None, cw.shape[1] // steps, cw.shape[2]), lambda i: (cast_layer, i, 0))
               for cw in cast_ws]
    cast_out = [pl.BlockSpec((cw.shape[1] // steps, cw.shape[2]), lambda i: (i, 0)) for cw in cast_ws]
    outs = pl.pallas_call(
        _even_in_kernel,
        grid=(steps,),
        in_specs=[row_spec(D_MODEL), full((1, D_MODEL)), full((D_MODEL, n_in)),
                  row_spec(LANES), row_spec(LANES), full((1, B_WIDTH)), full((1, B_WIDTH)),
                  full((B_GROUPS, B_CHUNK, B_CHUNK)), full((B_CHUNK, B_GROUPS))] + cast_in,
        out_specs=[row_spec(A_WIDTH), row_spec(A_WIDTH),
                   pl.BlockSpec((A_HEADS * VT_ROWS, tm), lambda i: (0, i)), row_spec(A_WIDTH)] + cast_out,
        out_shape=[out, out, jax.ShapeDtypeStruct((A_HEADS * VT_ROWS, s), BF16), out]
        + [jax.ShapeDtypeStruct(cw.shape[1:], BF16) for cw in cast_ws],
        compiler_params=_cparams("parallel"),
        name="even_in",
    )(h, g, w, cosf, sins, lng, lnb, sgu_w, sgu_bt, *cast_ws)
    return outs[0], outs[1], outs[2], outs[3], outs[4:]


def _flash_core(qi, t, score_fn, vt_ref, bufs, m_ref, acc_ref, per_half_values):
    m_ref[...] = jnp.full(m_ref.shape, -jnp.inf, F32)
    acc_ref[...] = jnp.zeros(acc_ref.shape, F32)
    n = acc_ref.shape[0]
    s_a, s_b = bufs[0], bufs[1]

    def process(buf, j, diagonal):
        s = buf[0][...]
        if diagonal:
            key = lax.broadcasted_iota(jnp.int32, s.shape, 0)
            qry = lax.broadcasted_iota(jnp.int32, s.shape, 1) % t
            s = jnp.where(key <= qry, s, -jnp.inf)
            tile_max = jnp.max(s, axis=0, keepdims=True)
        else:
            tile_max = buf[1][...]
        m_prev = m_ref[...]
        m_new = jnp.maximum(m_prev, tile_max)
        alpha = jnp.exp2(m_prev - m_new)
        p = jnp.exp2(s - m_new).astype(BF16)
        off = pl.multiple_of(j * t, t)
        if per_half_values:
            for half in range(2):
                cols = slice(half * t, (half + 1) * t)
                acc_ref[:, cols] = (acc_ref[:, cols] * alpha[:, cols]
                                    + _dot(vt_ref[half * n:(half + 1) * n, pl.ds(off, t)], p[:, cols]))
        else:
            acc_ref[...] = acc_ref[...] * alpha + _dot(vt_ref[:, pl.ds(off, t)], p)
        m_ref[...] = m_new

    def score(j, buf):
        score_fn(j, buf[0], buf[1])

    score(0, s_a)

    def pair(j):
        score(j + 1, s_b)
        process(s_a, j, False)
        score(j + 2, s_a)
        process(s_b, j + 1, False)

    def body4(it, carry):
        pair(4 * it)
        pair(4 * it + 2)
        return carry

    def body2(it, carry):
        pair(2 * it)
        return carry

    def body8(it, carry):
        for u in range(4):
            pair(8 * it + 2 * u)
        return carry

    lax.fori_loop(0, qi // 8, body8, 0)
    lax.fori_loop(qi // 8 * 2, qi // 4, body4, 0)
    lax.fori_loop(qi // 4 * 2, qi // 2, body2, 0)

    @pl.when(qi % 2 == 1)
    def _():
        score(qi, s_b)
        process(s_a, qi - 1, False)
        process(s_b, qi, True)

    @pl.when(qi % 2 == 0)
    def _():
        process(s_a, qi, True)


def _attn_scratch(t, acc_rows):
    return [pltpu.VMEM((LANES, 2 * t), BF16), pltpu.VMEM((t, 2 * t), F32), pltpu.VMEM((t, 2 * t), F32),
            pltpu.VMEM((1, 2 * t), F32), pltpu.VMEM((1, 2 * t), F32),
            pltpu.VMEM((1, 2 * t), F32), pltpu.VMEM((acc_rows, 2 * t), F32)]


def _diff_attn_kernel(q_ref, k_ref, vt_ref, lq1_ref, lk1_ref, lq2_ref, lk2_ref, sg_ref, w0_ref, w1_ref, w2_ref,
                      o_ref, c0_ref, c1_ref, c2_ref, qt_ref, s_a, s_b, mx_a, mx_b, m_ref, acc_ref, *, lam_init):
    t = q_ref.shape[0]
    qi = pl.program_id(1)
    c0_ref[...] = w0_ref[...].astype(BF16)
    c1_ref[...] = w1_ref[...].astype(BF16)
    c2_ref[...] = w2_ref[...].astype(BF16)
    qt = q_ref[...].astype(F32).T
    row = lax.broadcasted_iota(jnp.int32, qt.shape, 0)
    qt_ref[:, 0:t] = jnp.where(row < A_QK_DIM, qt, 0.0).astype(BF16)
    qt_ref[:, t:2 * t] = jnp.where(row >= A_QK_DIM, qt, 0.0).astype(BF16)

    def score_fn(j, dst_ref, max_ref):
        off = pl.multiple_of(j * t, t)
        sc = _dot(k_ref[pl.ds(off, t), :], qt_ref[...])
        dst_ref[...] = sc
        max_ref[...] = jnp.max(sc, axis=0, keepdims=True)

    _flash_core(qi, t, score_fn, vt_ref, ((s_a, mx_a), (s_b, mx_b)), m_ref, acc_ref, per_half_values=False)

    lam = (jnp.exp(jnp.sum(lq1_ref[...] * lk1_ref[...], axis=-1, keepdims=True))
           - jnp.exp(jnp.sum(lq2_ref[...] * lk2_ref[...], axis=-1, keepdims=True)) + lam_init)
    o = (acc_ref[0:LANES, 0:t] / acc_ref[LANES:LANES + 1, 0:t]
         - lam * (acc_ref[0:LANES, t:2 * t] / acc_ref[LANES:LANES + 1, t:2 * t]))
    o = o * lax.rsqrt(jnp.mean(o * o, axis=0, keepdims=True) + NORM_EPS) * sg_ref[...] * (1.0 - lam_init)
    o_ref[...] = o.T.astype(BF16)


def _diff_attn(q, k, vt, lq1, lk1, lq2, lk2, subln_g, lam_init, cast_ws, cast_layer):
    s = q.shape[0]
    t = min(ATTN_TILE, s)
    nq = s // t
    hw = 2 * A_QK_DIM
    small = lambda n: pl.BlockSpec((1, n), lambda h, i: (0, 0))
    steps = A_HEADS * nq
    flat = [w.reshape(w.shape[0], w.shape[1] * w.shape[2], w.shape[3]) for w in cast_ws]
    cast_in = [pl.BlockSpec((None, w.shape[1] // steps, w.shape[2]), lambda h, i: (cast_layer, h * nq + i, 0))
               for w in flat]
    cast_out = [pl.BlockSpec((w.shape[1] // steps, w.shape[2]), lambda h, i: (h * nq + i, 0)) for w in flat]
    outs = pl.pallas_call(
        functools.partial(_diff_attn_kernel, lam_init=lam_init),
        grid=(A_HEADS, nq),
        in_specs=[pl.BlockSpec((t, hw), lambda h, i: (i, h)),
                  pl.BlockSpec((s, hw), lambda h, i: (0, h)),
                  pl.BlockSpec((VT_ROWS, s), lambda h, i: (h, 0)),
                  small(A_QK_DIM), small(A_QK_DIM), small(A_QK_DIM), small(A_QK_DIM),
                  pl.BlockSpec((hw, 1), lambda h, i: (0, 0))] + cast_in,
        out_specs=[pl.BlockSpec((t, hw), lambda h, i: (i, h))] + cast_out,
        out_shape=[jax.ShapeDtypeStruct((s, A_WIDTH), BF16)]
        + [jax.ShapeDtypeStruct(w.shape[1:], BF16) for w in flat],
        scratch_shapes=_attn_scratch(t, VT_ROWS),
        compiler_params=_cparams("parallel", "parallel"),
        name="diff_attn",
    )(q, k, vt, lq1, lk1, lq2, lk2, subln_g, *flat)
    return outs[0], [c.reshape(w.shape[1:]) for c, w in zip(outs[1:], cast_ws)]


def _mla_attn_kernel(q_ref, k_ref, vt_ref, o_ref, qt_ref, s_a, s_b, mx_a, mx_b, m_ref, acc_ref):
    t = q_ref.shape[0]
    qi = pl.program_id(1)
    qt_ref[:, 0:t] = q_ref[:, 0:D_HEAD_PAD].astype(F32).T.astype(BF16)
    qt_ref[:, t:2 * t] = q_ref[:, D_HEAD_PAD:2 * D_HEAD_PAD].astype(F32).T.astype(BF16)

    def score_fn(j, dst_ref, max_ref):
        off = pl.multiple_of(j * t, t)
        for half in range(2):
            cols = slice(half * t, (half + 1) * t)
            sc = _dot(k_ref[pl.ds(off, t), half * D_HEAD_PAD:(half + 1) * D_HEAD_PAD], qt_ref[:, cols])
            dst_ref[:, cols] = sc
            max_ref[:, cols] = jnp.max(sc, axis=0, keepdims=True)

    _flash_core(qi, t, score_fn, vt_ref, ((s_a, mx_a), (s_b, mx_b)), m_ref, acc_ref, per_half_values=True)

    oa = acc_ref[0:D_V, 0:t] / acc_ref[D_V:D_V + 1, 0:t]
    ob = acc_ref[0:D_V, t:2 * t] / acc_ref[D_V:D_V + 1, t:2 * t]
    o_ref[...] = jnp.concatenate([oa, ob], axis=0).T.astype(BF16)


def _mla_attn(q, k, vt):
    s = q.shape[0]
    t = min(ATTN_TILE, s)
    return pl.pallas_call(
        _mla_attn_kernel,
        grid=(D_HEADS // 2, s // t),
        in_specs=[pl.BlockSpec((t, 2 * D_HEAD_PAD), lambda h, i: (i, h)),
                  pl.BlockSpec((s, 2 * D_HEAD_PAD), lambda h, i: (0, h)),
                  pl.BlockSpec((VT_ROWS_D, s), lambda h, i: (h, 0))],
        out_specs=pl.BlockSpec((t, 2 * D_V), lambda h, i: (i, h)),
        out_shape=jax.ShapeDtypeStruct((s, D_HEADS * D_V), BF16),
        scratch_shapes=_attn_scratch(t, VT_ROWS_D // 2),
        compiler_params=_cparams("parallel", "parallel"),
        name="mla_attn",
    )(q, k, vt)


def _mixer_out(h_ref, a_ref, b_ref, wo_ref):
    half = a_ref.shape[1]
    return h_ref[...] + _dot(a_ref[...], wo_ref[0:half, :]) + _dot(b_ref[...], wo_ref[half:2 * half, :])


def _dense_ffn_kernel(h_ref, a_ref, b_ref, wo_ref, g_ref, wg_ref, wu_ref, wd_ref, o_ref, xn_ref):
    f = pl.program_id(1)

    @pl.when(f == 0)
    def _():
        x = _mixer_out(h_ref, a_ref, b_ref, wo_ref)
        xn_ref[...] = _rms(x, g_ref[...]).astype(BF16)
        o_ref[...] = x

    xn = xn_ref[...]
    a = jax.nn.silu(_dot(xn, wg_ref[...])) * _dot(xn, wu_ref[...])
    o_ref[...] += _dot(a.astype(BF16), wd_ref[...])


def _dense_ffn(h, a, b, wo, g, wg, wu, wd, f_tile):
    s = h.shape[0]
    tm = min(ROW_TILE, s)
    ffn = wg.shape[1]
    half = a.shape[1]
    once = pl.Buffered(1) if f_tile == ffn else None
    return pl.pallas_call(
        _dense_ffn_kernel,
        grid=(s // tm, ffn // f_tile),
        in_specs=[pl.BlockSpec((tm, D_MODEL), lambda i, f: (i, 0)),
                  pl.BlockSpec((tm, half), lambda i, f: (i, 0)),
                  pl.BlockSpec((tm, half), lambda i, f: (i, 0)),
                  pl.BlockSpec((2 * half, D_MODEL), lambda i, f: (0, 0), pipeline_mode=once),
                  pl.BlockSpec((1, D_MODEL), lambda i, f: (0, 0)),
                  pl.BlockSpec((D_MODEL, f_tile), lambda i, f: (0, f), pipeline_mode=once),
                  pl.BlockSpec((D_MODEL, f_tile), lambda i, f: (0, f), pipeline_mode=once),
                  pl.BlockSpec((f_tile, D_MODEL), lambda i, f: (f, 0), pipeline_mode=once)],
        out_specs=pl.BlockSpec((tm, D_MODEL), lambda i, f: (i, 0)),
        out_shape=jax.ShapeDtypeStruct((s, D_MODEL), F32),
        scratch_shapes=[pltpu.VMEM((tm, D_MODEL), BF16)],
        compiler_params=_cparams("parallel", "arbitrary"),
        name="dense_ffn",
    )(h, a, b, wo, g, wg, wu, wd)


def _row_copy(src, dst, sem):
    return pltpu.make_async_copy(src, dst, sem)


def _expert_ffn_kernel(te_ref, nu_ref, src_ref, x_hbm, wg_ref, wu_ref, wd_ref, o_ref, xbuf_ref, xb_ref, a_ref, sem,
                       *, n_f):
    r = pl.program_id(0)
    f = pl.program_id(1)
    tm = o_ref.shape[0]
    n_used = nu_ref[0]

    def tile_start(tile, slot):
        for t in range(tm):
            _row_copy(x_hbm.at[pl.ds(src_ref[tile * tm + t], 1)], xbuf_ref.at[slot, pl.ds(t, 1)],
                      sem.at[slot]).start()

    def tile_wait(slot):
        _row_copy(x_hbm.at[pl.ds(0, tm)], xbuf_ref.at[slot], sem.at[slot]).wait()

    @pl.when(r < n_used)
    def _():
        @pl.when(jnp.logical_and(r == 0, f == 0))
        def _():
            def first(t, carry):
                _row_copy(x_hbm.at[pl.ds(src_ref[t], 1)], xbuf_ref.at[0, pl.ds(t, 1)], sem.at[0]).start()
                return carry

            lax.fori_loop(0, tm, first, 0)

        for slot in range(2):
            @pl.when(jnp.logical_and(f == 0, r % 2 == slot))
            def _():
                tile_wait(slot)
                xb_ref[...] = xbuf_ref[slot].astype(BF16)
                o_ref[...] = jnp.zeros(o_ref.shape, F32)
                tile_start(jnp.minimum(r + 1, pl.num_programs(0) - 1), 1 - slot)

        xb = xb_ref[...]
        for c in range(a_ref.shape[1] // FFN_COLS):
            cs = slice(c * FFN_COLS, (c + 1) * FFN_COLS)
            a_ref[:, cs] = (jax.nn.silu(_dot(xb, wg_ref[:, cs])) * _dot(xb, wu_ref[:, cs])).astype(BF16)
        o_ref[...] += _dot(a_ref[...], wd_ref[...])

        for slot in range(2):
            @pl.when(jnp.logical_and(jnp.logical_and(r == n_used - 1, f == n_f - 1), r % 2 == slot))
            def _():
                tile_wait(1 - slot)

    @pl.when(jnp.logical_and(r >= n_used, f == 0))
    def _():
        o_ref[...] = jnp.zeros(o_ref.shape, F32)


def _expert_ffn(tile_expert, n_used, src, xn, wg, wu, wd, f_tile):
    rows = src.shape[0]
    tm = MOE_TILE
    ffn = wg.shape[2]
    grid_spec = pltpu.PrefetchScalarGridSpec(
        num_scalar_prefetch=3,
        grid=(rows // tm, ffn // f_tile),
        in_specs=[pl.BlockSpec(memory_space=pl.ANY),
                  pl.BlockSpec((None, D_MODEL, f_tile),
                               lambda r, f, te, nu, sr: (te[r], 0, jnp.where(r < nu[0], f, 0))),
                  pl.BlockSpec((None, D_MODEL, f_tile),
                               lambda r, f, te, nu, sr: (te[r], 0, jnp.where(r < nu[0], f, 0))),
                  pl.BlockSpec((None, f_tile, D_MODEL),
                               lambda r, f, te, nu, sr: (te[r], jnp.where(r < nu[0], f, 0), 0))],
        out_specs=pl.BlockSpec((tm, D_MODEL), lambda r, f, te, nu, sr: (r, 0)),
        scratch_shapes=[pltpu.VMEM((2, tm, D_MODEL), F32), pltpu.VMEM((tm, D_MODEL), BF16),
                        pltpu.VMEM((tm, f_tile), BF16), pltpu.SemaphoreType.DMA((2,))],
    )
    return pl.pallas_call(
        functools.partial(_expert_ffn_kernel, n_f=ffn // f_tile),
        grid_spec=grid_spec,
        out_shape=jax.ShapeDtypeStruct((rows, D_MODEL), F32),
        compiler_params=_cparams("arbitrary", "arbitrary"),
        name="expert_ffn",
    )(tile_expert, n_used, src, xn, wg, wu, wd)


def _odd_in_kernel(h_ref, g_ref, w_ref, wkpe_ref, cos_ref, sin_ref, pw_ref, ps_ref,
                   qg_ref, wuq_ref, kvg_ref, wk_ref, wv_ref,
                   oc_ref, q_ref, k_ref, v_ref, halo_ref):
    tm = h_ref.shape[0]
    i = pl.program_id(0)
    xn = _rms(h_ref[...], g_ref[...]).astype(BF16)
    cosf = cos_ref[...]
    sins = sin_ref[...]
    lane = lax.broadcasted_iota(jnp.int32, (tm, LANES), 1)
    lo_mask = lane < (D_NOPE + D_ROPE // 2)

    @pl.when(i == 0)
    def _():
        halo_ref[...] = jnp.zeros(halo_ref.shape, F32)

    xc = _dot(xn, w_ref[:, 0:C_WIDTH])
    ext = jnp.concatenate([halo_ref[...], xc], axis=0)
    halo_ref[...] = xc[tm - C_HALO:tm, :]
    pos = i * tm + lax.broadcasted_iota(jnp.int32, (tm, 1), 0)
    gc = C_WIDTH // len(C_WINDOWS)
    for g, win in enumerate(C_WINDOWS):
        cs = slice(g * gc, (g + 1) * gc)
        a = ext[:, cs]
        span = 1
        while span < win:
            n = a.shape[0] - span
            a = a[span:span + n, :] + a[0:n, :]
            span *= 2
        start = C_HALO - (win - 1)
        wsum = a[start:start + tm, :]
        cnt = jnp.minimum(pos + 1, win).astype(F32)
        pooled = (wsum / cnt - xc[:, cs]).astype(BF16)
        oc_ref[:, cs] = (_dot(pooled, pw_ref[g]) * ps_ref[:, cs]).astype(BF16)

    cq = _dot(xn, w_ref[:, C_WIDTH:C_WIDTH + D_Q_RANK])
    q = _dot(_rms(cq, qg_ref[...]).astype(BF16), wuq_ref[...])
    q = _rope_wide(q, cosf, sins, D_ROPE // 2, lo_mask) * ((D_NOPE + D_ROPE) ** -0.5 * LOG2E)
    q_ref[...] = q.astype(BF16)

    ckv = _dot(xn, w_ref[:, C_WIDTH + D_Q_RANK:C_WIDTH + D_Q_RANK + D_KV_RANK])
    ckvn = _rms(ckv, kvg_ref[...]).astype(BF16)
    kpe = _rope_block(_dot(xn, wkpe_ref[...]), cosf, sins, D_ROPE // 2, lo_mask)
    kn = _dot(ckvn, wk_ref[...])
    k_ref[...] = jnp.concatenate(
        [kn[:, b * LANES:(b + 1) * LANES] + kpe for b in range(D_HEADS)], axis=1).astype(BF16)
    _store_vt(v_ref, _dot(ckvn, wv_ref[...]), D_V)


def _odd_in(h, g, w, wkpe, cosf, sins, pool_w, pool_scale, qg, wuq, kvg, wk, wv):
    s = h.shape[0]
    tm = min(ROW_TILE, s)
    row_spec = lambda width: pl.BlockSpec((tm, width), lambda i: (i, 0))
    full = lambda shape: pl.BlockSpec(shape, lambda i: (0,) * len(shape))
    return pl.pallas_call(
        _odd_in_kernel,
        grid=(s // tm,),
        in_specs=[row_spec(D_MODEL), full((1, D_MODEL)), full(w.shape), full(wkpe.shape),
                  row_spec(LANES), row_spec(LANES), full(pool_w.shape), full((1, C_WIDTH)),
                  full((1, D_Q_RANK)), full(wuq.shape), full((1, D_KV_RANK)), full(wk.shape), full(wv.shape)],
        out_specs=[row_spec(C_WIDTH), row_spec(D_HEADS * D_HEAD_PAD), row_spec(D_HEADS * D_HEAD_PAD),
                   pl.BlockSpec((D_HEADS // 2 * VT_ROWS_D, tm), lambda i: (0, i))],
        out_shape=[jax.ShapeDtypeStruct((s, C_WIDTH), BF16),
                   jax.ShapeDtypeStruct((s, D_HEADS * D_HEAD_PAD), BF16),
                   jax.ShapeDtypeStruct((s, D_HEADS * D_HEAD_PAD), BF16),
                   jax.ShapeDtypeStruct((D_HEADS // 2 * VT_ROWS_D, s), BF16)],
        scratch_shapes=[pltpu.VMEM((C_HALO, C_WIDTH), F32)],
        compiler_params=_cparams("arbitrary"),
        name="odd_in",
    )(h, g, w, wkpe, cosf, sins, pool_w, pool_scale, qg, wuq, kvg, wk, wv)


def _router_kernel(h_ref, a_ref, b_ref, wo_ref, g_ref, rt_ref,
                   h2_ref, xn_ref, eidx_ref, rank_ref, wcol_ref, cnt_ref, run_ref):
    tm = h_ref.shape[0]
    i = pl.program_id(0)

    @pl.when(i == 0)
    def _():
        run_ref[...] = jnp.zeros(run_ref.shape, F32)

    x = _mixer_out(h_ref, a_ref, b_ref, wo_ref)
    h2_ref[...] = x
    xn = _rms(x, g_ref[...])
    xn_ref[...] = xn
    lg = lax.dot_general(rt_ref[...], xn, (((1,), (1,)), ((), ())),
                         preferred_element_type=F32, precision=lax.Precision.HIGHEST)
    eio = lax.broadcasted_iota(jnp.int32, lg.shape, 0)
    m1 = jnp.max(lg, axis=0, keepdims=True)
    i1 = jnp.min(jnp.where(lg == m1, eio, N_EXPERTS), axis=0, keepdims=True)
    lg2 = jnp.where(eio == i1, -jnp.inf, lg)
    m2 = jnp.max(lg2, axis=0, keepdims=True)
    i2 = jnp.min(jnp.where(lg2 == m2, eio, N_EXPERTS), axis=0, keepdims=True)
    e = jnp.exp(m2 - m1)
    w1 = 1.0 / (1.0 + e)
    w2 = e / (1.0 + e)

    sel1 = eio == i1
    sel2 = eio == i2
    onehot = jnp.logical_or(sel1, sel2)
    tr = lax.broadcasted_iota(jnp.int32, (tm, tm), 0)
    tc = lax.broadcasted_iota(jnp.int32, (tm, tm), 1)
    upper = (tr < tc).astype(BF16)
    before = _dot(onehot.astype(BF16), upper) + run_ref[:, 0:1]
    r1 = jnp.sum(jnp.where(sel1, before, 0.0), axis=0, keepdims=True)
    r2 = jnp.sum(jnp.where(sel2, before, 0.0), axis=0, keepdims=True)
    run_new = run_ref[:, 0:1] + jnp.sum(onehot.astype(F32), axis=1, keepdims=True)
    run_ref[...] = jnp.broadcast_to(run_new, run_ref.shape)
    cnt_ref[...] = jnp.broadcast_to(run_new, cnt_ref.shape).astype(jnp.int32)

    eidx_ref[...] = jnp.concatenate([i1, i2], axis=0)
    rank_ref[...] = jnp.concatenate([r1, r2], axis=0).astype(jnp.int32)
    w8 = jnp.concatenate([w1, w2, jnp.zeros((N_EXPERTS - 2, tm), F32)], axis=0)
    wcol_ref[...] = w8.T


def _router(h, a, b, wo, g, router_t):
    s = h.shape[0]
    tm = min(ROW_TILE, s)
    half = a.shape[1]
    return pl.pallas_call(
        _router_kernel,
        grid=(s // tm,),
        in_specs=[pl.BlockSpec((tm, D_MODEL), lambda i: (i, 0)),
                  pl.BlockSpec((tm, half), lambda i: (i, 0)),
                  pl.BlockSpec((tm, half), lambda i: (i, 0)),
                  pl.BlockSpec((2 * half, D_MODEL), lambda i: (0, 0)),
                  pl.BlockSpec((1, D_MODEL), lambda i: (0, 0)),
                  pl.BlockSpec((N_EXPERTS, D_MODEL), lambda i: (0, 0))],
        out_specs=[pl.BlockSpec((tm, D_MODEL), lambda i: (i, 0)),
                   pl.BlockSpec((tm, D_MODEL), lambda i: (i, 0)),
                   pl.BlockSpec((2, tm), lambda i: (0, i)),
                   pl.BlockSpec((2, tm), lambda i: (0, i)),
                   pl.BlockSpec((tm, N_EXPERTS), lambda i: (i, 0)),
                   pl.BlockSpec((N_EXPERTS, LANES), lambda i: (0, 0))],
        out_shape=[jax.ShapeDtypeStruct((s, D_MODEL), F32),
                   jax.ShapeDtypeStruct((s, D_MODEL), F32),
                   jax.ShapeDtypeStruct((2, s), jnp.int32),
                   jax.ShapeDtypeStruct((2, s), jnp.int32),
                   jax.ShapeDtypeStruct((s, N_EXPERTS), F32),
                   jax.ShapeDtypeStruct((N_EXPERTS, LANES), jnp.int32)],
        scratch_shapes=[pltpu.VMEM((N_EXPERTS, LANES), F32)],
        compiler_params=_cparams("arbitrary"),
        name="router",
    )(h, a, b, wo, g, router_t)


def _invert_kernel(gap_ref, pos_ref, src_ref):
    n_tok = pos_ref.shape[0] // 2

    def clear(i, carry):
        src_ref[i] = 0
        return carry

    for g in range(gap_ref.shape[0] // 2):
        lax.fori_loop(gap_ref[2 * g], gap_ref[2 * g + 1], clear, 0)

    def put(t, carry):
        src_ref[pos_ref[t]] = t
        src_ref[pos_ref[n_tok + t]] = t
        return carry

    lax.fori_loop(0, n_tok, put, 0, unroll=8)


def _invert(gaps, pos_flat, n_slots):
    return pl.pallas_call(
        _invert_kernel,
        in_specs=[pl.BlockSpec(memory_space=pltpu.SMEM), pl.BlockSpec(memory_space=pltpu.SMEM)],
        out_specs=pl.BlockSpec(memory_space=pltpu.SMEM),
        out_shape=jax.ShapeDtypeStruct((n_slots,), jnp.int32),
        name="invert",
    )(gaps, pos_flat)


def _combine_kernel(pos_ref, nxt_ref, ys_hbm, h_ref, wcol_ref, fg_ref, o_ref, buf_ref, sem, *, final_norm):
    i = pl.program_id(0)
    tm = h_ref.shape[0]

    def tile_start(idx_ref, slot):
        for t in range(tm):
            for k in range(2):
                _row_copy(ys_hbm.at[pl.ds(idx_ref[k, t], 1)], buf_ref.at[slot, k, pl.ds(t, 1)],
                          sem.at[slot]).start()

    @pl.when(i == 0)
    def _():
        tile_start(pos_ref, 0)

    for slot in range(2):
        @pl.when(jnp.logical_and(i % 2 == slot, i + 1 < pl.num_programs(0)))
        def _():
            tile_start(nxt_ref, 1 - slot)

    for slot in range(2):
        @pl.when(i % 2 == slot)
        def _():
            for k in range(2):
                _row_copy(ys_hbm.at[pl.ds(0, tm)], buf_ref.at[slot, k], sem.at[slot]).wait()
            out = h_ref[...] + wcol_ref[:, 0:1] * buf_ref[slot, 0] + wcol_ref[:, 1:2] * buf_ref[slot, 1]
            if final_norm:
                out = _rms(out, fg_ref[...])
            o_ref[...] = out


def _combine(pos, ys, h, wcol, final_g, final_norm):
    s = h.shape[0]
    tm = min(ROW_TILE, s)
    last = s // tm - 1
    return pl.pallas_call(
        functools.partial(_combine_kernel, final_norm=final_norm),
        grid=(s // tm,),
        in_specs=[pl.BlockSpec((2, tm), lambda i: (0, i), memory_space=pltpu.SMEM),
                  pl.BlockSpec((2, tm), lambda i: (0, jnp.minimum(i + 1, last)), memory_space=pltpu.SMEM),
                  pl.BlockSpec(memory_space=pl.ANY),
                  pl.BlockSpec((tm, D_MODEL), lambda i: (i, 0)),
                  pl.BlockSpec((tm, N_EXPERTS), lambda i: (i, 0)),
                  pl.BlockSpec((1, D_MODEL), lambda i: (0, 0))],
        out_specs=pl.BlockSpec((tm, D_MODEL), lambda i: (i, 0)),
        out_shape=jax.ShapeDtypeStruct((s, D_MODEL), F32),
        scratch_shapes=[pltpu.VMEM((2, 2, tm, D_MODEL), F32), pltpu.SemaphoreType.DMA((2,))],
        compiler_params=_cparams("arbitrary"),
        name="combine",
    )(pos, pos, ys, h, wcol, final_g)


def _final_norm_kernel(h_ref, g_ref, o_ref):
    o_ref[...] = _rms(h_ref[...], g_ref[...])


def _final_norm(h, g):
    s = h.shape[0]
    tm = min(ROW_TILE, s)
    return pl.pallas_call(
        _final_norm_kernel,
        grid=(s // tm,),
        in_specs=[pl.BlockSpec((tm, D_MODEL), lambda i: (i, 0)), pl.BlockSpec((1, D_MODEL), lambda i: (0, 0))],
        out_specs=pl.BlockSpec((tm, D_MODEL), lambda i: (i, 0)),
        out_shape=jax.ShapeDtypeStruct((s, D_MODEL), F32),
        compiler_params=_cparams("parallel"),
        name="final_norm",
    )(h, g)


def _rope_lane_tables(seq, rot_dim, first_lane, period):
    half = rot_dim // 2
    pos = jnp.arange(seq, dtype=F32)
    inv = ROPE_THETA ** (-jnp.arange(0, rot_dim, 2, dtype=F32) / rot_dim)
    ang = pos[:, None] * inv[None, :]
    cos, sin = jnp.cos(ang), jnp.sin(ang)
    lane = jnp.arange(LANES) % period - first_lane
    lo = (lane >= 0) & (lane < half)
    hi = (lane >= half) & (lane < rot_dim)
    idx = jnp.clip(jnp.where(hi, lane - half, lane), 0, half - 1)
    cosf = jnp.where((lo | hi)[None, :], cos[:, idx], 1.0)
    sins = jnp.where(lo[None, :], -sin[:, idx], jnp.where(hi[None, :], sin[:, idx], 0.0))
    return cosf, sins


def _even_layer(h, l, p, i, rope_a):
    w_in = p["w_in_even"][i].astype(BF16)
    q, k, v, ob, ffn_w = _even_in(
        h, p["norm_mix_even"][i][None, :], w_in, rope_a[0], rope_a[1],
        p["sgu_ln_g"][i][None, :], p["sgu_ln_b"][i][None, :], p["sgu_w"][i], p["sgu_b"][i].T,
        (p["ffn_wg"], p["ffn_wu"], p["ffn_wd"]), i)
    lam_init = 0.8 - 0.6 * math.exp(-0.3 * l)
    oa, moe_w = _diff_attn(q, k, v, p["lam_q1"][i][None, :], p["lam_k1"][i][None, :],
                           p["lam_q2"][i][None, :], p["lam_k2"][i][None, :], p["subln_g"][i][:, None], lam_init,
                           (p["moe_wg"], p["moe_wu"], p["moe_wd"]), min(i, p["moe_wg"].shape[0] - 1))
    ffn = p["ffn_wg"].shape[2]
    h = _dense_ffn(h, oa, ob, p["w_out_even"][i].astype(BF16), p["norm_ffn_even"][i][None, :],
                   ffn_w[0], ffn_w[1], ffn_w[2], ffn)
    return h, moe_w


def _odd_layer(h, p, i, moe_w, rope_d, final_g, final_norm):
    s = h.shape[0]
    w_in = p["w_in_odd"][i]
    n_main = C_WIDTH + D_Q_RANK + D_KV_RANK
    w_main = w_in[:, :n_main].astype(BF16)
    wkpe = jnp.zeros((D_MODEL, LANES), F32).at[:, D_NOPE:D_NOPE + D_ROPE].set(w_in[:, n_main:]).astype(BF16)
    wuq = jnp.pad(p["w_uq"][i].reshape(D_Q_RANK, D_HEADS, D_NOPE + D_ROPE),
                  ((0, 0), (0, 0), (0, D_HEAD_PAD - D_NOPE - D_ROPE))).reshape(D_Q_RANK, -1).astype(BF16)
    wukv = p["w_ukv"][i].reshape(D_KV_RANK, D_HEADS, D_NOPE + D_V)
    wk = jnp.pad(wukv[:, :, :D_NOPE], ((0, 0), (0, 0), (0, D_HEAD_PAD - D_NOPE))).reshape(D_KV_RANK, -1).astype(BF16)
    wv = wukv[:, :, D_NOPE:].reshape(D_KV_RANK, -1).astype(BF16)
    oc, q, k, v = _odd_in(
        h, p["norm_mix_odd"][i][None, :], w_main, wkpe, rope_d[0], rope_d[1],
        p["pool_w"][i].astype(BF16), p["pool_scale"][i][None, :],
        p["q_norm_g"][i][None, :], wuq, p["kv_norm_g"][i][None, :], wk, wv)
    od = _mla_attn(q, k, v)

    h, xn, eidx, rank, wcol, cnt = _router(h, oc, od, p["w_out_odd"][i].astype(BF16),
                                           p["norm_ffn_odd"][i][None, :], p["router"][i].T)
    counts = cnt[:, 0]
    padded = (counts + MOE_TILE - 1) // MOE_TILE * MOE_TILE
    ends = jnp.cumsum(padded)
    base = ends - padded
    pos = rank
    for e in range(N_EXPERTS):
        pos = pos + jnp.where(eidx == e, base[e], 0)
    n_tiles = (2 * s) // MOE_TILE + N_EXPERTS
    tile_start = jnp.arange(n_tiles, dtype=jnp.int32) * MOE_TILE
    tile_expert = jnp.minimum(jnp.sum(tile_start[:, None] >= ends[None, :], axis=1), N_EXPERTS - 1).astype(jnp.int32)
    n_used = (ends[-1:] // MOE_TILE).astype(jnp.int32)
    n_slots = n_tiles * MOE_TILE
    gaps = jnp.stack([jnp.append(base + counts, ends[-1]), jnp.append(ends, n_slots)], axis=1)
    src = _invert(gaps.reshape(-1).astype(jnp.int32), pos.reshape(-1), n_slots)
    ys = _expert_ffn(tile_expert, n_used, src, xn, moe_w[0], moe_w[1], moe_w[2], MOE_F_TILE)
    return _combine(pos, ys, h, wcol, final_g, final_norm)


def kernel(x, norm_mix_even, w_in_even, lam_q1, lam_k1, lam_q2, lam_k2, subln_g, sgu_ln_g, sgu_ln_b, sgu_w, sgu_b, w_out_even, norm_ffn_even, ffn_wg, ffn_wu, ffn_wd, norm_mix_odd, w_in_odd, pool_w, pool_scale, q_norm_g, w_uq, kv_norm_g, w_ukv, w_out_odd, norm_ffn_odd, router, moe_wg, moe_wu, moe_wd, final_norm):
    p = dict(norm_mix_even=norm_mix_even, w_in_even=w_in_even, lam_q1=lam_q1, lam_k1=lam_k1, lam_q2=lam_q2,
             lam_k2=lam_k2, subln_g=subln_g, sgu_ln_g=sgu_ln_g, sgu_ln_b=sgu_ln_b, sgu_w=sgu_w, sgu_b=sgu_b,
             w_out_even=w_out_even, norm_ffn_even=norm_ffn_even, ffn_wg=ffn_wg, ffn_wu=ffn_wu, ffn_wd=ffn_wd,
             norm_mix_odd=norm_mix_odd, w_in_odd=w_in_odd, pool_w=pool_w, pool_scale=pool_scale,
             q_norm_g=q_norm_g, w_uq=w_uq, kv_norm_g=kv_norm_g, w_ukv=w_ukv, w_out_odd=w_out_odd,
             norm_ffn_odd=norm_ffn_odd, router=router, moe_wg=moe_wg, moe_wu=moe_wu, moe_wd=moe_wd)
    b, s, d = x.shape
    depth = norm_mix_even.shape[0] + norm_mix_odd.shape[0]
    rope_a = _rope_lane_tables(s, A_ROT, 0, A_QK_DIM)
    rope_d = _rope_lane_tables(s, D_ROPE, D_NOPE, LANES)
    final_g = final_norm[None, :]
    outs = []
    for bi in range(b):
        h = x[bi]
        for l in range(depth):
            if l % 2 == 0:
                h, moe_w = _even_layer(h, l, p, l // 2, rope_a)
            else:
                h = _odd_layer(h, p, l // 2, moe_w, rope_d, final_g, final_norm=(l == depth - 1))
        if depth % 2 == 1:
            h = _final_norm(h, final_g)
        outs.append(h)
    return jnp.stack(outs)
```
